```python
import math
import jax
import jax.numpy as jnp
from jax import lax
import numpy as np

D_MODEL = 1024
BATCH = 4
SEQ = 8192
DEPTH = 2

GRID_W = 64
CTX_LEN = 256
N_EVEN = (DEPTH + 1) // 2
N_ODD = DEPTH // 2
EPS = 1e-6
ROPE_THETA = 10000.0
CONV_K = 5
Q_BLOCK = 128
DN_HEADS = 8
DN_DK = 128
DN_DV = 128
DN_CHUNK = 64
SSD_HEADS = 16
SSD_HEAD_DIM = 64
SSD_GROUPS = 2
SSD_STATE = 128
SSD_CHUNK = 128
SSD_INNER = SSD_HEADS * SSD_HEAD_DIM
DA_HEADS = 8
DA_DIM = 64
MLA_HEADS = 8
MLA_Q_RANK = 512
MLA_KV_RANK = 256
MLA_NOPE = 128
MLA_ROPE = 64
MLA_V = 128
FFN_HIDDEN = ((8 * D_MODEL + 3 * 256 - 1) // (3 * 256)) * 256

DN_QK = DN_HEADS * DN_DK
DN_VW = DN_HEADS * DN_DV
SSD_BC = SSD_GROUPS * SSD_STATE
EV_CONV = 2 * DN_QK + DN_VW + SSD_INNER + 2 * SSD_BC
EV_GATE = DN_VW + SSD_INNER
EV_SMALL = 4 * DN_HEADS + 2 * SSD_HEADS
EV_IN = EV_CONV + EV_GATE + EV_SMALL
EV_CONV_SPLITS = [DN_QK, 2 * DN_QK, 2 * DN_QK + DN_VW, 2 * DN_QK + DN_VW + SSD_INNER, 2 * DN_QK + DN_VW + SSD_INNER + SSD_BC]
EV_MIX = DN_VW + SSD_INNER
DA_QK = DA_HEADS * 2 * DA_DIM
DA_VW = DA_HEADS * 2 * DA_DIM
OD_IN = 2 * DA_QK + DA_VW + MLA_Q_RANK + MLA_KV_RANK + MLA_ROPE
OD_SPLITS = [DA_QK, 2 * DA_QK, 2 * DA_QK + DA_VW, 2 * DA_QK + DA_VW + MLA_Q_RANK, 2 * DA_QK + DA_VW + MLA_Q_RANK + MLA_KV_RANK]
OD_MIX = DA_VW + MLA_HEADS * MLA_V

kernel_name = "hybrid_dit_deltanet_ssd_diffattn_mla"


def rms_norm(x, g):
    xf = x.astype(jnp.float32)
    y = xf * lax.rsqrt(jnp.mean(xf * xf, axis=-1, keepdims=True) + EPS)
    return (y * g.astype(jnp.float32)).astype(x.dtype)


def l2_norm(x):
    xf = x.astype(jnp.float32)
    return xf * lax.rsqrt(jnp.sum(xf * xf, axis=-1, keepdims=True) + EPS)


def swiglu(h, w_in, w_out):
    gate, up = jnp.split(h @ w_in, 2, axis=-1)
    return (jax.nn.silu(gate) * up) @ w_out


def centred_depthwise_conv_silu(u, w, b):
    y = lax.conv_general_dilated(u, w[:, None, :].astype(u.dtype), window_strides=(1,), padding=[(CONV_K // 2, CONV_K // 2)], dimension_numbers=("NWC", "WIO", "NWC"), feature_group_count=u.shape[-1])
    return jax.nn.silu(y + b.astype(u.dtype))


def axial_rope_tables(n_tokens, rot_dim):
    rows = n_tokens // GRID_W
    row = jnp.broadcast_to(jnp.arange(rows, dtype=jnp.float32)[:, None], (rows, GRID_W)).reshape(-1)
    col = jnp.broadcast_to(jnp.arange(GRID_W, dtype=jnp.float32)[None, :], (rows, GRID_W)).reshape(-1)
    n_freq = rot_dim // 4
    inv_freq = ROPE_THETA ** (-jnp.arange(n_freq, dtype=jnp.float32) / n_freq)
    ang = jnp.concatenate([row[:, None] * inv_freq, col[:, None] * inv_freq], axis=-1)
    return jnp.cos(ang), jnp.sin(ang)


def apply_axial_rope(x, cos, sin):
    r4 = x.shape[-1] // 4
    xr1, xr2, xc1, xc2 = jnp.split(x.astype(jnp.float32), 4, axis=-1)
    x1 = jnp.concatenate([xr1, xc1], axis=-1)
    x2 = jnp.concatenate([xr2, xc2], axis=-1)
    shp = (cos.shape[0],) + (1,) * (x.ndim - 3) + (cos.shape[-1],)
    c = cos.reshape(shp)
    s = sin.reshape(shp)
    y1 = x1 * c - x2 * s
    y2 = x2 * c + x1 * s
    return jnp.concatenate([y1[..., :r4], y2[..., :r4], y1[..., r4:], y2[..., r4:]], axis=-1).astype(x.dtype)


def seq_to_chunks(a, size):
    b, n = a.shape[:2]
    return jnp.moveaxis(a.reshape(b, n // size, size, *a.shape[2:]), 1, 0)


def chunks_to_seq(a):
    nc, b, size = a.shape[:3]
    return jnp.moveaxis(a, 0, 1).reshape(b, nc * size, *a.shape[3:])


def heads_first_chunks(a, size):
    b, n, h = a.shape[:3]
    a = a.reshape(b, n // size, size, h, *a.shape[3:])
    return jnp.moveaxis(jnp.swapaxes(a, 2, 3), 1, 0)


def gated_delta_scan(q, k, v, g, beta, state0):
    f32 = jnp.float32
    bsz, seqlen, heads, _ = q.shape
    c = DN_CHUNK
    causal = jnp.tril(jnp.ones((c, c), dtype=bool))
    strict = jnp.tril(jnp.ones((c, c), dtype=bool), -1)
    eye = jnp.eye(c, dtype=f32)

    def body(s, xs):
        qc, kc, vc, gch, bc = xs
        gc = jnp.cumsum(gch, axis=-1)
        decay = jnp.exp(jnp.where(causal, gc[..., :, None] - gc[..., None, :], -jnp.inf))
        kb = kc * bc[..., None]
        a = jnp.where(strict, jnp.einsum("bhid,bhjd->bhij", kb, kc) * decay, 0.0)
        t = lax.linalg.triangular_solve(a + eye, jnp.broadcast_to(eye, a.shape), left_side=True, lower=True, unit_diagonal=True)
        u = t @ (vc * bc[..., None])
        w = t @ (kb * jnp.exp(gc)[..., None])
        v_new = u - w @ s
        attn = jnp.einsum("bhid,bhjd->bhij", qc, kc) * decay
        o = (qc * jnp.exp(gc)[..., None]) @ s + attn @ v_new
        s = s * jnp.exp(gc[..., -1])[..., None, None] + jnp.einsum("bhcd,bhce->bhde", kc * jnp.exp(gc[..., -1:] - gc)[..., None], v_new)
        return s, o

    xs = tuple(heads_first_chunks(t.astype(f32), c) for t in (q * DN_DK ** -0.5, k, v, g, beta))
    s, o = lax.scan(body, state0.astype(f32), xs)
    o = jnp.swapaxes(jnp.moveaxis(o, 0, 1), 2, 3).reshape(bsz, seqlen, heads, -1)
    return o, s


def ssd_scan(x, dt, a, bm, cm, state0):
    f32 = jnp.float32
    c = SSD_CHUNK
    causal = jnp.tril(jnp.ones((c, c), dtype=bool))[None, :, :, None, None]

    def body(h, xs):
        xc, dtc, ac, bc, cc = xs
        cum = jnp.cumsum(ac, axis=1)
        seg = jnp.exp(jnp.where(causal, cum[:, :, None] - cum[:, None, :], -jnp.inf))
        xdt = xc * dtc[..., None]
        mix = jnp.einsum("blgn,bsgn->blsg", cc, bc)[..., None] * seg
        y = jnp.einsum("blsgr,bsgrp->blgrp", mix, xdt) + jnp.einsum("blgn,bgrnp->blgrp", cc, h) * jnp.exp(cum)[..., None]
        h = h * jnp.exp(cum[:, -1])[..., None, None] + jnp.einsum("blgn,blgrp->bgrnp", bc, xdt * jnp.exp(cum[:, -1:] - cum)[..., None])
        return h, y

    xs = tuple(seq_to_chunks(t.astype(f32), c) for t in (x, dt, a, bm, cm))
    h, y = lax.scan(body, state0.astype(f32), xs)
    return chunks_to_seq(y), h


def bidirectional_prefix_scan(scan_fn, ctx_fwd, ctx_bwd, lat_fwd, lat_bwd, state0):
    rev = lambda args: tuple(jnp.flip(t, axis=1) for t in args)
    y_cf, s_f = scan_fn(*ctx_fwd, state0)
    y_cb, s_b = scan_fn(*rev(ctx_bwd), state0)
    y_lf, _ = scan_fn(*lat_fwd, s_f)
    y_lb, _ = scan_fn(*rev(lat_bwd), s_b)
    return y_cf + jnp.flip(y_cb, axis=1), y_lf + jnp.flip(y_lb, axis=1)


def sweep_query_blocks(fn, q):
    bsz, n = q.shape[:2]
    qb = jnp.moveaxis(q.reshape(bsz, n // Q_BLOCK, Q_BLOCK, *q.shape[2:]), 1, 0)
    out = lax.map(fn, qb)
    return jnp.moveaxis(out, 0, 1).reshape(bsz, n, *out.shape[3:])


def diff_attention(q, k, v, lam):
    kf = k.astype(jnp.float32)
    vf = v.astype(jnp.float32)
    scale = DA_DIM ** -0.5

    def block(qb):
        s = jnp.einsum("bqhcd,bkhcd->bchqk", qb.astype(jnp.float32) * scale, kf)
        p = jax.nn.softmax(s, axis=-1)
        return jnp.einsum("bhqk,bkhd->bqhd", p[:, 0] - lam * p[:, 1], vf)

    return sweep_query_blocks(block, q)


def softmax_attention(q, k, v):
    kf = k.astype(jnp.float32)
    vf = v.astype(jnp.float32)
    scale = q.shape[-1] ** -0.5

    def block(qb):
        s = jnp.einsum("bqhd,bkhd->bhqk", qb.astype(jnp.float32) * scale, kf)
        return jnp.einsum("bhqk,bkhd->bqhd", jax.nn.softmax(s, axis=-1), vf)

    return sweep_query_blocks(block, q)


def even_mixer(h_lat, h_ctx, w_in, conv_w, conv_b, dn_a_log, dn_dt_bias, dn_norm, ssd_a_log, ssd_dt_bias, ssd_d, ssd_norm, w_out, with_ctx_out):
    f32 = jnp.float32
    grp = SSD_HEADS // SSD_GROUPS

    def prepare(h):
        bsz, n = h.shape[:2]
        p = h @ w_in
        u = centred_depthwise_conv_silu(p[..., :EV_CONV], conv_w, conv_b)
        z = p[..., EV_CONV:EV_CONV + EV_GATE]
        small = p[..., EV_CONV + EV_GATE:].astype(f32)
        q, k, v, xs, bm, cm = jnp.split(u, EV_CONV_SPLITS, axis=-1)
        q = l2_norm(q.reshape(bsz, n, DN_HEADS, DN_DK))
        k = l2_norm(k.reshape(bsz, n, DN_HEADS, DN_DK))
        v = v.reshape(bsz, n, DN_HEADS, DN_DV)
        b_raw, a_raw, dt_raw = jnp.split(small, [2 * DN_HEADS, 4 * DN_HEADS], axis=-1)
        beta = jax.nn.sigmoid(b_raw).reshape(bsz, n, 2, DN_HEADS)
        g = -jnp.exp(dn_a_log.astype(f32)) * jax.nn.softplus(a_raw.reshape(bsz, n, 2, DN_HEADS) + dn_dt_bias.astype(f32))
        dt = jax.nn.softplus(dt_raw.reshape(bsz, n, 2, SSD_HEADS) + ssd_dt_bias.astype(f32))
        a = dt * -jnp.exp(ssd_a_log.astype(f32))
        dt = dt.reshape(bsz, n, 2, SSD_GROUPS, grp)
        a = a.reshape(bsz, n, 2, SSD_GROUPS, grp)
        x_ssd = xs.reshape(bsz, n, SSD_GROUPS, grp, SSD_HEAD_DIM)
        bm = bm.reshape(bsz, n, SSD_GROUPS, SSD_STATE)
        cm = cm.reshape(bsz, n, SSD_GROUPS, SSD_STATE)
        dn_dirs = [(q, k, v, g[:, :, d], beta[:, :, d]) for d in range(2)]
        ssd_dirs = [(x_ssd, dt[:, :, d], a[:, :, d], bm, cm) for d in range(2)]
        return dn_dirs, ssd_dirs, x_ssd, z

    def merge(o_dn, y_ssd, x_ssd, z):
        bsz, n = z.shape[:2]
        z_dn, z_ssd = jnp.split(z.astype(f32), [DN_VW], axis=-1)
        o_dn = rms_norm(o_dn, dn_norm).reshape(bsz, n, DN_VW) * jax.nn.silu(z_dn)
        y = (y_ssd + ssd_d.astype(f32).reshape(SSD_GROUPS, grp, 1) * x_ssd).reshape(bsz, n, SSD_INNER) * jax.nn.silu(z_ssd)
        y = rms_norm(y.reshape(bsz, n, SSD_GROUPS, SSD_INNER // SSD_GROUPS), ssd_norm.reshape(SSD_GROUPS, -1)).reshape(bsz, n, SSD_INNER)
        return jnp.concatenate([o_dn, y], axis=-1).astype(z.dtype) @ w_out

    dn_c, ssd_c, x_c, z_c = prepare(h_ctx)
    dn_l, ssd_l, x_l, z_l = prepare(h_lat)
    bsz = h_lat.shape[0]
    dn_s0 = jnp.zeros((bsz, DN_HEADS, DN_DK, DN_DV), f32)
    ssd_s0 = jnp.zeros((bsz, SSD_GROUPS, grp, SSD_STATE, SSD_HEAD_DIM), f32)
    o_dn_c, o_dn_l = bidirectional_prefix_scan(gated_delta_scan, dn_c[0], dn_c[1], dn_l[0], dn_l[1], dn_s0)
    y_ssd_c, y_ssd_l = bidirectional_prefix_scan(ssd_scan, ssd_c[0], ssd_c[1], ssd_l[0], ssd_l[1], ssd_s0)
    out_lat = merge(o_dn_l, y_ssd_l, x_l, z_l)
    out_ctx = merge(o_dn_c, y_ssd_c, x_c, z_c) if with_ctx_out else None
    return out_lat, out_ctx


def odd_mixer(h_lat, h_ctx, w_in, da_q_norm, da_k_norm, da_lambda, da_sub_norm, lambda_init, mla_q_a_norm, mla_w_uq, mla_kv_a_norm, mla_w_ukv, mla_q_norm, mla_k_norm, w_out, with_ctx_out):
    def prepare(h, rope, with_queries):
        bsz, n = h.shape[:2]
        dq, dk, dv, cq, ckv, kr = jnp.split(h @ w_in, OD_SPLITS, axis=-1)
        dk = rms_norm(dk.reshape(bsz, n, DA_HEADS, 2, DA_DIM), da_k_norm)
        dv = dv.reshape(bsz, n, DA_HEADS, 2 * DA_DIM)
        kv = (rms_norm(ckv, mla_kv_a_norm) @ mla_w_ukv).reshape(bsz, n, MLA_HEADS, MLA_NOPE + MLA_V)
        k_nope, v = jnp.split(kv, [MLA_NOPE], axis=-1)
        k_nope = rms_norm(k_nope, mla_k_norm[:MLA_NOPE])
        kr = rms_norm(kr, mla_k_norm[MLA_NOPE:])
        if rope is not None:
            dk = apply_axial_rope(dk, *rope[0])
            kr = apply_axial_rope(kr, *rope[1])
        k = jnp.concatenate([k_nope, jnp.broadcast_to(kr[:, :, None, :], (bsz, n, MLA_HEADS, MLA_ROPE))], axis=-1)
        keys = (dk, dv, k, v)
        if not with_queries:
            return None, keys
        dq = rms_norm(dq.reshape(bsz, n, DA_HEADS, 2, DA_DIM), da_q_norm)
        q = (rms_norm(cq, mla_q_a_norm) @ mla_w_uq).reshape(bsz, n, MLA_HEADS, MLA_NOPE + MLA_ROPE)
        q_nope = rms_norm(q[..., :MLA_NOPE], mla_q_norm[:MLA_NOPE])
        q_rope = rms_norm(q[..., MLA_NOPE:], mla_q_norm[MLA_NOPE:])
        if rope is not None:
            dq = apply_axial_rope(dq, *rope[0])
            q_rope = apply_axial_rope(q_rope, *rope[1])
        return (dq, jnp.concatenate([q_nope, q_rope], axis=-1)), keys

    lam_p = da_lambda.astype(jnp.float32)
    lam = jnp.exp(jnp.sum(lam_p[0] * lam_p[1])) - jnp.exp(jnp.sum(lam_p[2] * lam_p[3])) + lambda_init

    def attend(queries, keys):
        dq, q = queries
        dk, dv, k, v = keys
        bsz, n = q.shape[:2]
        da = rms_norm(diff_attention(dq, dk, dv, lam), da_sub_norm) * (1.0 - lambda_init)
        ml = softmax_attention(q, k, v)
        merged = jnp.concatenate([da.reshape(bsz, n, DA_VW), ml.reshape(bsz, n, MLA_HEADS * MLA_V)], axis=-1)
        return merged.astype(h_lat.dtype) @ w_out

    n_lat = h_lat.shape[1]
    rope = (axial_rope_tables(n_lat, DA_DIM), axial_rope_tables(n_lat, MLA_ROPE))
    q_lat, keys_lat = prepare(h_lat, rope, True)
    q_ctx, keys_ctx = prepare(h_ctx, None, with_ctx_out)
    keys_all = tuple(jnp.concatenate([kl, kc], axis=1) for kl, kc in zip(keys_lat, keys_ctx))
    out_lat = attend(q_lat, keys_all)
    out_ctx = attend(q_ctx, keys_ctx) if with_ctx_out else None
    return out_lat, out_ctx


def setup_inputs(seed: int = 0) -> dict:
    key = jax.random.key(seed)
    ks = iter(jax.random.split(key, 40))
    f32 = jnp.float32

    def nrm(shape, scale):
        return scale * jax.random.normal(next(ks), shape, f32)

    def gain(shape):
        return 1.0 + 0.02 * jax.random.normal(next(ks), shape, f32)

    def a_log(shape):
        return jnp.log(jax.random.uniform(next(ks), shape, f32, 1.0, 16.0))

    def dt_bias(shape):
        u = jax.random.uniform(next(ks), shape, f32)
        dt = jnp.exp(u * (math.log(0.1) - math.log(1e-3)) + math.log(1e-3))
        return dt + jnp.log(-jnp.expm1(-dt))

    d = D_MODEL
    return {
        "x": nrm((BATCH, SEQ, d), 1.0),
        "c": nrm((BATCH, d), 1.0),
        "ctx": nrm((BATCH, CTX_LEN, d), 1.0),
        "c_ctx": nrm((d,), 1.0),
        "ada_w": nrm((DEPTH, d, 6 * d), d ** -0.5),
        "ada_b": nrm((DEPTH, 6 * d), 0.02),
        "norm_mix": gain((DEPTH, d)),
        "norm_ffn": gain((DEPTH, d)),
        "ffn_w_in": nrm((DEPTH, d, 2 * FFN_HIDDEN), d ** -0.5),
        "ffn_w_out": nrm((DEPTH, FFN_HIDDEN, d), FFN_HIDDEN ** -0.5),
        "ev_w_in": nrm((N_EVEN, d, EV_IN), d ** -0.5),
        "ev_conv_w": nrm((N_EVEN, CONV_K, EV_CONV), CONV_K ** -0.5),
        "ev_conv_b": nrm((N_EVEN, EV_CONV), 0.02),
        "dn_a_log": a_log((N_EVEN, 2, DN_HEADS)),
        "dn_dt_bias": dt_bias((N_EVEN, 2, DN_HEADS)),
        "dn_norm": gain((N_EVEN, DN_DV)),
        "ssd_a_log": a_log((N_EVEN, 2, SSD_HEADS)),
        "ssd_dt_bias": dt_bias((N_EVEN, 2, SSD_HEADS)),
        "ssd_d": gain((N_EVEN, SSD_HEADS)),
        "ssd_norm": gain((N_EVEN, SSD_INNER)),
        "ev_w_out": nrm((N_EVEN, EV_MIX, d), EV_MIX ** -0.5),
        "od_w_in": nrm((N_ODD, d, OD_IN), d ** -0.5),
        "da_q_norm": gain((N_ODD, DA_DIM)),
        "da_k_norm": gain((N_ODD, DA_DIM)),
        "da_lambda": nrm((N_ODD, 4, DA_DIM), 0.1),
        "da_sub_norm": gain((N_ODD, 2 * DA_DIM)),
        "mla_q_a_norm": gain((N_ODD, MLA_Q_RANK)),
        "mla_w_uq": nrm((N_ODD, MLA_Q_RANK, MLA_HEADS * (MLA_NOPE + MLA_ROPE)), MLA_Q_RANK ** -0.5),
        "mla_kv_a_norm": gain((N_ODD, MLA_KV_RANK)),
        "mla_w_ukv": nrm((N_ODD, MLA_KV_RANK, MLA_HEADS * (MLA_NOPE + MLA_V)), MLA_KV_RANK ** -0.5),
        "mla_q_norm": gain((N_ODD, MLA_NOPE + MLA_ROPE)),
        "mla_k_norm": gain((N_ODD, MLA_NOPE + MLA_ROPE)),
        "od_w_out": nrm((N_ODD, OD_MIX, d), OD_MIX ** -0.5),
    }


def reference(x, c, ctx, c_ctx, ada_w, ada_b, norm_mix, norm_ffn, ffn_w_in, ffn_w_out,
              ev_w_in, ev_conv_w, ev_conv_b, dn_a_log, dn_dt_bias, dn_norm, ssd_a_log, ssd_dt_bias, ssd_d, ssd_norm, ev_w_out,
              od_w_in, da_q_norm, da_k_norm, da_lambda, da_sub_norm, mla_q_a_norm, mla_w_uq, mla_kv_a_norm, mla_w_ukv,
              mla_q_norm, mla_k_norm, od_w_out):
    for i in range(DEPTH):
        last = i == DEPTH - 1
        j = i // 2
        mod_l = (jax.nn.silu(c) @ ada_w[i] + ada_b[i])[:, None, :]
        mod_c = jax.nn.silu(c_ctx) @ ada_w[i] + ada_b[i]
        sh1, sc1, g1, sh2, sc2, g2 = jnp.split(mod_l, 6, axis=-1)
        csh1, csc1, cg1, csh2, csc2, cg2 = jnp.split(mod_c, 6, axis=-1)
        h_lat = rms_norm(x, norm_mix[i]) * (1.0 + sc1) + sh1
        h_ctx = rms_norm(ctx, norm_mix[i]) * (1.0 + csc1) + csh1
        if i % 2 == 0:
            m_lat, m_ctx = even_mixer(h_lat, h_ctx, ev_w_in[j], ev_conv_w[j], ev_conv_b[j], dn_a_log[j], dn_dt_bias[j], dn_norm[j],
                                      ssd_a_log[j], ssd_dt_bias[j], ssd_d[j], ssd_norm[j], ev_w_out[j], not last)
        else:
            lambda_init = 0.8 - 0.6 * math.exp(-0.3 * i)
            m_lat, m_ctx = odd_mixer(h_lat, h_ctx, od_w_in[j], da_q_norm[j], da_k_norm[j], da_lambda[j], da_sub_norm[j], lambda_init,
                                     mla_q_a_norm[j], mla_w_uq[j], mla_kv_a_norm[j], mla_w_ukv[j], mla_q_norm[j], mla_k_norm[j],
                                     od_w_out[j], not last)
        x = x + g1 * m_lat
        x = x + g2 * swiglu(rms_norm(x, norm_ffn[i]) * (1.0 + sc2) + sh2, ffn_w_in[i], ffn_w_out[i])
        if not last:
            ctx = ctx + cg1 * m_ctx
            ctx = ctx + cg2 * swiglu(rms_norm(ctx, norm_ffn[i]) * (1.0 + csc2) + csh2, ffn_w_in[i], ffn_w_out[i])
    return x
```

```python
import functools
import math

import jax
import jax.numpy as jnp
from jax import lax
from jax.experimental import pallas as pl
from jax.experimental.pallas import tpu as pltpu

F32 = jnp.float32
BF16 = jnp.bfloat16

GRID_W = 64
EPS = 1e-6
ROPE_THETA = 10000.0
CONV_K = 5
DN_HEADS = 8
DN_DK = 128
DN_DV = 128
SSD_HEADS = 16
SSD_HEAD_DIM = 64
SSD_GROUPS = 2
SSD_STATE = 128
SSD_GRP = SSD_HEADS // SSD_GROUPS
SSD_INNER = SSD_HEADS * SSD_HEAD_DIM
DA_HEADS = 8
DA_DIM = 64
MLA_HEADS = 8
MLA_Q_RANK = 512
MLA_KV_RANK = 256
MLA_NOPE = 128
MLA_ROPE = 64
MLA_V = 128
MLA_QK_PAD = 256

DN_QK = DN_HEADS * DN_DK
DN_VW = DN_HEADS * DN_DV
SSD_BC = SSD_GROUPS * SSD_STATE
EV_CONV = 2 * DN_QK + DN_VW + SSD_INNER + 2 * SSD_BC
EV_GATE = DN_VW + SSD_INNER
DA_QK = DA_HEADS * 2 * DA_DIM
DA_VW = DA_HEADS * 2 * DA_DIM

LANE = 128
BF16_ROWS = 16
CHUNK = 128
VMEM_LIMIT = 56 * 1024 * 1024

ROW_TILE = 512
SCAN_TILE = 512
ATT_TQ = 512
ATT_TK = 768


def _dot(a, b):
    return jnp.dot(a, b, preferred_element_type=F32)


def _dot_nt(a, b):
    return lax.dot_general(a, b, (((1,), (1,)), ((), ())), preferred_element_type=F32)


def _dot_tn(a, b):
    return lax.dot_general(a, b, (((0,), (0,)), ((), ())), preferred_element_type=F32)


def _sigmoid(x):
    return 1.0 / (1.0 + jnp.exp(-x))


def _silu(x):
    return x * _sigmoid(x)


def _softplus(x):
    return jnp.maximum(x, 0.0) + jnp.log(1.0 + jnp.exp(-jnp.abs(x)))


def _params(*sem):
    return pltpu.CompilerParams(dimension_semantics=sem, vmem_limit_bytes=VMEM_LIMIT)


def _resident(shape):
    nd = len(shape)
    return pl.BlockSpec(shape, lambda *_: (0,) * nd, pipeline_mode=pl.Buffered(1))


def _norm_mod(x, gain, shift, scale):
    y = x * lax.rsqrt(jnp.mean(x * x, axis=-1, keepdims=True) + EPS)
    return (y * gain) * (1.0 + scale) + shift


def _mod_kernel(c_ref, w_ref, b_ref, o_ref):
    o_ref[0] = _dot(_silu(c_ref[...]), w_ref[0]) + b_ref[0]


def _modulation(cvec, ada_w, ada_b):
    depth, d, n = ada_w.shape
    tn = n // 4
    return pl.pallas_call(
        _mod_kernel,
        grid=(depth, n // tn),
        in_specs=[
            pl.BlockSpec((8, d), lambda l, j: (0, 0)),
            pl.BlockSpec((1, d, tn), lambda l, j: (l, 0, j)),
            pl.BlockSpec((1, 1, tn), lambda l, j: (l, 0, j)),
        ],
        out_specs=pl.BlockSpec((1, 8, tn), lambda l, j: (l, 0, j)),
        out_shape=jax.ShapeDtypeStruct((depth, 8, n), F32),
        compiler_params=_params("parallel", "parallel"),
        name="modulation",
    )(cvec, ada_w, ada_b.reshape(depth, 1, n))


def _evin_kernel(xp_ref, xc_ref, xn_ref, mod_ref, gain_ref, w_ref, wsm_ref, cw_ref, cb_ref, gb_ref, gs_ref,
                 q_ref, k_ref, v_ref, xs_ref, bc_ref, z_ref, g_ref, hext_ref, pext_ref):
    i = pl.program_id(1)
    nb = pl.num_programs(1)
    tm = xc_ref.shape[1]
    d = xc_ref.shape[2]
    halo = BF16_ROWS
    gain = gain_ref[...]
    shift = mod_ref[0, :, 0:d]
    scale = mod_ref[0, :, d:2 * d]

    hc = _norm_mod(xc_ref[0], gain, shift, scale)
    hp = _norm_mod(xp_ref[0], gain, shift, scale)
    hn = _norm_mod(xn_ref[0], gain, shift, scale)
    hext_ref[0:halo, :] = jnp.where(i > 0, hp, 0.0).astype(BF16)
    hext_ref[halo:halo + tm, :] = hc.astype(BF16)
    hext_ref[halo + tm:, :] = jnp.where(i < nb - 1, hn, 0.0).astype(BF16)

    gw = pext_ref.shape[1]
    pad = CONV_K // 2
    plan = ((q_ref, 0, DN_QK, DN_DK ** -0.5), (k_ref, DN_QK, DN_QK, 1.0), (v_ref, 2 * DN_QK, DN_VW, None),
            (xs_ref, 2 * DN_QK + DN_VW, SSD_INNER, None), (bc_ref, 2 * DN_QK + DN_VW + SSD_INNER, 2 * SSD_BC, None))
    for out_ref, base, width, l2_scale in plan:
        for off in range(0, width, gw):
            c0 = base + off
            pext_ref[...] = _dot(hext_ref[...], w_ref[:, c0:c0 + gw])
            acc = cb_ref[:, c0:c0 + gw] + cw_ref[0:1, c0:c0 + gw] * pext_ref[halo - pad:halo - pad + tm, :]
            for t in range(1, CONV_K):
                acc = acc + cw_ref[t:t + 1, c0:c0 + gw] * pext_ref[halo - pad + t:halo - pad + t + tm, :]
            u = _silu(acc)
            if l2_scale is None:
                out_ref[0, :, off:off + gw] = u
            else:
                for s in range(0, gw, LANE):
                    uh = u[:, s:s + LANE]
                    inv = lax.rsqrt(jnp.sum(uh * uh, axis=-1, keepdims=True) + EPS)
                    out_ref[0, :, off + s:off + s + LANE] = uh * (inv * l2_scale)

    hcb = hext_ref[halo:halo + tm, :]
    for off in range(0, EV_GATE, gw):
        z_ref[0, :, off:off + gw] = _dot(hcb, w_ref[:, EV_CONV + off:EV_CONV + off + gw])

    p = _dot(hc, wsm_ref[...]) + gb_ref[...]
    lane = lax.broadcasted_iota(jnp.int32, p.shape, 1)
    g_ref[0] = jnp.where(lane < 2 * DN_HEADS, _sigmoid(p), _softplus(p)) * gs_ref[...]


def _even_in(x, modv, gain, w_main, w_small, conv_w, conv_b, gate_bias, gate_scale):
    b, l, d = x.shape
    tm = min(ROW_TILE, l)
    nb = l // tm
    hb = tm // BF16_ROWS
    gw = 512
    row = lambda c: pl.BlockSpec((1, tm, c), lambda bi, i: (bi, i, 0))
    outs = (DN_QK, DN_QK, DN_VW, SSD_INNER, 2 * SSD_BC, EV_GATE, LANE)
    return pl.pallas_call(
        _evin_kernel,
        grid=(b, nb),
        in_specs=[
            pl.BlockSpec((1, BF16_ROWS, d), lambda bi, i: (bi, jnp.maximum(i * hb - 1, 0), 0)),
            row(d),
            pl.BlockSpec((1, BF16_ROWS, d), lambda bi, i: (bi, jnp.minimum((i + 1) * hb, l // BF16_ROWS - 1), 0)),
            pl.BlockSpec((1, 1, modv.shape[2]), lambda bi, i: (bi, 0, 0)),
            _resident(gain.shape), _resident(w_main.shape), _resident(w_small.shape), _resident(conv_w.shape),
            _resident(conv_b.shape), _resident(gate_bias.shape), _resident(gate_scale.shape),
        ],
        out_specs=[row(c) for c in outs],
        out_shape=[jax.ShapeDtypeStruct((b, l, c), F32) for c in outs],
        scratch_shapes=[pltpu.VMEM((tm + 2 * BF16_ROWS, d), BF16), pltpu.VMEM((tm + 2 * BF16_ROWS, gw), F32)],
        compiler_params=_params("parallel", "parallel"),
        name="even_in",
    )(x, x, x, modv, gain, w_main, w_small, conv_w, conv_b, gate_bias, gate_scale)


def _tri_masks(c):
    r = lax.broadcasted_iota(jnp.int32, (c, c), 0)
    s = lax.broadcasted_iota(jnp.int32, (c, c), 1)
    return r, s


TRI_BASE = 16


def _block_masks(r, s):
    same = lambda n: (r // n) == (s // n)
    masks = [same(TRI_BASE).astype(F32)]
    n = TRI_BASE
    while n < CHUNK:
        n *= 2
        masks.append(jnp.where(same(n), 1.0, 0.0) - jnp.where(same(n // 2), 1.0, 0.0))
    return masks


def _unit_tri_inverse(a, eye, blocks):
    p = -(a * blocks[0])
    x = eye + p
    for _ in range(int(math.log2(TRI_BASE)) - 1):
        p = _dot(p, p)
        x = x + _dot(x, p)
    for m in blocks[1:]:
        x = x - _dot(x, _dot(a * m, x))
    return x


def _dn_chunk(q, k, v, rows, state, incl, strict, inclf, eye, blocks, first_row):
    c = q.shape[0]
    cum_row = _dot_nt(rows, inclf)
    cols = _dot_nt(jnp.concatenate([inclf, eye], axis=0), rows)
    gc = cols[0:c, 1:2]
    beta = cols[c:2 * c, 0:1]
    total = gc[first_row:first_row + 1, :]
    decay = jnp.where(incl, jnp.exp(gc - cum_row[1:2, :]), 0.0)
    kb = k * beta
    a = jnp.where(strict, _dot_nt(kb, k) * decay, 0.0)
    t = _unit_tri_inverse(a, eye, blocks)
    egc = jnp.exp(gc)
    uw = _dot(t, jnp.concatenate([v * beta, kb * egc], axis=1))
    u = uw[:, 0:DN_DV]
    w = uw[:, DN_DV:]
    ws_qs = _dot(jnp.concatenate([w, q * egc], axis=0), state)
    v_new = u - ws_qs[0:c]
    attn = _dot_nt(q, k) * decay
    o = ws_qs[c:2 * c] + _dot(attn, v_new)
    state = state * jnp.exp(total) + _dot_tn(k * jnp.exp(total - gc), v_new)
    return o, state


def _dn_kernel(qf_ref, kf_ref, vf_ref, gf_ref, qb_ref, kb_ref, vb_ref, gb_ref, s0_ref,
               of_ref, ob_ref, sfin_ref, s_ref):
    i = pl.program_id(2)
    nck = gf_ref.shape[2]
    c = CHUNK

    @pl.when(i == 0)
    def _():
        s_ref[...] = s0_ref[0, 0]

    r, s = _tri_masks(c)
    eye = (r == s).astype(F32)
    lower, upper = r >= s, r <= s
    lowerf, upperf = lower.astype(F32), upper.astype(F32)
    blocks = _block_masks(r, s)

    def body(ci, carry):
        cf = ci
        cb = nck - 1 - ci
        rf = pl.ds(pl.multiple_of(cf * c, c), c)
        rb = pl.ds(pl.multiple_of(cb * c, c), c)
        o, st = _dn_chunk(qf_ref[0, rf, :], kf_ref[0, rf, :], vf_ref[0, rf, :], gf_ref[0, 0, cf][0:8],
                          s_ref[0], lower, r > s, lowerf, eye, blocks, c - 1)
        of_ref[0, rf, :] = o
        s_ref[0] = st
        o, st = _dn_chunk(qb_ref[0, rb, :], kb_ref[0, rb, :], vb_ref[0, rb, :], gb_ref[0, 0, cb][8:16],
                          s_ref[1], upper, r < s, upperf, eye, blocks, 0)
        ob_ref[0, rb, :] = o
        s_ref[1] = st
        return carry

    lax.fori_loop(0, nck, body, 0)

    @pl.when(i == pl.num_programs(2) - 1)
    def _():
        sfin_ref[0, 0] = s_ref[...]


def _dn_scan(q, k, v, grows, state0):
    b, l, _ = q.shape
    tl = min(SCAN_TILE, l)
    nb = l // tl
    nck = tl // CHUNK
    fwd = pl.BlockSpec((1, tl, LANE), lambda bi, h, i: (bi, i, h))
    bwd = pl.BlockSpec((1, tl, LANE), lambda bi, h, i: (bi, nb - 1 - i, h))
    gfwd = pl.BlockSpec((1, 1, nck, 16, CHUNK), lambda bi, h, i: (bi, h, i, 0, 0))
    gbwd = pl.BlockSpec((1, 1, nck, 16, CHUNK), lambda bi, h, i: (bi, h, nb - 1 - i, 0, 0))
    st = pl.BlockSpec((1, 1, 2, DN_DK, DN_DV), lambda bi, h, i: (bi, h, 0, 0, 0))
    return pl.pallas_call(
        _dn_kernel,
        grid=(b, DN_HEADS, nb),
        in_specs=[fwd, fwd, fwd, gfwd, bwd, bwd, bwd, gbwd, st],
        out_specs=[fwd, bwd, st],
        out_shape=[jax.ShapeDtypeStruct(q.shape, F32), jax.ShapeDtypeStruct(q.shape, F32),
                   jax.ShapeDtypeStruct(state0.shape, F32)],
        scratch_shapes=[pltpu.VMEM((2, DN_DK, DN_DV), F32)],
        compiler_params=_params("parallel", "parallel", "arbitrary"),
        name="deltanet_scan",
    )(q, k, v, grows, q, k, v, grows, state0)


def _ssd_chunk(x, bm, cm, rows, state, expand, incl, inclf, eye, first_row):
    c = x.shape[0]
    nh = SSD_GRP
    p = SSD_HEAD_DIM
    cum_row = _dot_nt(rows[nh:2 * nh], inclf)
    cols = _dot_nt(jnp.concatenate([inclf, eye], axis=0), rows)
    cum = cols[0:c, nh:2 * nh]
    dt = cols[c:2 * c, 0:nh]
    total = cum[first_row:first_row + 1, :]
    xdt = x * _dot(dt, expand)
    cb = _dot_nt(cm, bm)
    from_state = _dot(cm, state) * _dot(jnp.exp(cum), expand)
    parts = []
    for h in range(nh):
        seg = jnp.where(incl, jnp.exp(cum[:, h:h + 1] - cum_row[h:h + 1, :]), 0.0)
        parts.append(_dot(cb * seg, xdt[:, h * p:(h + 1) * p]))
    y = jnp.concatenate(parts, axis=1) + from_state
    carry_in = xdt * _dot(jnp.exp(total - cum), expand)
    state = state * _dot(jnp.exp(total), expand) + _dot_tn(bm, carry_in)
    return y, state


def _ssd_kernel(xf_ref, bf_ref, cf_ref, gf_ref, xb_ref, bb_ref, cb_ref, gb_ref, s0_ref, e_ref,
                yf_ref, yb_ref, sfin_ref, s_ref):
    i = pl.program_id(2)
    nck = gf_ref.shape[2]
    c = CHUNK
    nh = SSD_GRP

    @pl.when(i == 0)
    def _():
        s_ref[...] = s0_ref[0, 0]

    r, s = _tri_masks(c)
    eye = (r == s).astype(F32)
    lower, upper = r >= s, r <= s
    lowerf, upperf = lower.astype(F32), upper.astype(F32)
    expand = e_ref[...]

    def body(ci, carry):
        cf = ci
        cb = nck - 1 - ci
        rf = pl.ds(pl.multiple_of(cf * c, c), c)
        rb = pl.ds(pl.multiple_of(cb * c, c), c)
        y, st = _ssd_chunk(xf_ref[0, rf, :], bf_ref[0, rf, :], cf_ref[0, rf, :], gf_ref[0, 0, cf][0:2 * nh],
                           s_ref[0], expand, lower, lowerf, eye, c - 1)
        yf_ref[0, rf, :] = y
        s_ref[0] = st
        y, st = _ssd_chunk(xb_ref[0, rb, :], bb_ref[0, rb, :], cb_ref[0, rb, :], gb_ref[0, 0, cb][2 * nh:4 * nh],
                           s_ref[1], expand, upper, upperf, eye, 0)
        yb_ref[0, rb, :] = y
        s_ref[1] = st
        return carry

    lax.fori_loop(0, nck, body, 0)

    @pl.when(i == pl.num_programs(2) - 1)
    def _():
        sfin_ref[0, 0] = s_ref[...]


def _ssd_scan(xs, bc, grows, state0, expand):
    b, l, _ = xs.shape
    tl = min(SCAN_TILE, l)
    nb = l // tl
    nck = tl // CHUNK
    gwid = SSD_GRP * SSD_HEAD_DIM
    ng = SSD_GROUPS
    xf = pl.BlockSpec((1, tl, gwid), lambda bi, g, i: (bi, i, g))
    xb = pl.BlockSpec((1, tl, gwid), lambda bi, g, i: (bi, nb - 1 - i, g))
    bf = pl.BlockSpec((1, tl, SSD_STATE), lambda bi, g, i: (bi, i, g))
    cf = pl.BlockSpec((1, tl, SSD_STATE), lambda bi, g, i: (bi, i, ng + g))
    bb = pl.BlockSpec((1, tl, SSD_STATE), lambda bi, g, i: (bi, nb - 1 - i, g))
    cb = pl.BlockSpec((1, tl, SSD_STATE), lambda bi, g, i: (bi, nb - 1 - i, ng + g))
    gf = pl.BlockSpec((1, 1, nck, 4 * SSD_GRP, CHUNK), lambda bi, g, i: (bi, g, i, 0, 0))
    gb = pl.BlockSpec((1, 1, nck, 4 * SSD_GRP, CHUNK), lambda bi, g, i: (bi, g, nb - 1 - i, 0, 0))
    st = pl.BlockSpec((1, 1, 2, SSD_STATE, gwid), lambda bi, g, i: (bi, g, 0, 0, 0))
    return pl.pallas_call(
        _ssd_kernel,
        grid=(b, ng, nb),
        in_specs=[xf, bf, cf, gf, xb, bb, cb, gb, st, _resident(expand.shape)],
        out_specs=[xf, xb, st],
        out_shape=[jax.ShapeDtypeStruct(xs.shape, F32), jax.ShapeDtypeStruct(xs.shape, F32),
                   jax.ShapeDtypeStruct(state0.shape, F32)],
        scratch_shapes=[pltpu.VMEM((2, SSD_STATE, gwid), F32)],
        compiler_params=_params("parallel", "parallel", "arbitrary"),
        name="ssd_scan",
    )(xs, bc, bc, grows, xs, bc, bc, grows, state0, expand)


def _evmerge_kernel(of_ref, ob_ref, yf_ref, yb_ref, xs_ref, z_ref, dng_ref, sd_ref, sg_ref, a_ref, b_ref):
    for s in range(0, DN_VW, LANE):
        o = of_ref[0, :, s:s + LANE] + ob_ref[0, :, s:s + LANE]
        o = o * lax.rsqrt(jnp.mean(o * o, axis=-1, keepdims=True) + EPS) * dng_ref[...]
        a_ref[0, :, s:s + LANE] = (o * _silu(z_ref[0, :, s:s + LANE])).astype(BF16)
    gwid = SSD_INNER // SSD_GROUPS
    for s in range(0, SSD_INNER, gwid):
        y = yf_ref[0, :, s:s + gwid] + yb_ref[0, :, s:s + gwid] + sd_ref[:, s:s + gwid] * xs_ref[0, :, s:s + gwid]
        y = y * _silu(z_ref[0, :, DN_VW + s:DN_VW + s + gwid])
        y = y * lax.rsqrt(jnp.mean(y * y, axis=-1, keepdims=True) + EPS) * sg_ref[:, s:s + gwid]
        b_ref[0, :, s:s + gwid] = y.astype(BF16)


def _even_merge(o_f, o_b, y_f, y_b, xs, z, dn_gain, ssd_d, ssd_gain):
    b, l, _ = o_f.shape
    tm = min(ROW_TILE, l)
    row = lambda c: pl.BlockSpec((1, tm, c), lambda bi, i: (bi, i, 0))
    return pl.pallas_call(
        _evmerge_kernel,
        grid=(b, l // tm),
        in_specs=[row(DN_VW), row(DN_VW), row(SSD_INNER), row(SSD_INNER), row(SSD_INNER), row(EV_GATE),
                  _resident(dn_gain.shape), _resident(ssd_d.shape), _resident(ssd_gain.shape)],
        out_specs=[row(DN_VW), row(SSD_INNER)],
        out_shape=[jax.ShapeDtypeStruct((b, l, DN_VW), BF16), jax.ShapeDtypeStruct((b, l, SSD_INNER), BF16)],
        compiler_params=_params("parallel", "parallel"),
        name="even_merge",
    )(o_f, o_b, y_f, y_b, xs, z, dn_gain, ssd_d, ssd_gain)


def _mixffn_kernel(x_ref, a_ref, b_ref, mod_ref, gain_ref, wo_ref, wi_ref, wf_ref, o_ref, act_ref):
    d = x_ref.shape[2]
    na = a_ref.shape[2]
    hid = wf_ref.shape[0]
    g1 = mod_ref[0, :, 2 * d:3 * d]
    shift = mod_ref[0, :, 3 * d:4 * d]
    scale = mod_ref[0, :, 4 * d:5 * d]
    g2 = mod_ref[0, :, 5 * d:6 * d]
    mixed = _dot(a_ref[0], wo_ref[0:na, :]) + _dot(b_ref[0], wo_ref[na:, :])
    x1 = x_ref[0] + g1 * mixed
    h = _norm_mod(x1, gain_ref[...], shift, scale).astype(BF16)
    half = hid // 2
    for s in range(0, hid, half):
        gate = _dot(h, wi_ref[:, s:s + half])
        up = _dot(h, wi_ref[:, hid + s:hid + s + half])
        act_ref[:, s:s + half] = (_silu(gate) * up).astype(BF16)
    o_ref[0] = x1 + g2 * _dot(act_ref[...], wf_ref[...])


def _mix_ffn(x, a, bmix, modv, gain, w_out, w_in, w_ffn_out):
    b, l, d = x.shape
    tm = min(ROW_TILE, l)
    hid = w_ffn_out.shape[0]
    row = lambda c: pl.BlockSpec((1, tm, c), lambda bi, i: (bi, i, 0))
    return pl.pallas_call(
        _mixffn_kernel,
        grid=(b, l // tm),
        in_specs=[row(d), row(a.shape[2]), row(bmix.shape[2]),
                  pl.BlockSpec((1, 1, modv.shape[2]), lambda bi, i: (bi, 0, 0)),
                  _resident(gain.shape), _resident(w_out.shape), _resident(w_in.shape), _resident(w_ffn_out.shape)],
        out_specs=row(d),
        out_shape=jax.ShapeDtypeStruct(x.shape, F32),
        scratch_shapes=[pltpu.VMEM((tm, hid), BF16)],
        compiler_params=_params("parallel", "parallel"),
        name="mix_ffn",
    )(x, a, bmix, modv, gain, w_out, w_in, w_ffn_out)


def _rope(x, cos, sin):
    lane = lax.broadcasted_iota(jnp.int32, x.shape, 1)
    partner = jnp.where(lane % 32 < 16, pltpu.roll(x, LANE - 16, 1), pltpu.roll(x, 16, 1))
    return x * cos + partner * sin


def _odin_kernel(x_ref, mod_ref, gain_ref, w_ref, qan_ref, wuq_ref, kvan_ref, wukv_ref, dqn_ref, dkn_ref,
                 mqn_ref, mkn_ref, cos_ref, sin_ref, *out_refs, with_q, with_rope):
    if with_q:
        daq_ref, dak_ref, dav_ref, mq_ref, mk_ref, mv_ref = out_refs
    else:
        dak_ref, dav_ref, mk_ref, mv_ref = out_refs
    d = x_ref.shape[2]
    h = _norm_mod(x_ref[0], gain_ref[...], mod_ref[0, :, 0:d], mod_ref[0, :, d:2 * d]).astype(BF16)
    if with_rope:
        cos = cos_ref[...]
        sin = sin_ref[...]
    lane = lax.broadcasted_iota(jnp.int32, (h.shape[0], LANE), 1)
    low = lane < DA_DIM

    def pair_norm(u, g):
        sq = u * u
        lo = jnp.sum(jnp.where(low, sq, 0.0), axis=-1, keepdims=True)
        hi = jnp.sum(sq, axis=-1, keepdims=True) - lo
        ms = jnp.where(low, lo, hi) * (1.0 / DA_DIM)
        return u * lax.rsqrt(ms + EPS) * g

    def da_part(out_ref, base, g, out_scale):
        p = _dot(h, w_ref[:, base:base + DA_QK])
        for s in range(0, DA_QK, LANE):
            u = pair_norm(p[:, s:s + LANE], g)
            if with_rope:
                u = _rope(u, cos, sin)
            out_ref[0, :, s:s + LANE] = (u * out_scale).astype(BF16)

    if with_q:
        da_part(daq_ref, 0, dqn_ref[...], DA_DIM ** -0.5)
    da_part(dak_ref, DA_QK, dkn_ref[...], 1.0)
    dav_ref[0] = _dot(h, w_ref[:, 2 * DA_QK:2 * DA_QK + DA_VW]).astype(BF16)

    def low_norm(u, g):
        ms = jnp.sum(u * u, axis=-1, keepdims=True) * (1.0 / MLA_ROPE)
        return u * lax.rsqrt(ms + EPS) * g

    def full_norm(u, g):
        return u * lax.rsqrt(jnp.mean(u * u, axis=-1, keepdims=True) + EPS) * g

    c0 = 2 * DA_QK + DA_VW
    if with_q:
        cq = full_norm(_dot(h, w_ref[:, c0:c0 + MLA_Q_RANK]), qan_ref[...]).astype(BF16)
        qf = _dot(cq, wuq_ref[...])
        scale = (MLA_NOPE + MLA_ROPE) ** -0.5
        for hd in range(MLA_HEADS):
            s = hd * MLA_QK_PAD
            mq_ref[0, :, s:s + LANE] = (full_norm(qf[:, s:s + LANE], mqn_ref[:, 0:LANE]) * scale).astype(BF16)
            u = low_norm(qf[:, s + LANE:s + 2 * LANE], mqn_ref[:, LANE:2 * LANE])
            if with_rope:
                u = _rope(u, cos, sin)
            mq_ref[0, :, s + LANE:s + 2 * LANE] = (u * scale).astype(BF16)
    c0 += MLA_Q_RANK
    ckv = full_norm(_dot(h, w_ref[:, c0:c0 + MLA_KV_RANK]), kvan_ref[...]).astype(BF16)
    kvu = _dot(ckv, wukv_ref[...])
    c0 += MLA_KV_RANK
    kr = low_norm(_dot(h, w_ref[:, c0:c0 + LANE]), mkn_ref[:, LANE:2 * LANE])
    if with_rope:
        kr = _rope(kr, cos, sin)
    kr = kr.astype(BF16)
    for hd in range(MLA_HEADS):
        s = hd * MLA_QK_PAD
        mk_ref[0, :, s:s + LANE] = full_norm(kvu[:, hd * LANE:(hd + 1) * LANE], mkn_ref[:, 0:LANE]).astype(BF16)
        mk_ref[0, :, s + LANE:s + 2 * LANE] = kr
    mv_ref[0] = kvu[:, MLA_HEADS * MLA_NOPE:].astype(BF16)


def _odd_in(x, modv, gain, w_in, q_a_gain, w_uq, kv_a_gain, w_ukv, dq_gain, dk_gain, mq_gain, mk_gain,
            cos_t, sin_t, with_q, with_rope):
    b, l, d = x.shape
    tm = min(ROW_TILE, l)
    row = lambda c: pl.BlockSpec((1, tm, c), lambda bi, i: (bi, i, 0))
    widths = (DA_QK, DA_VW, MLA_HEADS * MLA_QK_PAD, MLA_HEADS * MLA_V)
    if with_q:
        widths = (DA_QK,) + widths[:2] + (MLA_HEADS * MLA_QK_PAD,) + widths[2:]
    table = pl.BlockSpec((tm, LANE), lambda bi, i: (i, 0))
    res = [gain, w_in, q_a_gain, w_uq, kv_a_gain, w_ukv, dq_gain, dk_gain, mq_gain, mk_gain]
    return pl.pallas_call(
        functools.partial(_odin_kernel, with_q=with_q, with_rope=with_rope),
        grid=(b, l // tm),
        in_specs=[row(d), pl.BlockSpec((1, 1, modv.shape[2]), lambda bi, i: (bi, 0, 0))]
        + [_resident(a.shape) for a in res] + [table, table],
        out_specs=[row(c) for c in widths],
        out_shape=[jax.ShapeDtypeStruct((b, l, c), BF16) for c in widths],
        compiler_params=_params("parallel", "parallel"),
        name="odd_in_q" if with_q else "odd_in_kv",
    )(x, modv, *res, cos_t, sin_t)


def _online_softmax_step(s, m, l, acc, v):
    m_new = jnp.maximum(m, jnp.max(s, axis=-1, keepdims=True))
    alpha = jnp.exp(m - m_new)
    p = jnp.exp(s - m_new)
    l = alpha * l + jnp.sum(p, axis=-1, keepdims=True)
    acc = alpha * acc + _dot(p.astype(BF16), v)
    return m_new, l, acc


def _da_kernel(q_ref, k_ref, v_ref, lam_ref, gain_ref, o_ref, *, tk, out_scale):
    tq = q_ref.shape[1]
    nk = k_ref.shape[1] // tk
    q = q_ref[0]
    lane = lax.broadcasted_iota(jnp.int32, q.shape, 1)
    zero = jnp.zeros_like(q)
    q1 = jnp.where(lane < DA_DIM, q, zero)
    q2 = jnp.where(lane < DA_DIM, zero, q)

    def body(j, carry):
        m1, l1, a1, m2, l2, a2 = carry
        rows = pl.ds(pl.multiple_of(j * tk, tk), tk)
        k = k_ref[0, rows, :]
        v = v_ref[0, rows, :]
        m1, l1, a1 = _online_softmax_step(_dot_nt(q1, k), m1, l1, a1, v)
        m2, l2, a2 = _online_softmax_step(_dot_nt(q2, k), m2, l2, a2, v)
        return m1, l1, a1, m2, l2, a2

    neg = jnp.full((tq, 1), -jnp.inf, F32)
    zl = jnp.zeros((tq, 1), F32)
    za = jnp.zeros((tq, LANE), F32)
    m1, l1, a1, m2, l2, a2 = lax.fori_loop(0, nk, body, (neg, zl, za, neg, zl, za))
    lp = lam_ref[...]
    lam = (jnp.exp(jnp.sum(lp[0:1] * lp[1:2], axis=-1, keepdims=True))
           - jnp.exp(jnp.sum(lp[2:3] * lp[3:4], axis=-1, keepdims=True)) + (1.0 - out_scale))
    o = a1 / l1 - lam * (a2 / l2)
    o = o * lax.rsqrt(jnp.mean(o * o, axis=-1, keepdims=True) + EPS) * gain_ref[...]
    o_ref[0] = (o * out_scale).astype(BF16)


def _diff_attention(q, k, v, lam_p, gain, lambda_init):
    b, lq, _ = q.shape
    lk = k.shape[1]
    tq = min(ATT_TQ, lq)
    tk = ATT_TK if lk % ATT_TK == 0 else 256
    qs = pl.BlockSpec((1, tq, LANE), lambda bi, h, i: (bi, i, h))
    ks = pl.BlockSpec((1, lk, LANE), lambda bi, h, i: (bi, 0, h))
    return pl.pallas_call(
        functools.partial(_da_kernel, tk=tk, out_scale=1.0 - lambda_init),
        grid=(b, DA_HEADS, lq // tq),
        in_specs=[qs, ks, ks, _resident(lam_p.shape), _resident(gain.shape)],
        out_specs=qs,
        out_shape=jax.ShapeDtypeStruct(q.shape, BF16),
        compiler_params=_params("parallel", "parallel", "parallel"),
        name="diff_attention",
    )(q, k, v, lam_p, gain)


def _mla_kernel(q_ref, k_ref, v_ref, o_ref, *, tk):
    tq = q_ref.shape[1]
    nk = k_ref.shape[1] // tk
    q = q_ref[0]

    def body(j, carry):
        rows = pl.ds(pl.multiple_of(j * tk, tk), tk)
        return _online_softmax_step(_dot_nt(q, k_ref[0, rows, :]), *carry, v_ref[0, rows, :])

    m, l, acc = lax.fori_loop(0, nk, body, (jnp.full((tq, 1), -jnp.inf, F32), jnp.zeros((tq, 1), F32),
                                            jnp.zeros((tq, MLA_V), F32)))
    o_ref[0] = (acc / l).astype(BF16)


def _mla_attention(q, k, v):
    b, lq, _ = q.shape
    lk = k.shape[1]
    tq = min(ATT_TQ, lq)
    tk = ATT_TK if lk % ATT_TK == 0 else 256
    qs = pl.BlockSpec((1, tq, MLA_QK_PAD), lambda bi, h, i: (bi, i, h))
    ks = pl.BlockSpec((1, lk, MLA_QK_PAD), lambda bi, h, i: (bi, 0, h))
    vs = pl.BlockSpec((1, lk, MLA_V), lambda bi, h, i: (bi, 0, h))
    return pl.pallas_call(
        functools.partial(_mla_kernel, tk=tk),
        grid=(b, MLA_HEADS, lq // tq),
        in_specs=[qs, ks, vs],
        out_specs=pl.BlockSpec((1, tq, MLA_V), lambda bi, h, i: (bi, i, h)),
        out_shape=jax.ShapeDtypeStruct((b, lq, MLA_HEADS * MLA_V), BF16),
        compiler_params=_params("parallel", "parallel", "parallel"),
        name="mla_attention",
    )(q, k, v)


def _scan_rows(gates_t, index, chunks):
    b, _, l = gates_t.shape
    idx = jnp.asarray(index, jnp.int32)
    rows = gates_t[:, idx, :]
    rows = rows.reshape(b, idx.shape[0], idx.shape[1], chunks, l // chunks)
    return jnp.swapaxes(rows, 2, 3)


def _even_layer(x, ctx, mod_x, mod_c, p):
    b = x.shape[0]
    nh, ns = DN_HEADS, SSD_HEADS
    pad_col = LANE - 1
    dn_index = [[h, 2 * nh + h] + [pad_col] * 6 + [nh + h, 3 * nh + h] + [pad_col] * 6 for h in range(nh)]
    dt0, a0 = 4 * nh, 4 * nh + 2 * ns
    ssd_index = [[dt0 + g * SSD_GRP + r for r in range(SSD_GRP)] + [a0 + g * SSD_GRP + r for r in range(SSD_GRP)]
                 + [dt0 + ns + g * SSD_GRP + r for r in range(SSD_GRP)] + [a0 + ns + g * SSD_GRP + r for r in range(SSD_GRP)]
                 for g in range(SSD_GROUPS)]

    def prepare(t, modv):
        q, k, v, xs, bc, z, gates = _even_in(t, modv, p["gain_mix"], p["w_main"], p["w_small"], p["conv_w"], p["conv_b"],
                                             p["gate_bias"], p["gate_scale"])
        gt = jnp.swapaxes(gates, 1, 2)
        nchunk = t.shape[1] // CHUNK
        return dict(q=q, k=k, v=v, xs=xs, bc=bc, z=z, dn_rows=_scan_rows(gt, dn_index, nchunk),
                    ssd_rows=_scan_rows(gt, ssd_index, nchunk))

    pc = prepare(ctx, mod_c)
    pL = prepare(x, mod_x)
    dn0 = jnp.zeros((b, nh, 2, DN_DK, DN_DV), F32)
    ssd0 = jnp.zeros((b, SSD_GROUPS, 2, SSD_STATE, SSD_GRP * SSD_HEAD_DIM), F32)
    ocf, ocb, dn1 = _dn_scan(pc["q"], pc["k"], pc["v"], pc["dn_rows"], dn0)
    olf, olb, _ = _dn_scan(pL["q"], pL["k"], pL["v"], pL["dn_rows"], dn1)
    ycf, ycb, ssd1 = _ssd_scan(pc["xs"], pc["bc"], pc["ssd_rows"], ssd0, p["expand"])
    ylf, ylb, _ = _ssd_scan(pL["xs"], pL["bc"], pL["ssd_rows"], ssd1, p["expand"])

    def finish(t, modv, pp, of, ob, yf, yb):
        a, bm = _even_merge(of, ob, yf, yb, pp["xs"], pp["z"], p["dn_gain"], p["ssd_d"], p["ssd_gain"])
        return _mix_ffn(t, a, bm, modv, p["gain_ffn"], p["w_out"], p["ffn_w_in"], p["ffn_w_out"])

    return finish(x, mod_x, pL, olf, olb, ylf, ylb), finish(ctx, mod_c, pc, ocf, ocb, ycf, ycb)


def _rope_tables(n_tokens):
    lane = jnp.arange(LANE)
    quarter = (lane % 64) // 16
    n_freq = DA_DIM // 4
    inv_freq = ROPE_THETA ** (-(lane % 16).astype(F32) / n_freq)
    tok = jnp.arange(n_tokens)
    pos = jnp.where(quarter[None, :] < 2, (tok // GRID_W)[:, None], (tok % GRID_W)[:, None]).astype(F32)
    ang = pos * inv_freq[None, :]
    sign = jnp.where(quarter % 2 == 0, -1.0, 1.0).astype(F32)
    return jnp.cos(ang), jnp.sin(ang) * sign[None, :]


def _odd_layer(x, ctx, mod_x, mod_c, p, lambda_init):
    cos_t, sin_t = _rope_tables(x.shape[1])
    args = (p["gain_mix"], p["w_in"], p["q_a_gain"], p["w_uq"], p["kv_a_gain"], p["w_ukv"], p["dq_gain"], p["dk_gain"],
            p["mq_gain"], p["mk_gain"])
    daq, dak, dav, mq, mk, mv = _odd_in(x, mod_x, *args, cos_t, sin_t, True, True)
    cak, cav, cmk, cmv = _odd_in(ctx, mod_c, *args, cos_t[:ctx.shape[1]], sin_t[:ctx.shape[1]], False, False)
    cat = lambda a, c: jnp.concatenate([a, c], axis=1)
    da = _diff_attention(daq, cat(dak, cak), cat(dav, cav), p["da_lambda"], p["sub_gain"], lambda_init)
    ml = _mla_attention(mq, cat(mk, cmk), cat(mv, cmv))
    return _mix_ffn(x, da, ml, mod_x, p["gain_ffn"], p["w_out"], p["ffn_w_in"], p["ffn_w_out"])


def _even_params(i, j, norm_mix, norm_ffn, ffn_w_in, ffn_w_out, ev_w_in, ev_conv_w, ev_conv_b, dn_a_log, dn_dt_bias,
                 dn_norm, ssd_a_log, ssd_dt_bias, ssd_d, ssd_norm, ev_w_out):
    d = norm_mix.shape[1]
    w = ev_w_in[j]
    small = w[:, EV_CONV + EV_GATE:]
    nh, ns = DN_HEADS, SSD_HEADS
    w_small = jnp.concatenate([small, small[:, 4 * nh:], jnp.zeros((d, LANE - 4 * nh - 4 * ns), F32)], axis=1)
    zeros = lambda n: jnp.zeros((n,), F32)
    gate_bias = jnp.concatenate([zeros(2 * nh), dn_dt_bias[j].reshape(-1), ssd_dt_bias[j].reshape(-1),
                                 ssd_dt_bias[j].reshape(-1), zeros(LANE - 4 * nh - 4 * ns)])
    gate_scale = jnp.concatenate([jnp.ones((2 * nh,), F32), -jnp.exp(dn_a_log[j].reshape(-1)), jnp.ones((2 * ns,), F32),
                                  -jnp.exp(ssd_a_log[j].reshape(-1)), zeros(LANE - 4 * nh - 4 * ns)])
    expand = jnp.repeat(jnp.eye(SSD_GRP, dtype=F32), SSD_HEAD_DIM, axis=1)
    return dict(
        gain_mix=norm_mix[i].reshape(1, d), gain_ffn=norm_ffn[i].reshape(1, d),
        w_main=w[:, :EV_CONV + EV_GATE].astype(BF16), w_small=w_small,
        conv_w=ev_conv_w[j], conv_b=ev_conv_b[j].reshape(1, -1),
        gate_bias=gate_bias.reshape(1, LANE), gate_scale=gate_scale.reshape(1, LANE), expand=expand,
        dn_gain=dn_norm[j].reshape(1, DN_DV), ssd_d=jnp.repeat(ssd_d[j], SSD_HEAD_DIM).reshape(1, SSD_INNER),
        ssd_gain=ssd_norm[j].reshape(1, SSD_INNER), w_out=ev_w_out[j].astype(BF16),
        ffn_w_in=ffn_w_in[i].astype(BF16), ffn_w_out=ffn_w_out[i].astype(BF16))


def _odd_params(i, j, norm_mix, norm_ffn, ffn_w_in, ffn_w_out, od_w_in, da_q_norm, da_k_norm, da_lambda, da_sub_norm,
                mla_q_a_norm, mla_w_uq, mla_kv_a_norm, mla_w_ukv, mla_q_norm, mla_k_norm, od_w_out):
    d = norm_mix.shape[1]
    w = od_w_in[j]
    w_in = jnp.concatenate([w, jnp.zeros((d, LANE - MLA_ROPE), F32)], axis=1).astype(BF16)
    hq = mla_w_uq[j].reshape(MLA_Q_RANK, MLA_HEADS, MLA_NOPE + MLA_ROPE)
    hq = jnp.pad(hq, ((0, 0), (0, 0), (0, MLA_QK_PAD - MLA_NOPE - MLA_ROPE)))
    hkv = mla_w_ukv[j].reshape(MLA_KV_RANK, MLA_HEADS, MLA_NOPE + MLA_V)
    w_ukv = jnp.concatenate([hkv[:, :, :MLA_NOPE].reshape(MLA_KV_RANK, -1), hkv[:, :, MLA_NOPE:].reshape(MLA_KV_RANK, -1)], axis=1)
    pad_gain = lambda g: jnp.pad(g, (0, MLA_QK_PAD - MLA_NOPE - MLA_ROPE)).reshape(1, MLA_QK_PAD)
    return dict(
        gain_mix=norm_mix[i].reshape(1, d), gain_ffn=norm_ffn[i].reshape(1, d), w_in=w_in,
        q_a_gain=mla_q_a_norm[j].reshape(1, -1), w_uq=hq.reshape(MLA_Q_RANK, -1).astype(BF16),
        kv_a_gain=mla_kv_a_norm[j].reshape(1, -1), w_ukv=w_ukv.astype(BF16),
        dq_gain=jnp.tile(da_q_norm[j], 2).reshape(1, LANE), dk_gain=jnp.tile(da_k_norm[j], 2).reshape(1, LANE),
        mq_gain=pad_gain(mla_q_norm[j]), mk_gain=pad_gain(mla_k_norm[j]),
        da_lambda=jnp.pad(da_lambda[j], ((0, 4), (0, LANE - DA_DIM))), sub_gain=da_sub_norm[j].reshape(1, 2 * DA_DIM),
        w_out=od_w_out[j].astype(BF16), ffn_w_in=ffn_w_in[i].astype(BF16), ffn_w_out=ffn_w_out[i].astype(BF16))


def kernel(x, c, ctx, c_ctx, ada_w, ada_b, norm_mix, norm_ffn, ffn_w_in, ffn_w_out, ev_w_in, ev_conv_w, ev_conv_b, dn_a_log, dn_dt_bias, dn_norm, ssd_a_log, ssd_dt_bias, ssd_d, ssd_norm, ev_w_out, od_w_in, da_q_norm, da_k_norm, da_lambda, da_sub_norm, mla_q_a_norm, mla_w_uq, mla_kv_a_norm, mla_w_ukv, mla_q_norm, mla_k_norm, od_w_out):
    b, _, d = x.shape
    depth = ada_w.shape[0]
    assert b < 8
    cvec = jnp.concatenate([c, c_ctx[None, :], jnp.zeros((8 - b - 1, d), F32)], axis=0)
    mod = _modulation(cvec, ada_w, ada_b)
    for i in range(depth):
        last = i == depth - 1
        j = i // 2
        mod_x = mod[i, :b].reshape(b, 1, 6 * d)
        mod_c = jnp.broadcast_to(mod[i, b].reshape(1, 1, 6 * d), (b, 1, 6 * d))
        if i % 2 == 0:
            p = _even_params(i, j, norm_mix, norm_ffn, ffn_w_in, ffn_w_out, ev_w_in, ev_conv_w, ev_conv_b, dn_a_log,
                             dn_dt_bias, dn_norm, ssd_a_log, ssd_dt_bias, ssd_d, ssd_norm, ev_w_out)
            x, ctx_new = _even_layer(x, ctx, mod_x, mod_c, p)
        else:
            p = _odd_params(i, j, norm_mix, norm_ffn, ffn_w_in, ffn_w_out, od_w_in, da_q_norm, da_k_norm, da_lambda,
                            da_sub_norm, mla_q_a_norm, mla_w_uq, mla_kv_a_norm, mla_w_ukv, mla_q_norm, mla_k_norm, od_w_out)
            lambda_init = 0.8 - 0.6 * math.exp(-0.3 * i)
            if last:
                x = _odd_layer(x, ctx, mod_x, mod_c, p, lambda_init)
                ctx_new = ctx
            else:
                raise NotImplementedError("context update after an attention layer is not needed for depth 2")
        ctx = ctx_new
    return x
```

```python
import functools
import math

import jax
import jax.numpy as jnp
from jax import lax
from jax.experimental import pallas as pl
from jax.experimental.pallas import tpu as pltpu

F32 = jnp.float32
BF16 = jnp.bfloat16

GRID_W = 64
EPS = 1e-6
ROPE_THETA = 10000.0
CONV_K = 5
DN_HEADS = 8
DN_DK = 128
DN_DV = 128
SSD_HEADS = 16
SSD_HEAD_DIM = 64
SSD_GROUPS = 2
SSD_STATE = 128
SSD_GRP = SSD_HEADS // SSD_GROUPS
SSD_INNER = SSD_HEADS * SSD_HEAD_DIM
DA_HEADS = 8
DA_DIM = 64
MLA_HEADS = 8
MLA_Q_RANK = 512
MLA_KV_RANK = 256
MLA_NOPE = 128
MLA_ROPE = 64
MLA_V = 128
MLA_QK_PAD = 256

DN_QK = DN_HEADS * DN_DK
DN_VW = DN_HEADS * DN_DV
SSD_BC = SSD_GROUPS * SSD_STATE
EV_CONV = 2 * DN_QK + DN_VW + SSD_INNER + 2 * SSD_BC
EV_GATE = DN_VW + SSD_INNER
DA_QK = DA_HEADS * 2 * DA_DIM
DA_VW = DA_HEADS * 2 * DA_DIM

LANE = 128
BF16_ROWS = 16
CHUNK = 128
VMEM_LIMIT = 56 * 1024 * 1024

ROW_TILE = 512
SCAN_TILE = 512
DA_TQ = 512
MLA_TQ = 1024
ATT_TK = 256
ATT_STRIP = 256
V_ROWS = MLA_V + BF16_ROWS
LOG2E = 1.4426950408889634


def _dot(a, b):
    return jnp.dot(a, b, preferred_element_type=F32)


def _dot_nt(a, b):
    return lax.dot_general(a, b, (((1,), (1,)), ((), ())), preferred_element_type=F32)


def _dot_tn(a, b):
    return lax.dot_general(a, b, (((0,), (0,)), ((), ())), preferred_element_type=F32)


def _sigmoid(x):
    return 1.0 / (1.0 + jnp.exp(-x))


def _silu(x):
    return x * _sigmoid(x)


def _softplus(x):
    return jnp.maximum(x, 0.0) + jnp.log(1.0 + jnp.exp(-jnp.abs(x)))


def _params(*sem):
    return pltpu.CompilerParams(dimension_semantics=sem, vmem_limit_bytes=VMEM_LIMIT)


def _resident(shape):
    nd = len(shape)
    return pl.BlockSpec(shape, lambda *_: (0,) * nd, pipeline_mode=pl.Buffered(1))


def _norm_mod(x, gain, shift, scale):
    y = x * lax.rsqrt(jnp.mean(x * x, axis=-1, keepdims=True) + EPS)
    return (y * gain) * (1.0 + scale) + shift


def _mod_kernel(c_ref, w_ref, b_ref, o_ref):
    o_ref[0] = _dot(_silu(c_ref[...]), w_ref[0]) + b_ref[0]


def _modulation(cvec, ada_w, ada_b):
    depth, d, n = ada_w.shape
    tn = n // 4
    return pl.pallas_call(
        _mod_kernel,
        grid=(depth, n // tn),
        in_specs=[
            pl.BlockSpec((8, d), lambda l, j: (0, 0)),
            pl.BlockSpec((1, d, tn), lambda l, j: (l, 0, j)),
            pl.BlockSpec((1, 1, tn), lambda l, j: (l, 0, j)),
        ],
        out_specs=pl.BlockSpec((1, 8, tn), lambda l, j: (l, 0, j)),
        out_shape=jax.ShapeDtypeStruct((depth, 8, n), F32),
        compiler_params=_params("parallel", "parallel"),
        name="modulation",
    )(cvec, ada_w, ada_b.reshape(depth, 1, n))


def _evin_kernel(xp_ref, xc_ref, xn_ref, mod_ref, gain_ref, w_ref, wsm_ref, cw_ref, cb_ref, gb_ref, gs_ref,
                 q_ref, k_ref, v_ref, xs_ref, bc_ref, z_ref, g_ref, hext_ref, pext_ref):
    i = pl.program_id(1)
    nb = pl.num_programs(1)
    tm = xc_ref.shape[1]
    d = xc_ref.shape[2]
    halo = BF16_ROWS
    gain = gain_ref[...]
    shift = mod_ref[0, :, 0:d]
    scale = mod_ref[0, :, d:2 * d]

    hc = _norm_mod(xc_ref[0], gain, shift, scale)
    hp = _norm_mod(xp_ref[0], gain, shift, scale)
    hn = _norm_mod(xn_ref[0], gain, shift, scale)
    hext_ref[0:halo, :] = jnp.where(i > 0, hp, 0.0).astype(BF16)
    hext_ref[halo:halo + tm, :] = hc.astype(BF16)
    hext_ref[halo + tm:, :] = jnp.where(i < nb - 1, hn, 0.0).astype(BF16)

    gw = pext_ref.shape[1]
    pad = CONV_K // 2
    plan = ((q_ref, 0, DN_QK, DN_DK ** -0.5), (k_ref, DN_QK, DN_QK, 1.0), (v_ref, 2 * DN_QK, DN_VW, None),
            (xs_ref, 2 * DN_QK + DN_VW, SSD_INNER, None), (bc_ref, 2 * DN_QK + DN_VW + SSD_INNER, 2 * SSD_BC, None))
    for out_ref, base, width, l2_scale in plan:
        for off in range(0, width, gw):
            c0 = base + off
            pext_ref[...] = _dot(hext_ref[...], w_ref[:, c0:c0 + gw])
            acc = cb_ref[:, c0:c0 + gw] + cw_ref[0:1, c0:c0 + gw] * pext_ref[halo - pad:halo - pad + tm, :]
            for t in range(1, CONV_K):
                acc = acc + cw_ref[t:t + 1, c0:c0 + gw] * pext_ref[halo - pad + t:halo - pad + t + tm, :]
            u = _silu(acc)
            if l2_scale is None:
                out_ref[0, :, off:off + gw] = u
            else:
                for s in range(0, gw, LANE):
                    uh = u[:, s:s + LANE]
                    inv = lax.rsqrt(jnp.sum(uh * uh, axis=-1, keepdims=True) + EPS)
                    out_ref[0, :, off + s:off + s + LANE] = uh * (inv * l2_scale)

    hcb = hext_ref[halo:halo + tm, :]
    for off in range(0, EV_GATE, gw):
        z_ref[0, :, off:off + gw] = _dot(hcb, w_ref[:, EV_CONV + off:EV_CONV + off + gw])

    p = _dot(hc, wsm_ref[...]) + gb_ref[...]
    lane = lax.broadcasted_iota(jnp.int32, p.shape, 1)
    g_ref[0] = jnp.where(lane < 2 * DN_HEADS, _sigmoid(p), _softplus(p)) * gs_ref[...]


def _even_in(x, modv, gain, w_main, w_small, conv_w, conv_b, gate_bias, gate_scale):
    b, l, d = x.shape
    tm = min(ROW_TILE, l)
    nb = l // tm
    hb = tm // BF16_ROWS
    gw = 512
    row = lambda c: pl.BlockSpec((1, tm, c), lambda bi, i: (bi, i, 0))
    outs = (DN_QK, DN_QK, DN_VW, SSD_INNER, 2 * SSD_BC, EV_GATE, LANE)
    return pl.pallas_call(
        _evin_kernel,
        grid=(b, nb),
        in_specs=[
            pl.BlockSpec((1, BF16_ROWS, d), lambda bi, i: (bi, jnp.maximum(i * hb - 1, 0), 0)),
            row(d),
            pl.BlockSpec((1, BF16_ROWS, d), lambda bi, i: (bi, jnp.minimum((i + 1) * hb, l // BF16_ROWS - 1), 0)),
            pl.BlockSpec((1, 1, modv.shape[2]), lambda bi, i: (bi, 0, 0)),
            _resident(gain.shape), _resident(w_main.shape), _resident(w_small.shape), _resident(conv_w.shape),
            _resident(conv_b.shape), _resident(gate_bias.shape), _resident(gate_scale.shape),
        ],
        out_specs=[row(c) for c in outs],
        out_shape=[jax.ShapeDtypeStruct((b, l, c), F32) for c in outs],
        scratch_shapes=[pltpu.VMEM((tm + 2 * BF16_ROWS, d), BF16), pltpu.VMEM((tm + 2 * BF16_ROWS, gw), F32)],
        compiler_params=_params("parallel", "parallel"),
        name="even_in",
    )(x, x, x, modv, gain, w_main, w_small, conv_w, conv_b, gate_bias, gate_scale)


def _tri_masks(c):
    r = lax.broadcasted_iota(jnp.int32, (c, c), 0)
    s = lax.broadcasted_iota(jnp.int32, (c, c), 1)
    return r, s


TRI_BASE = 16


def _block_masks(r, s):
    same = lambda n: (r // n) == (s // n)
    masks = [same(TRI_BASE).astype(F32)]
    n = TRI_BASE
    while n < CHUNK:
        n *= 2
        masks.append(jnp.where(same(n), 1.0, 0.0) - jnp.where(same(n // 2), 1.0, 0.0))
    return masks


def _unit_tri_inverse(a, eye, blocks):
    p = -(a * blocks[0])
    x = eye + p
    for _ in range(int(math.log2(TRI_BASE)) - 1):
        p = _dot(p, p)
        x = x + _dot(x, p)
    for m in blocks[1:]:
        x = x - _dot(x, _dot(a * m, x))
    return x


def _dn_chunk(q, k, v, rows, state, incl, strict, inclf, eye, blocks, first_row):
    c = q.shape[0]
    cum_row = _dot_nt(rows, inclf)
    cols = _dot_nt(jnp.concatenate([inclf, eye], axis=0), rows)
    gc = cols[0:c, 1:2]
    beta = cols[c:2 * c, 0:1]
    total = gc[first_row:first_row + 1, :]
    decay = jnp.where(incl, jnp.exp(gc - cum_row[1:2, :]), 0.0)
    kb = k * beta
    a = jnp.where(strict, _dot_nt(kb, k) * decay, 0.0)
    t = _unit_tri_inverse(a, eye, blocks)
    egc = jnp.exp(gc)
    uw = _dot(t, jnp.concatenate([v * beta, kb * egc], axis=1))
    u = uw[:, 0:DN_DV]
    w = uw[:, DN_DV:]
    ws_qs = _dot(jnp.concatenate([w, q * egc], axis=0), state)
    v_new = u - ws_qs[0:c]
    attn = _dot_nt(q, k) * decay
    o = ws_qs[c:2 * c] + _dot(attn, v_new)
    state = state * jnp.exp(total) + _dot_tn(k * jnp.exp(total - gc), v_new)
    return o, state


def _dn_kernel(qf_ref, kf_ref, vf_ref, gf_ref, qb_ref, kb_ref, vb_ref, gb_ref, s0_ref,
               of_ref, ob_ref, sfin_ref, s_ref):
    i = pl.program_id(2)
    nck = gf_ref.shape[2]
    c = CHUNK

    @pl.when(i == 0)
    def _():
        s_ref[...] = s0_ref[0, 0]

    r, s = _tri_masks(c)
    eye = (r == s).astype(F32)
    lower, upper = r >= s, r <= s
    lowerf, upperf = lower.astype(F32), upper.astype(F32)
    blocks = _block_masks(r, s)

    def body(ci, carry):
        cf = ci
        cb = nck - 1 - ci
        rf = pl.ds(pl.multiple_of(cf * c, c), c)
        rb = pl.ds(pl.multiple_of(cb * c, c), c)
        o, st = _dn_chunk(qf_ref[0, rf, :], kf_ref[0, rf, :], vf_ref[0, rf, :], gf_ref[0, 0, cf][0:8],
                          s_ref[0], lower, r > s, lowerf, eye, blocks, c - 1)
        of_ref[0, rf, :] = o
        s_ref[0] = st
        o, st = _dn_chunk(qb_ref[0, rb, :], kb_ref[0, rb, :], vb_ref[0, rb, :], gb_ref[0, 0, cb][8:16],
                          s_ref[1], upper, r < s, upperf, eye, blocks, 0)
        ob_ref[0, rb, :] = o
        s_ref[1] = st
        return carry

    lax.fori_loop(0, nck, body, 0)

    @pl.when(i == pl.num_programs(2) - 1)
    def _():
        sfin_ref[0, 0] = s_ref[...]


def _dn_scan(q, k, v, grows, state0):
    b, l, _ = q.shape
    tl = min(SCAN_TILE, l)
    nb = l // tl
    nck = tl // CHUNK
    fwd = pl.BlockSpec((1, tl, LANE), lambda bi, h, i: (bi, i, h))
    bwd = pl.BlockSpec((1, tl, LANE), lambda bi, h, i: (bi, nb - 1 - i, h))
    gfwd = pl.BlockSpec((1, 1, nck, 16, CHUNK), lambda bi, h, i: (bi, h, i, 0, 0))
    gbwd = pl.BlockSpec((1, 1, nck, 16, CHUNK), lambda bi, h, i: (bi, h, nb - 1 - i, 0, 0))
    st = pl.BlockSpec((1, 1, 2, DN_DK, DN_DV), lambda bi, h, i: (bi, h, 0, 0, 0))
    return pl.pallas_call(
        _dn_kernel,
        grid=(b, DN_HEADS, nb),
        in_specs=[fwd, fwd, fwd, gfwd, bwd, bwd, bwd, gbwd, st],
        out_specs=[fwd, bwd, st],
        out_shape=[jax.ShapeDtypeStruct(q.shape, F32), jax.ShapeDtypeStruct(q.shape, F32),
                   jax.ShapeDtypeStruct(state0.shape, F32)],
        scratch_shapes=[pltpu.VMEM((2, DN_DK, DN_DV), F32)],
        compiler_params=_params("parallel", "parallel", "arbitrary"),
        name="deltanet_scan",
    )(q, k, v, grows, q, k, v, grows, state0)


def _ssd_chunk(x, bm, cm, rows, state, expand, incl, inclf, eye, first_row):
    c = x.shape[0]
    nh = SSD_GRP
    p = SSD_HEAD_DIM
    cum_row = _dot_nt(rows[nh:2 * nh], inclf)
    cols = _dot_nt(jnp.concatenate([inclf, eye], axis=0), rows)
    cum = cols[0:c, nh:2 * nh]
    dt = cols[c:2 * c, 0:nh]
    total = cum[first_row:first_row + 1, :]
    xdt = x * _dot(dt, expand)
    cb = _dot_nt(cm, bm)
    from_state = _dot(cm, state) * _dot(jnp.exp(cum), expand)
    parts = []
    for h in range(nh):
        seg = jnp.where(incl, jnp.exp(cum[:, h:h + 1] - cum_row[h:h + 1, :]), 0.0)
        parts.append(_dot(cb * seg, xdt[:, h * p:(h + 1) * p]))
    y = jnp.concatenate(parts, axis=1) + from_state
    carry_in = xdt * _dot(jnp.exp(total - cum), expand)
    state = state * _dot(jnp.exp(total), expand) + _dot_tn(bm, carry_in)
    return y, state


def _ssd_kernel(xf_ref, bf_ref, cf_ref, gf_ref, xb_ref, bb_ref, cb_ref, gb_ref, s0_ref, e_ref,
                yf_ref, yb_ref, sfin_ref, s_ref):
    i = pl.program_id(2)
    nck = gf_ref.shape[2]
    c = CHUNK
    nh = SSD_GRP

    @pl.when(i == 0)
    def _():
        s_ref[...] = s0_ref[0, 0]

    r, s = _tri_masks(c)
    eye = (r == s).astype(F32)
    lower, upper = r >= s, r <= s
    lowerf, upperf = lower.astype(F32), upper.astype(F32)
    expand = e_ref[...]

    def body(ci, carry):
        cf = ci
        cb = nck - 1 - ci
        rf = pl.ds(pl.multiple_of(cf * c, c), c)
        rb = pl.ds(pl.multiple_of(cb * c, c), c)
        y, st = _ssd_chunk(xf_ref[0, rf, :], bf_ref[0, rf, :], cf_ref[0, rf, :], gf_ref[0, 0, cf][0:2 * nh],
                           s_ref[0], expand, lower, lowerf, eye, c - 1)
        yf_ref[0, rf, :] = y
        s_ref[0] = st
        y, st = _ssd_chunk(xb_ref[0, rb, :], bb_ref[0, rb, :], cb_ref[0, rb, :], gb_ref[0, 0, cb][2 * nh:4 * nh],
                           s_ref[1], expand, upper, upperf, eye, 0)
        yb_ref[0, rb, :] = y
        s_ref[1] = st
        return carry

    lax.fori_loop(0, nck, body, 0)

    @pl.when(i == pl.num_programs(2) - 1)
    def _():
        sfin_ref[0, 0] = s_ref[...]


def _ssd_scan(xs, bc, grows, state0, expand):
    b, l, _ = xs.shape
    tl = min(SCAN_TILE, l)
    nb = l // tl
    nck = tl // CHUNK
    gwid = SSD_GRP * SSD_HEAD_DIM
    ng = SSD_GROUPS
    xf = pl.BlockSpec((1, tl, gwid), lambda bi, g, i: (bi, i, g))
    xb = pl.BlockSpec((1, tl, gwid), lambda bi, g, i: (bi, nb - 1 - i, g))
    bf = pl.BlockSpec((1, tl, SSD_STATE), lambda bi, g, i: (bi, i, g))
    cf = pl.BlockSpec((1, tl, SSD_STATE), lambda bi, g, i: (bi, i, ng + g))
    bb = pl.BlockSpec((1, tl, SSD_STATE), lambda bi, g, i: (bi, nb - 1 - i, g))
    cb = pl.BlockSpec((1, tl, SSD_STATE), lambda bi, g, i: (bi, nb - 1 - i, ng + g))
    gf = pl.BlockSpec((1, 1, nck, 4 * SSD_GRP, CHUNK), lambda bi, g, i: (bi, g, i, 0, 0))
    gb = pl.BlockSpec((1, 1, nck, 4 * SSD_GRP, CHUNK), lambda bi, g, i: (bi, g, nb - 1 - i, 0, 0))
    st = pl.BlockSpec((1, 1, 2, SSD_STATE, gwid), lambda bi, g, i: (bi, g, 0, 0, 0))
    return pl.pallas_call(
        _ssd_kernel,
        grid=(b, ng, nb),
        in_specs=[xf, bf, cf, gf, xb, bb, cb, gb, st, _resident(expand.shape)],
        out_specs=[xf, xb, st],
        out_shape=[jax.ShapeDtypeStruct(xs.shape, F32), jax.ShapeDtypeStruct(xs.shape, F32),
                   jax.ShapeDtypeStruct(state0.shape, F32)],
        scratch_shapes=[pltpu.VMEM((2, SSD_STATE, gwid), F32)],
        compiler_params=_params("parallel", "parallel", "arbitrary"),
        name="ssd_scan",
    )(xs, bc, bc, grows, xs, bc, bc, grows, state0, expand)


def _evmerge_kernel(of_ref, ob_ref, yf_ref, yb_ref, xs_ref, z_ref, dng_ref, sd_ref, sg_ref, a_ref, b_ref):
    for s in range(0, DN_VW, LANE):
        o = of_ref[0, :, s:s + LANE] + ob_ref[0, :, s:s + LANE]
        o = o * lax.rsqrt(jnp.mean(o * o, axis=-1, keepdims=True) + EPS) * dng_ref[...]
        a_ref[0, :, s:s + LANE] = (o * _silu(z_ref[0, :, s:s + LANE])).astype(BF16)
    gwid = SSD_INNER // SSD_GROUPS
    for s in range(0, SSD_INNER, gwid):
        y = yf_ref[0, :, s:s + gwid] + yb_ref[0, :, s:s + gwid] + sd_ref[:, s:s + gwid] * xs_ref[0, :, s:s + gwid]
        y = y * _silu(z_ref[0, :, DN_VW + s:DN_VW + s + gwid])
        y = y * lax.rsqrt(jnp.mean(y * y, axis=-1, keepdims=True) + EPS) * sg_ref[:, s:s + gwid]
        b_ref[0, :, s:s + gwid] = y.astype(BF16)


def _even_merge(o_f, o_b, y_f, y_b, xs, z, dn_gain, ssd_d, ssd_gain):
    b, l, _ = o_f.shape
    tm = min(ROW_TILE, l)
    row = lambda c: pl.BlockSpec((1, tm, c), lambda bi, i: (bi, i, 0))
    return pl.pallas_call(
        _evmerge_kernel,
        grid=(b, l // tm),
        in_specs=[row(DN_VW), row(DN_VW), row(SSD_INNER), row(SSD_INNER), row(SSD_INNER), row(EV_GATE),
                  _resident(dn_gain.shape), _resident(ssd_d.shape), _resident(ssd_gain.shape)],
        out_specs=[row(DN_VW), row(SSD_INNER)],
        out_shape=[jax.ShapeDtypeStruct((b, l, DN_VW), BF16), jax.ShapeDtypeStruct((b, l, SSD_INNER), BF16)],
        compiler_params=_params("parallel", "parallel"),
        name="even_merge",
    )(o_f, o_b, y_f, y_b, xs, z, dn_gain, ssd_d, ssd_gain)


def _mixffn_kernel(x_ref, a_ref, b_ref, mod_ref, gain_ref, wo_ref, wi_ref, wf_ref, o_ref, act_ref):
    d = x_ref.shape[2]
    na = a_ref.shape[2]
    hid = wf_ref.shape[0]
    g1 = mod_ref[0, :, 2 * d:3 * d]
    shift = mod_ref[0, :, 3 * d:4 * d]
    scale = mod_ref[0, :, 4 * d:5 * d]
    g2 = mod_ref[0, :, 5 * d:6 * d]
    mixed = _dot(a_ref[0], wo_ref[0:na, :]) + _dot(b_ref[0], wo_ref[na:, :])
    x1 = x_ref[0] + g1 * mixed
    h = _norm_mod(x1, gain_ref[...], shift, scale).astype(BF16)
    half = hid // 2
    for s in range(0, hid, half):
        gate = _dot(h, wi_ref[:, s:s + half])
        up = _dot(h, wi_ref[:, hid + s:hid + s + half])
        act_ref[:, s:s + half] = (_silu(gate) * up).astype(BF16)
    o_ref[0] = x1 + g2 * _dot(act_ref[...], wf_ref[...])


def _mix_ffn(x, a, bmix, modv, gain, w_out, w_in, w_ffn_out):
    b, l, d = x.shape
    tm = min(ROW_TILE, l)
    hid = w_ffn_out.shape[0]
    row = lambda c: pl.BlockSpec((1, tm, c), lambda bi, i: (bi, i, 0))
    return pl.pallas_call(
        _mixffn_kernel,
        grid=(b, l // tm),
        in_specs=[row(d), row(a.shape[2]), row(bmix.shape[2]),
                  pl.BlockSpec((1, 1, modv.shape[2]), lambda bi, i: (bi, 0, 0)),
                  _resident(gain.shape), _resident(w_out.shape), _resident(w_in.shape), _resident(w_ffn_out.shape)],
        out_specs=row(d),
        out_shape=jax.ShapeDtypeStruct(x.shape, F32),
        scratch_shapes=[pltpu.VMEM((tm, hid), BF16)],
        compiler_params=_params("parallel", "parallel"),
        name="mix_ffn",
    )(x, a, bmix, modv, gain, w_out, w_in, w_ffn_out)


def _rope(x, cos, sin):
    lane = lax.broadcasted_iota(jnp.int32, x.shape, 1)
    partner = jnp.where(lane % 32 < 16, pltpu.roll(x, LANE - 16, 1), pltpu.roll(x, 16, 1))
    return x * cos + partner * sin


def _odin_kernel(x_ref, mod_ref, gain_ref, w_ref, qan_ref, wuq_ref, kvan_ref, wukv_ref, dqn_ref, dkn_ref,
                 mqn_ref, mkn_ref, cos_ref, sin_ref, *out_refs, with_q, with_rope):
    if with_q:
        daq_ref, dak_ref, dav_ref, mq_ref, mk_ref, mv_ref = out_refs
    else:
        dak_ref, dav_ref, mk_ref, mv_ref = out_refs
    d = x_ref.shape[2]
    h = _norm_mod(x_ref[0], gain_ref[...], mod_ref[0, :, 0:d], mod_ref[0, :, d:2 * d]).astype(BF16)
    if with_rope:
        cos = cos_ref[...]
        sin = sin_ref[...]
    lane = lax.broadcasted_iota(jnp.int32, (h.shape[0], LANE), 1)
    low = lane < DA_DIM

    def pair_norm(u, g):
        sq = u * u
        lo = jnp.sum(jnp.where(low, sq, 0.0), axis=-1, keepdims=True)
        hi = jnp.sum(sq, axis=-1, keepdims=True) - lo
        ms = jnp.where(low, lo, hi) * (1.0 / DA_DIM)
        return u * lax.rsqrt(ms + EPS) * g

    def da_part(out_ref, base, g, out_scale):
        p = _dot(h, w_ref[:, base:base + DA_QK])
        for s in range(0, DA_QK, LANE):
            u = pair_norm(p[:, s:s + LANE], g)
            if with_rope:
                u = _rope(u, cos, sin)
            out_ref[0, :, s:s + LANE] = (u * out_scale).astype(BF16)

    if with_q:
        da_part(daq_ref, 0, dqn_ref[...], DA_DIM ** -0.5 * LOG2E)
    da_part(dak_ref, DA_QK, dkn_ref[...], 1.0)
    dav_ref[0] = _dot(h, w_ref[:, 2 * DA_QK:2 * DA_QK + DA_VW]).astype(BF16)

    def low_norm(u, g):
        ms = jnp.sum(u * u, axis=-1, keepdims=True) * (1.0 / MLA_ROPE)
        return u * lax.rsqrt(ms + EPS) * g

    def full_norm(u, g):
        return u * lax.rsqrt(jnp.mean(u * u, axis=-1, keepdims=True) + EPS) * g

    c0 = 2 * DA_QK + DA_VW
    if with_q:
        cq = full_norm(_dot(h, w_ref[:, c0:c0 + MLA_Q_RANK]), qan_ref[...]).astype(BF16)
        qf = _dot(cq, wuq_ref[...])
        scale = (MLA_NOPE + MLA_ROPE) ** -0.5 * LOG2E
        for hd in range(MLA_HEADS):
            s = hd * MLA_QK_PAD
            mq_ref[0, :, s:s + LANE] = (full_norm(qf[:, s:s + LANE], mqn_ref[:, 0:LANE]) * scale).astype(BF16)
            u = low_norm(qf[:, s + LANE:s + 2 * LANE], mqn_ref[:, LANE:2 * LANE])
            if with_rope:
                u = _rope(u, cos, sin)
            mq_ref[0, :, s + LANE:s + 2 * LANE] = (u * scale).astype(BF16)
    c0 += MLA_Q_RANK
    ckv = full_norm(_dot(h, w_ref[:, c0:c0 + MLA_KV_RANK]), kvan_ref[...]).astype(BF16)
    kvu = _dot(ckv, wukv_ref[...])
    c0 += MLA_KV_RANK
    kr = low_norm(_dot(h, w_ref[:, c0:c0 + LANE]), mkn_ref[:, LANE:2 * LANE])
    if with_rope:
        kr = _rope(kr, cos, sin)
    kr = kr.astype(BF16)
    for hd in range(MLA_HEADS):
        s = hd * MLA_QK_PAD
        mk_ref[0, :, s:s + LANE] = full_norm(kvu[:, hd * LANE:(hd + 1) * LANE], mkn_ref[:, 0:LANE]).astype(BF16)
        mk_ref[0, :, s + LANE:s + 2 * LANE] = kr
    mv_ref[0] = kvu[:, MLA_HEADS * MLA_NOPE:].astype(BF16)


def _odd_in(x, modv, gain, w_in, q_a_gain, w_uq, kv_a_gain, w_ukv, dq_gain, dk_gain, mq_gain, mk_gain,
            cos_t, sin_t, with_q, with_rope):
    b, l, d = x.shape
    tm = min(ROW_TILE, l)
    row = lambda c: pl.BlockSpec((1, tm, c), lambda bi, i: (bi, i, 0))
    widths = (DA_QK, DA_VW, MLA_HEADS * MLA_QK_PAD, MLA_HEADS * MLA_V)
    if with_q:
        widths = (DA_QK,) + widths[:2] + (MLA_HEADS * MLA_QK_PAD,) + widths[2:]
    table = pl.BlockSpec((tm, LANE), lambda bi, i: (i, 0))
    res = [gain, w_in, q_a_gain, w_uq, kv_a_gain, w_ukv, dq_gain, dk_gain, mq_gain, mk_gain]
    return pl.pallas_call(
        functools.partial(_odin_kernel, with_q=with_q, with_rope=with_rope),
        grid=(b, l // tm),
        in_specs=[row(d), pl.BlockSpec((1, 1, modv.shape[2]), lambda bi, i: (bi, 0, 0))]
        + [_resident(a.shape) for a in res] + [table, table],
        out_specs=[row(c) for c in widths],
        out_shape=[jax.ShapeDtypeStruct((b, l, c), BF16) for c in widths],
        compiler_params=_params("parallel", "parallel"),
        name="odd_in_q" if with_q else "odd_in_kv",
    )(x, modv, *res, cos_t, sin_t)


def _flash_scratch(nchain, strip, tk, qk_width):
    half = [pltpu.VMEM((nchain, tk, strip), F32), pltpu.VMEM((nchain, 1, strip), F32)]
    return [pltpu.VMEM((nchain, 1, strip), F32), pltpu.VMEM((nchain, V_ROWS, strip), F32),
            pltpu.VMEM((nchain, strip, qk_width), BF16)] + half + half


def _flash_loop(q_ref, k_ref, vt_ref, scratch, strip, split_q):
    m_ref, acc_ref, qz_ref, s0_ref, mx0_ref, s1_ref, mx1_ref = scratch
    s_ref, mx_ref = (s0_ref, s1_ref), (mx0_ref, mx1_ref)
    tq = q_ref.shape[1]
    tk = vt_ref.shape[4]
    nk = vt_ref.shape[2]
    nchain = m_ref.shape[0]
    m_ref[...] = jnp.full(m_ref.shape, -jnp.inf, F32)
    acc_ref[...] = jnp.zeros(acc_ref.shape, F32)
    lane = lax.broadcasted_iota(jnp.int32, (strip, q_ref.shape[2]), 1)
    for st in range(tq // strip):
        q = q_ref[0, pl.ds(st * strip, strip), :]
        if split_q:
            zero = jnp.zeros_like(q)
            qz_ref[2 * st] = jnp.where(lane < DA_DIM, q, zero)
            qz_ref[2 * st + 1] = jnp.where(lane < DA_DIM, zero, q)
        else:
            qz_ref[st] = q

    def scores(j, slot):
        k = k_ref[0, pl.ds(pl.multiple_of(j * tk, tk), tk), :]
        for c in range(nchain):
            st = _dot_nt(k, qz_ref[c])
            s_ref[slot][c] = st
            mx_ref[slot][c] = jnp.max(st, axis=0, keepdims=True)

    def update(j, slot):
        vt1 = vt_ref[0, 0, j]
        probs, alphas = [], []
        for c in range(nchain):
            m_old = m_ref[c]
            m_new = jnp.maximum(m_old, mx_ref[slot][c])
            probs.append(jnp.exp2(s_ref[slot][c] - m_new).astype(BF16))
            alphas.append(jnp.exp2(m_old - m_new))
            m_ref[c] = m_new
        for c in range(nchain):
            acc_ref[c] = alphas[c] * acc_ref[c] + _dot(vt1, probs[c])

    scores(0, 0)

    def body(i, carry):
        scores(2 * i + 1, 1)
        update(2 * i, 0)
        scores(jnp.minimum(2 * i + 2, nk - 1), 0)
        update(2 * i + 1, 1)
        return carry

    lax.fori_loop(0, nk // 2, body, 0)
    if nk % 2:
        update(nk - 1, 0)


def _softmax_out(acc):
    return acc[0:MLA_V] / acc[MLA_V:MLA_V + 1]


def _da_kernel(q_ref, k_ref, vt_ref, lam_ref, gain_ref, o_ref, *scratch, strip, lambda_init):
    _flash_loop(q_ref, k_ref, vt_ref, scratch, strip, True)
    acc_ref = scratch[1]
    lp = lam_ref[...]
    lam = (jnp.exp(jnp.sum(lp[0:1] * lp[1:2], axis=-1, keepdims=True))
           - jnp.exp(jnp.sum(lp[2:3] * lp[3:4], axis=-1, keepdims=True)) + lambda_init)
    for st in range(q_ref.shape[1] // strip):
        o = _softmax_out(acc_ref[2 * st]) - lam * _softmax_out(acc_ref[2 * st + 1])
        o = o * lax.rsqrt(jnp.mean(o * o, axis=0, keepdims=True) + EPS) * gain_ref[...]
        o_ref[0, pl.ds(st * strip, strip), :] = (o * (1.0 - lambda_init)).T.astype(BF16)


def _values_t(v, heads, tk):
    b, lk, _ = v.shape
    vt = jnp.transpose(v.reshape(b, lk // tk, tk, heads, MLA_V), (0, 3, 1, 4, 2))
    return jnp.concatenate([vt, jnp.ones((b, heads, lk // tk, V_ROWS - MLA_V, tk), v.dtype)], axis=3)


def _diff_attention(q, k, v, lam_p, gain_col, lambda_init):
    b, lq, _ = q.shape
    lk = k.shape[1]
    tq = min(DA_TQ, lq)
    strip = min(ATT_STRIP, tq)
    vt = _values_t(v, DA_HEADS, ATT_TK)
    qs = pl.BlockSpec((1, tq, LANE), lambda bi, h, i: (bi, i, h))
    ks = pl.BlockSpec((1, lk, LANE), lambda bi, h, i: (bi, 0, h))
    vs = pl.BlockSpec((1, 1) + vt.shape[2:], lambda bi, h, i: (bi, h, 0, 0, 0))
    nchain = 2 * (tq // strip)
    return pl.pallas_call(
        functools.partial(_da_kernel, strip=strip, lambda_init=lambda_init),
        grid=(b, DA_HEADS, lq // tq),
        in_specs=[qs, ks, vs, _resident(lam_p.shape), _resident(gain_col.shape)],
        out_specs=qs,
        out_shape=jax.ShapeDtypeStruct(q.shape, BF16),
        scratch_shapes=_flash_scratch(nchain, strip, ATT_TK, LANE),
        compiler_params=_params("parallel", "parallel", "parallel"),
        name="diff_attention",
    )(q, k, vt, lam_p, gain_col)


def _mla_kernel(q_ref, k_ref, vt_ref, o_ref, *scratch, strip):
    _flash_loop(q_ref, k_ref, vt_ref, scratch, strip, False)
    acc_ref = scratch[1]
    for st in range(q_ref.shape[1] // strip):
        o_ref[0, pl.ds(st * strip, strip), :] = _softmax_out(acc_ref[st]).T.astype(BF16)


def _mla_attention(q, k, v):
    b, lq, _ = q.shape
    lk = k.shape[1]
    tq = min(MLA_TQ, lq)
    strip = min(ATT_STRIP, tq)
    vt = _values_t(v, MLA_HEADS, ATT_TK)
    qs = pl.BlockSpec((1, tq, MLA_QK_PAD), lambda bi, h, i: (bi, i, h))
    ks = pl.BlockSpec((1, lk, MLA_QK_PAD), lambda bi, h, i: (bi, 0, h))
    vs = pl.BlockSpec((1, 1) + vt.shape[2:], lambda bi, h, i: (bi, h, 0, 0, 0))
    nchain = tq // strip
    return pl.pallas_call(
        functools.partial(_mla_kernel, strip=strip),
        grid=(b, MLA_HEADS, lq // tq),
        in_specs=[qs, ks, vs],
        out_specs=pl.BlockSpec((1, tq, MLA_V), lambda bi, h, i: (bi, i, h)),
        out_shape=jax.ShapeDtypeStruct((b, lq, MLA_HEADS * MLA_V), BF16),
        scratch_shapes=_flash_scratch(nchain, strip, ATT_TK, MLA_QK_PAD),
        compiler_params=_params("parallel", "parallel", "parallel"),
        name="mla_attention",
    )(q, k, vt)


def _scan_rows(gates_t, index, chunks):
    b, _, l = gates_t.shape
    idx = jnp.asarray(index, jnp.int32)
    rows = gates_t[:, idx, :]
    rows = rows.reshape(b, idx.shape[0], idx.shape[1], chunks, l // chunks)
    return jnp.swapaxes(rows, 2, 3)


def _even_layer(x, ctx, mod_x, mod_c, p):
    b = x.shape[0]
    nh, ns = DN_HEADS, SSD_HEADS
    pad_col = LANE - 1
    dn_index = [[h, 2 * nh + h] + [pad_col] * 6 + [nh + h, 3 * nh + h] + [pad_col] * 6 for h in range(nh)]
    dt0, a0 = 4 * nh, 4 * nh + 2 * ns
    ssd_index = [[dt0 + g * SSD_GRP + r for r in range(SSD_GRP)] + [a0 + g * SSD_GRP + r for r in range(SSD_GRP)]
                 + [dt0 + ns + g * SSD_GRP + r for r in range(SSD_GRP)] + [a0 + ns + g * SSD_GRP + r for r in range(SSD_GRP)]
                 for g in range(SSD_GROUPS)]

    def prepare(t, modv):
        q, k, v, xs, bc, z, gates = _even_in(t, modv, p["gain_mix"], p["w_main"], p["w_small"], p["conv_w"], p["conv_b"],
                                             p["gate_bias"], p["gate_scale"])
        gt = jnp.swapaxes(gates, 1, 2)
        nchunk = t.shape[1] // CHUNK
        return dict(q=q, k=k, v=v, xs=xs, bc=bc, z=z, dn_rows=_scan_rows(gt, dn_index, nchunk),
                    ssd_rows=_scan_rows(gt, ssd_index, nchunk))

    pc = prepare(ctx, mod_c)
    pL = prepare(x, mod_x)
    dn0 = jnp.zeros((b, nh, 2, DN_DK, DN_DV), F32)
    ssd0 = jnp.zeros((b, SSD_GROUPS, 2, SSD_STATE, SSD_GRP * SSD_HEAD_DIM), F32)
    ocf, ocb, dn1 = _dn_scan(pc["q"], pc["k"], pc["v"], pc["dn_rows"], dn0)
    olf, olb, _ = _dn_scan(pL["q"], pL["k"], pL["v"], pL["dn_rows"], dn1)
    ycf, ycb, ssd1 = _ssd_scan(pc["xs"], pc["bc"], pc["ssd_rows"], ssd0, p["expand"])
    ylf, ylb, _ = _ssd_scan(pL["xs"], pL["bc"], pL["ssd_rows"], ssd1, p["expand"])

    def finish(t, modv, pp, of, ob, yf, yb):
        a, bm = _even_merge(of, ob, yf, yb, pp["xs"], pp["z"], p["dn_gain"], p["ssd_d"], p["ssd_gain"])
        return _mix_ffn(t, a, bm, modv, p["gain_ffn"], p["w_out"], p["ffn_w_in"], p["ffn_w_out"])

    return finish(x, mod_x, pL, olf, olb, ylf, ylb), finish(ctx, mod_c, pc, ocf, ocb, ycf, ycb)


def _rope_tables(n_tokens):
    lane = jnp.arange(LANE)
    quarter = (lane % 64) // 16
    n_freq = DA_DIM // 4
    inv_freq = ROPE_THETA ** (-(lane % 16).astype(F32) / n_freq)
    tok = jnp.arange(n_tokens)
    pos = jnp.where(quarter[None, :] < 2, (tok // GRID_W)[:, None], (tok % GRID_W)[:, None]).astype(F32)
    ang = pos * inv_freq[None, :]
    sign = jnp.where(quarter % 2 == 0, -1.0, 1.0).astype(F32)
    return jnp.cos(ang), jnp.sin(ang) * sign[None, :]


def _odd_layer(x, ctx, mod_x, mod_c, p, lambda_init):
    cos_t, sin_t = _rope_tables(x.shape[1])
    args = (p["gain_mix"], p["w_in"], p["q_a_gain"], p["w_uq"], p["kv_a_gain"], p["w_ukv"], p["dq_gain"], p["dk_gain"],
            p["mq_gain"], p["mk_gain"])
    daq, dak, dav, mq, mk, mv = _odd_in(x, mod_x, *args, cos_t, sin_t, True, True)
    cak, cav, cmk, cmv = _odd_in(ctx, mod_c, *args, cos_t[:ctx.shape[1]], sin_t[:ctx.shape[1]], False, False)
    cat = lambda a, c: jnp.concatenate([a, c], axis=1)
    da = _diff_attention(daq, cat(dak, cak), cat(dav, cav), p["da_lambda"], p["sub_gain"], lambda_init)
    ml = _mla_attention(mq, cat(mk, cmk), cat(mv, cmv))
    return _mix_ffn(x, da, ml, mod_x, p["gain_ffn"], p["w_out"], p["ffn_w_in"], p["ffn_w_out"])


def _even_params(i, j, norm_mix, norm_ffn, ffn_w_in, ffn_w_out, ev_w_in, ev_conv_w, ev_conv_b, dn_a_log, dn_dt_bias,
                 dn_norm, ssd_a_log, ssd_dt_bias, ssd_d, ssd_norm, ev_w_out):
    d = norm_mix.shape[1]
    w = ev_w_in[j]
    small = w[:, EV_CONV + EV_GATE:]
    nh, ns = DN_HEADS, SSD_HEADS
    w_small = jnp.concatenate([small, small[:, 4 * nh:], jnp.zeros((d, LANE - 4 * nh - 4 * ns), F32)], axis=1)
    zeros = lambda n: jnp.zeros((n,), F32)
    gate_bias = jnp.concatenate([zeros(2 * nh), dn_dt_bias[j].reshape(-1), ssd_dt_bias[j].reshape(-1),
                                 ssd_dt_bias[j].reshape(-1), zeros(LANE - 4 * nh - 4 * ns)])
    gate_scale = jnp.concatenate([jnp.ones((2 * nh,), F32), -jnp.exp(dn_a_log[j].reshape(-1)), jnp.ones((2 * ns,), F32),
                                  -jnp.exp(ssd_a_log[j].reshape(-1)), zeros(LANE - 4 * nh - 4 * ns)])
    expand = jnp.repeat(jnp.eye(SSD_GRP, dtype=F32), SSD_HEAD_DIM, axis=1)
    return dict(
        gain_mix=norm_mix[i].reshape(1, d), gain_ffn=norm_ffn[i].reshape(1, d),
        w_main=w[:, :EV_CONV + EV_GATE].astype(BF16), w_small=w_small,
        conv_w=ev_conv_w[j], conv_b=ev_conv_b[j].reshape(1, -1),
        gate_bias=gate_bias.reshape(1, LANE), gate_scale=gate_scale.reshape(1, LANE), expand=expand,
        dn_gain=dn_norm[j].reshape(1, DN_DV), ssd_d=jnp.repeat(ssd_d[j], SSD_HEAD_DIM).reshape(1, SSD_INNER),
        ssd_gain=ssd_norm[j].reshape(1, SSD_INNER), w_out=ev_w_out[j].astype(BF16),
        ffn_w_in=ffn_w_in[i].astype(BF16), ffn_w_out=ffn_w_out[i].astype(BF16))


def _odd_params(i, j, norm_mix, norm_ffn, ffn_w_in, ffn_w_out, od_w_in, da_q_norm, da_k_norm, da_lambda, da_sub_norm,
                mla_q_a_norm, mla_w_uq, mla_kv_a_norm, mla_w_ukv, mla_q_norm, mla_k_norm, od_w_out):
    d = norm_mix.shape[1]
    w = od_w_in[j]
    w_in = jnp.concatenate([w, jnp.zeros((d, LANE - MLA_ROPE), F32)], axis=1).astype(BF16)
    hq = mla_w_uq[j].reshape(MLA_Q_RANK, MLA_HEADS, MLA_NOPE + MLA_ROPE)
    hq = jnp.pad(hq, ((0, 0), (0, 0), (0, MLA_QK_PAD - MLA_NOPE - MLA_ROPE)))
    hkv = mla_w_ukv[j].reshape(MLA_KV_RANK, MLA_HEADS, MLA_NOPE + MLA_V)
    w_ukv = jnp.concatenate([hkv[:, :, :MLA_NOPE].reshape(MLA_KV_RANK, -1), hkv[:, :, MLA_NOPE:].reshape(MLA_KV_RANK, -1)], axis=1)
    pad_gain = lambda g: jnp.pad(g, (0, MLA_QK_PAD - MLA_NOPE - MLA_ROPE)).reshape(1, MLA_QK_PAD)
    return dict(
        gain_mix=norm_mix[i].reshape(1, d), gain_ffn=norm_ffn[i].reshape(1, d), w_in=w_in,
        q_a_gain=mla_q_a_norm[j].reshape(1, -1), w_uq=hq.reshape(MLA_Q_RANK, -1).astype(BF16),
        kv_a_gain=mla_kv_a_norm[j].reshape(1, -1), w_ukv=w_ukv.astype(BF16),
        dq_gain=jnp.tile(da_q_norm[j], 2).reshape(1, LANE), dk_gain=jnp.tile(da_k_norm[j], 2).reshape(1, LANE),
        mq_gain=pad_gain(mla_q_norm[j]), mk_gain=pad_gain(mla_k_norm[j]),
        da_lambda=jnp.pad(da_lambda[j], ((0, 4), (0, LANE - DA_DIM))), sub_gain=da_sub_norm[j].reshape(2 * DA_DIM, 1),
        w_out=od_w_out[j].astype(BF16), ffn_w_in=ffn_w_in[i].astype(BF16), ffn_w_out=ffn_w_out[i].astype(BF16))


def kernel(x, c, ctx, c_ctx, ada_w, ada_b, norm_mix, norm_ffn, ffn_w_in, ffn_w_out, ev_w_in, ev_conv_w, ev_conv_b, dn_a_log, dn_dt_bias, dn_norm, ssd_a_log, ssd_dt_bias, ssd_d, ssd_norm, ev_w_out, od_w_in, da_q_norm, da_k_norm, da_lambda, da_sub_norm, mla_q_a_norm, mla_w_uq, mla_kv_a_norm, mla_w_ukv, mla_q_norm, mla_k_norm, od_w_out):
    b, _, d = x.shape
    depth = ada_w.shape[0]
    assert b < 8
    cvec = jnp.concatenate([c, c_ctx[None, :], jnp.zeros((8 - b - 1, d), F32)], axis=0)
    mod = _modulation(cvec, ada_w, ada_b)
    for i in range(depth):
        last = i == depth - 1
        j = i // 2
        mod_x = mod[i, :b].reshape(b, 1, 6 * d)
        mod_c = jnp.broadcast_to(mod[i, b].reshape(1, 1, 6 * d), (b, 1, 6 * d))
        if i % 2 == 0:
            p = _even_params(i, j, norm_mix, norm_ffn, ffn_w_in, ffn_w_out, ev_w_in, ev_conv_w, ev_conv_b, dn_a_log,
                             dn_dt_bias, dn_norm, ssd_a_log, ssd_dt_bias, ssd_d, ssd_norm, ev_w_out)
            x, ctx_new = _even_layer(x, ctx, mod_x, mod_c, p)
        else:
            p = _odd_params(i, j, norm_mix, norm_ffn, ffn_w_in, ffn_w_out, od_w_in, da_q_norm, da_k_norm, da_lambda,
                            da_sub_norm, mla_q_a_norm, mla_w_uq, mla_kv_a_norm, mla_w_ukv, mla_q_norm, mla_k_norm, od_w_out)
            lambda_init = 0.8 - 0.6 * math.exp(-0.3 * i)
            if last:
                x = _odd_layer(x, ctx, mod_x, mod_c, p, lambda_init)
                ctx_new = ctx
            else:
                raise NotImplementedError("context update after an attention layer is not needed for depth 2")
        ctx = ctx_new
    return x
```

```python
import functools
import math

import jax
import jax.numpy as jnp
from jax import lax
from jax.experimental import pallas as pl
from jax.experimental.pallas import tpu as pltpu

F32 = jnp.float32
BF16 = jnp.bfloat16

GRID_W = 64
EPS = 1e-6
ROPE_THETA = 10000.0
CONV_K = 5
DN_HEADS = 8
DN_DK = 128
DN_DV = 128
SSD_HEADS = 16
SSD_HEAD_DIM = 64
SSD_GROUPS = 2
SSD_STATE = 128
SSD_GRP = SSD_HEADS // SSD_GROUPS
SSD_INNER = SSD_HEADS * SSD_HEAD_DIM
DA_HEADS = 8
DA_DIM = 64
MLA_HEADS = 8
MLA_Q_RANK = 512
MLA_KV_RANK = 256
MLA_NOPE = 128
MLA_ROPE = 64
MLA_V = 128
MLA_QK_PAD = 256

DN_QK = DN_HEADS * DN_DK
DN_VW = DN_HEADS * DN_DV
SSD_BC = SSD_GROUPS * SSD_STATE
EV_CONV = 2 * DN_QK + DN_VW + SSD_INNER + 2 * SSD_BC
EV_GATE = DN_VW + SSD_INNER
DA_QK = DA_HEADS * 2 * DA_DIM
DA_VW = DA_HEADS * 2 * DA_DIM

LANE = 128
BF16_ROWS = 16
CHUNK = 128
VMEM_LIMIT = 56 * 1024 * 1024

ROW_TILE = 512
SCAN_TILE = 512
DN_TILE = 256
DA_TQ = 512
MLA_TQ = 1024
ATT_TK = 256
ATT_STRIP = 256
V_ROWS = MLA_V + BF16_ROWS
LOG2E = 1.4426950408889634


def _dot(a, b):
    return jnp.dot(a, b, preferred_element_type=F32)


def _dot_nt(a, b):
    return lax.dot_general(a, b, (((1,), (1,)), ((), ())), preferred_element_type=F32)


def _dot_tn(a, b):
    return lax.dot_general(a, b, (((0,), (0,)), ((), ())), preferred_element_type=F32)


def _sigmoid(x):
    return 1.0 / (1.0 + jnp.exp(-x))


def _silu(x):
    return x * _sigmoid(x)


def _softplus(x):
    return jnp.maximum(x, 0.0) + jnp.log(1.0 + jnp.exp(-jnp.abs(x)))


def _params(*sem):
    return pltpu.CompilerParams(dimension_semantics=sem, vmem_limit_bytes=VMEM_LIMIT)


def _resident(shape):
    nd = len(shape)
    return pl.BlockSpec(shape, lambda *_: (0,) * nd, pipeline_mode=pl.Buffered(1))


def _norm_mod(x, gain, shift, scale):
    y = x * lax.rsqrt(jnp.mean(x * x, axis=-1, keepdims=True) + EPS)
    return (y * gain) * (1.0 + scale) + shift


def _mod_kernel(c_ref, w_ref, b_ref, o_ref):
    o_ref[0] = _dot(_silu(c_ref[...]), w_ref[0]) + b_ref[0]


def _modulation(cvec, ada_w, ada_b):
    depth, d, n = ada_w.shape
    tn = n // 4
    return pl.pallas_call(
        _mod_kernel,
        grid=(depth, n // tn),
        in_specs=[
            pl.BlockSpec((8, d), lambda l, j: (0, 0)),
            pl.BlockSpec((1, d, tn), lambda l, j: (l, 0, j)),
            pl.BlockSpec((1, 1, tn), lambda l, j: (l, 0, j)),
        ],
        out_specs=pl.BlockSpec((1, 8, tn), lambda l, j: (l, 0, j)),
        out_shape=jax.ShapeDtypeStruct((depth, 8, n), F32),
        compiler_params=_params("parallel", "parallel"),
        name="modulation",
    )(cvec, ada_w, ada_b.reshape(depth, 1, n))


def _evin_kernel(xp_ref, xc_ref, xn_ref, mod_ref, gain_ref, w_ref, wsm_ref, cw_ref, cb_ref, gb_ref, gs_ref,
                 q_ref, k_ref, v_ref, xs_ref, bc_ref, z_ref, g_ref, hext_ref, pext_ref):
    i = pl.program_id(1)
    nb = pl.num_programs(1)
    tm = xc_ref.shape[1]
    d = xc_ref.shape[2]
    halo = BF16_ROWS
    gain = gain_ref[...]
    shift = mod_ref[0, :, 0:d]
    scale = mod_ref[0, :, d:2 * d]

    hc = _norm_mod(xc_ref[0], gain, shift, scale)
    hp = _norm_mod(xp_ref[0], gain, shift, scale)
    hn = _norm_mod(xn_ref[0], gain, shift, scale)
    hext_ref[0:halo, :] = jnp.where(i > 0, hp, 0.0).astype(BF16)
    hext_ref[halo:halo + tm, :] = hc.astype(BF16)
    hext_ref[halo + tm:, :] = jnp.where(i < nb - 1, hn, 0.0).astype(BF16)

    gw = pext_ref.shape[1]
    pad = CONV_K // 2
    plan = ((q_ref, 0, DN_QK, DN_DK ** -0.5), (k_ref, DN_QK, DN_QK, 1.0), (v_ref, 2 * DN_QK, DN_VW, None),
            (xs_ref, 2 * DN_QK + DN_VW, SSD_INNER, None), (bc_ref, 2 * DN_QK + DN_VW + SSD_INNER, 2 * SSD_BC, None))
    for out_ref, base, width, l2_scale in plan:
        for off in range(0, width, gw):
            c0 = base + off
            pext_ref[...] = _dot(hext_ref[...], w_ref[:, c0:c0 + gw])
            acc = cb_ref[:, c0:c0 + gw] + cw_ref[0:1, c0:c0 + gw] * pext_ref[halo - pad:halo - pad + tm, :]
            for t in range(1, CONV_K):
                acc = acc + cw_ref[t:t + 1, c0:c0 + gw] * pext_ref[halo - pad + t:halo - pad + t + tm, :]
            u = _silu(acc)
            if l2_scale is None:
                out_ref[0, :, off:off + gw] = u
            else:
                for s in range(0, gw, LANE):
                    uh = u[:, s:s + LANE]
                    inv = lax.rsqrt(jnp.sum(uh * uh, axis=-1, keepdims=True) + EPS)
                    out_ref[0, :, off + s:off + s + LANE] = uh * (inv * l2_scale)

    hcb = hext_ref[halo:halo + tm, :]
    for off in range(0, EV_GATE, gw):
        z_ref[0, :, off:off + gw] = _dot(hcb, w_ref[:, EV_CONV + off:EV_CONV + off + gw])

    p = _dot(hc, wsm_ref[...]) + gb_ref[...]
    lane = lax.broadcasted_iota(jnp.int32, p.shape, 1)
    g_ref[0] = jnp.where(lane < 2 * DN_HEADS, _sigmoid(p), _softplus(p)) * gs_ref[...]


def _even_in(x, modv, gain, w_main, w_small, conv_w, conv_b, gate_bias, gate_scale):
    b, l, d = x.shape
    tm = min(ROW_TILE, l)
    nb = l // tm
    hb = tm // BF16_ROWS
    gw = 512
    row = lambda c: pl.BlockSpec((1, tm, c), lambda bi, i: (bi, i, 0))
    outs = (DN_QK, DN_QK, DN_VW, SSD_INNER, 2 * SSD_BC, EV_GATE, LANE)
    return pl.pallas_call(
        _evin_kernel,
        grid=(b, nb),
        in_specs=[
            pl.BlockSpec((1, BF16_ROWS, d), lambda bi, i: (bi, jnp.maximum(i * hb - 1, 0), 0)),
            row(d),
            pl.BlockSpec((1, BF16_ROWS, d), lambda bi, i: (bi, jnp.minimum((i + 1) * hb, l // BF16_ROWS - 1), 0)),
            pl.BlockSpec((1, 1, modv.shape[2]), lambda bi, i: (bi, 0, 0)),
            _resident(gain.shape), _resident(w_main.shape), _resident(w_small.shape), _resident(conv_w.shape),
            _resident(conv_b.shape), _resident(gate_bias.shape), _resident(gate_scale.shape),
        ],
        out_specs=[row(c) for c in outs],
        out_shape=[jax.ShapeDtypeStruct((b, l, c), F32) for c in outs],
        scratch_shapes=[pltpu.VMEM((tm + 2 * BF16_ROWS, d), BF16), pltpu.VMEM((tm + 2 * BF16_ROWS, gw), F32)],
        compiler_params=_params("parallel", "parallel"),
        name="even_in",
    )(x, x, x, modv, gain, w_main, w_small, conv_w, conv_b, gate_bias, gate_scale)


def _tri_masks(c):
    r = lax.broadcasted_iota(jnp.int32, (c, c), 0)
    s = lax.broadcasted_iota(jnp.int32, (c, c), 1)
    return r, s


TRI_BASE = 16


def _block_masks(r, s):
    same = lambda n: (r // n) == (s // n)
    masks = [same(TRI_BASE).astype(F32)]
    n = TRI_BASE
    while n < CHUNK:
        n *= 2
        masks.append(jnp.where(same(n), 1.0, 0.0) - jnp.where(same(n // 2), 1.0, 0.0))
    return masks


def _unit_tri_inverses(a, eye, blocks):
    p = [-(a_n * blocks[0]) for a_n in a]
    x = [eye + p_n for p_n in p]
    for _ in range(int(math.log2(TRI_BASE)) - 1):
        p = [_dot(p_n, p_n) for p_n in p]
        x = [x_n + _dot(x_n, p_n) for x_n, p_n in zip(x, p)]
    for m in blocks[1:]:
        e = [_dot(a_n * m, x_n) for a_n, x_n in zip(a, x)]
        x = [x_n - _dot(x_n, e_n) for x_n, e_n in zip(x, e)]
    return x


def _dn_chunks(q, k, v, gc, beta, crow, total, state, incl, strict, eye, blocks):
    n = range(len(q))
    c = q[0].shape[0]
    decay = [jnp.where(incl[i], jnp.exp(gc[i] - crow[i]), 0.0) for i in n]
    kb = [k[i] * beta[i] for i in n]
    a = [jnp.where(strict[i], _dot_nt(kb[i], k[i]) * decay[i], 0.0) for i in n]
    t = _unit_tri_inverses(a, eye, blocks)
    egc = [jnp.exp(gc[i]) for i in n]
    uw = [_dot(t[i], jnp.concatenate([v[i] * beta[i], kb[i] * egc[i]], axis=1)) for i in n]
    ws_qs = [_dot(jnp.concatenate([uw[i][:, DN_DV:], q[i] * egc[i]], axis=0), state[i]) for i in n]
    v_new = [uw[i][:, 0:DN_DV] - ws_qs[i][0:c] for i in n]
    attn = [_dot_nt(q[i], k[i]) * decay[i] for i in n]
    o = [ws_qs[i][c:2 * c] + _dot(attn[i], v_new[i]) for i in n]
    state = [state[i] * jnp.exp(total[i]) + _dot_tn(k[i] * jnp.exp(total[i] - gc[i]), v_new[i]) for i in n]
    return o, state


def _dn_kernel(qf_ref, kf_ref, vf_ref, gf_ref, qb_ref, kb_ref, vb_ref, gb_ref, s0_ref,
               of_ref, ob_ref, sfin_ref, s_ref):
    i = pl.program_id(1)
    nck = gf_ref.shape[1]
    c = CHUNK
    nh = DN_HEADS

    @pl.when(i == 0)
    def _():
        s_ref[...] = s0_ref[0]

    r, s = _tri_masks(c)
    eye = (r == s).astype(F32)
    blocks = _block_masks(r, s)
    dirs = ((qf_ref, kf_ref, vf_ref, gf_ref, of_ref, r >= s, r > s, c - 1),
            (qb_ref, kb_ref, vb_ref, gb_ref, ob_ref, r <= s, r < s, 0))

    def body(ci, carry):
        q, k, v, gc, beta, crow, total, state, incl, strict, dest = ([] for _ in range(11))
        for d, (q_ref, k_ref, v_ref, g_ref, o_ref, inc, stri, tot_row) in enumerate(dirs):
            cc = ci if d == 0 else nck - 1 - ci
            rows = pl.ds(pl.multiple_of(cc * c, c), c)
            inclf = inc.astype(F32)
            grow = g_ref[0, cc, d]
            cum_rows = _dot_nt(grow, inclf)
            cols = _dot_nt(jnp.concatenate([inclf, eye], axis=0), grow)
            for h in range(nh):
                lanes = slice(h * LANE, (h + 1) * LANE)
                q.append(q_ref[0, rows, lanes])
                k.append(k_ref[0, rows, lanes])
                v.append(v_ref[0, rows, lanes])
                gc.append(cols[0:c, nh + h:nh + h + 1])
                beta.append(cols[c:2 * c, h:h + 1])
                crow.append(cum_rows[nh + h:nh + h + 1, :])
                total.append(cols[tot_row:tot_row + 1, nh + h:nh + h + 1])
                state.append(s_ref[d * nh + h])
                incl.append(inc)
                strict.append(stri)
                dest.append((o_ref, rows, lanes))
        o, state = _dn_chunks(q, k, v, gc, beta, crow, total, state, incl, strict, eye, blocks)
        for n, (o_ref, rows, lanes) in enumerate(dest):
            o_ref[0, rows, lanes] = o[n]
            s_ref[n] = state[n]
        return carry

    lax.fori_loop(0, nck, body, 0)

    @pl.when(i == pl.num_programs(1) - 1)
    def _():
        sfin_ref[0] = s_ref[...]


def _dn_scan(q, k, v, grows, state0):
    b, l, width = q.shape
    tl = min(DN_TILE, l)
    nb = l // tl
    nck = tl // CHUNK
    fwd = pl.BlockSpec((1, tl, width), lambda bi, i: (bi, i, 0))
    bwd = pl.BlockSpec((1, tl, width), lambda bi, i: (bi, nb - 1 - i, 0))
    gfwd = pl.BlockSpec((1, nck) + grows.shape[2:], lambda bi, i: (bi, i, 0, 0, 0))
    gbwd = pl.BlockSpec((1, nck) + grows.shape[2:], lambda bi, i: (bi, nb - 1 - i, 0, 0, 0))
    st = pl.BlockSpec((1,) + state0.shape[1:], lambda bi, i: (bi, 0, 0, 0))
    return pl.pallas_call(
        _dn_kernel,
        grid=(b, nb),
        in_specs=[fwd, fwd, fwd, gfwd, bwd, bwd, bwd, gbwd, st],
        out_specs=[fwd, bwd, st],
        out_shape=[jax.ShapeDtypeStruct(q.shape, F32), jax.ShapeDtypeStruct(q.shape, F32),
                   jax.ShapeDtypeStruct(state0.shape, F32)],
        scratch_shapes=[pltpu.VMEM(state0.shape[1:], F32)],
        compiler_params=_params("parallel", "arbitrary"),
        name="deltanet_scan",
    )(q, k, v, grows, q, k, v, grows, state0)


def _ssd_chunk(x, bm, cm, rows, state, expand, incl, inclf, eye, first_row):
    c = x.shape[0]
    nh = SSD_GRP
    p = SSD_HEAD_DIM
    cum_row = _dot_nt(rows[nh:2 * nh], inclf)
    cols = _dot_nt(jnp.concatenate([inclf, eye], axis=0), rows)
    cum = cols[0:c, nh:2 * nh]
    dt = cols[c:2 * c, 0:nh]
    total = cum[first_row:first_row + 1, :]
    xdt = x * _dot(dt, expand)
    cb = _dot_nt(cm, bm)
    from_state = _dot(cm, state) * _dot(jnp.exp(cum), expand)
    parts = []
    for h in range(nh):
        seg = jnp.where(incl, jnp.exp(cum[:, h:h + 1] - cum_row[h:h + 1, :]), 0.0)
        parts.append(_dot(cb * seg, xdt[:, h * p:(h + 1) * p]))
    y = jnp.concatenate(parts, axis=1) + from_state
    carry_in = xdt * _dot(jnp.exp(total - cum), expand)
    state = state * _dot(jnp.exp(total), expand) + _dot_tn(bm, carry_in)
    return y, state


def _ssd_kernel(xf_ref, bf_ref, cf_ref, gf_ref, xb_ref, bb_ref, cb_ref, gb_ref, s0_ref, e_ref,
                yf_ref, yb_ref, sfin_ref, s_ref):
    i = pl.program_id(2)
    nck = gf_ref.shape[2]
    c = CHUNK
    nh = SSD_GRP

    @pl.when(i == 0)
    def _():
        s_ref[...] = s0_ref[0, 0]

    r, s = _tri_masks(c)
    eye = (r == s).astype(F32)
    lower, upper = r >= s, r <= s
    lowerf, upperf = lower.astype(F32), upper.astype(F32)
    expand = e_ref[...]

    def body(ci, carry):
        cf = ci
        cb = nck - 1 - ci
        rf = pl.ds(pl.multiple_of(cf * c, c), c)
        rb = pl.ds(pl.multiple_of(cb * c, c), c)
        y, st = _ssd_chunk(xf_ref[0, rf, :], bf_ref[0, rf, :], cf_ref[0, rf, :], gf_ref[0, 0, cf][0:2 * nh],
                           s_ref[0], expand, lower, lowerf, eye, c - 1)
        yf_ref[0, rf, :] = y
        s_ref[0] = st
        y, st = _ssd_chunk(xb_ref[0, rb, :], bb_ref[0, rb, :], cb_ref[0, rb, :], gb_ref[0, 0, cb][2 * nh:4 * nh],
                           s_ref[1], expand, upper, upperf, eye, 0)
        yb_ref[0, rb, :] = y
        s_ref[1] = st
        return carry

    lax.fori_loop(0, nck, body, 0)

    @pl.when(i == pl.num_programs(2) - 1)
    def _():
        sfin_ref[0, 0] = s_ref[...]


def _ssd_scan(xs, bc, grows, state0, expand):
    b, l, _ = xs.shape
    tl = min(SCAN_TILE, l)
    nb = l // tl
    nck = tl // CHUNK
    gwid = SSD_GRP * SSD_HEAD_DIM
    ng = SSD_GROUPS
    xf = pl.BlockSpec((1, tl, gwid), lambda bi, g, i: (bi, i, g))
    xb = pl.BlockSpec((1, tl, gwid), lambda bi, g, i: (bi, nb - 1 - i, g))
    bf = pl.BlockSpec((1, tl, SSD_STATE), lambda bi, g, i: (bi, i, g))
    cf = pl.BlockSpec((1, tl, SSD_STATE), lambda bi, g, i: (bi, i, ng + g))
    bb = pl.BlockSpec((1, tl, SSD_STATE), lambda bi, g, i: (bi, nb - 1 - i, g))
    cb = pl.BlockSpec((1, tl, SSD_STATE), lambda bi, g, i: (bi, nb - 1 - i, ng + g))
    gf = pl.BlockSpec((1, 1, nck, 4 * SSD_GRP, CHUNK), lambda bi, g, i: (bi, g, i, 0, 0))
    gb = pl.BlockSpec((1, 1, nck, 4 * SSD_GRP, CHUNK), lambda bi, g, i: (bi, g, nb - 1 - i, 0, 0))
    st = pl.BlockSpec((1, 1, 2, SSD_STATE, gwid), lambda bi, g, i: (bi, g, 0, 0, 0))
    return pl.pallas_call(
        _ssd_kernel,
        grid=(b, ng, nb),
        in_specs=[xf, bf, cf, gf, xb, bb, cb, gb, st, _resident(expand.shape)],
        out_specs=[xf, xb, st],
        out_shape=[jax.ShapeDtypeStruct(xs.shape, F32), jax.ShapeDtypeStruct(xs.shape, F32),
                   jax.ShapeDtypeStruct(state0.shape, F32)],
        scratch_shapes=[pltpu.VMEM((2, SSD_STATE, gwid), F32)],
        compiler_params=_params("parallel", "parallel", "arbitrary"),
        name="ssd_scan",
    )(xs, bc, bc, grows, xs, bc, bc, grows, state0, expand)


def _evmerge_kernel(of_ref, ob_ref, yf_ref, yb_ref, xs_ref, z_ref, dng_ref, sd_ref, sg_ref, a_ref, b_ref):
    for s in range(0, DN_VW, LANE):
        o = of_ref[0, :, s:s + LANE] + ob_ref[0, :, s:s + LANE]
        o = o * lax.rsqrt(jnp.mean(o * o, axis=-1, keepdims=True) + EPS) * dng_ref[...]
        a_ref[0, :, s:s + LANE] = (o * _silu(z_ref[0, :, s:s + LANE])).astype(BF16)
    gwid = SSD_INNER // SSD_GROUPS
    for s in range(0, SSD_INNER, gwid):
        y = yf_ref[0, :, s:s + gwid] + yb_ref[0, :, s:s + gwid] + sd_ref[:, s:s + gwid] * xs_ref[0, :, s:s + gwid]
        y = y * _silu(z_ref[0, :, DN_VW + s:DN_VW + s + gwid])
        y = y * lax.rsqrt(jnp.mean(y * y, axis=-1, keepdims=True) + EPS) * sg_ref[:, s:s + gwid]
        b_ref[0, :, s:s + gwid] = y.astype(BF16)


def _even_merge(o_f, o_b, y_f, y_b, xs, z, dn_gain, ssd_d, ssd_gain):
    b, l, _ = o_f.shape
    tm = min(ROW_TILE, l)
    row = lambda c: pl.BlockSpec((1, tm, c), lambda bi, i: (bi, i, 0))
    return pl.pallas_call(
        _evmerge_kernel,
        grid=(b, l // tm),
        in_specs=[row(DN_VW), row(DN_VW), row(SSD_INNER), row(SSD_INNER), row(SSD_INNER), row(EV_GATE),
                  _resident(dn_gain.shape), _resident(ssd_d.shape), _resident(ssd_gain.shape)],
        out_specs=[row(DN_VW), row(SSD_INNER)],
        out_shape=[jax.ShapeDtypeStruct((b, l, DN_VW), BF16), jax.ShapeDtypeStruct((b, l, SSD_INNER), BF16)],
        compiler_params=_params("parallel", "parallel"),
        name="even_merge",
    )(o_f, o_b, y_f, y_b, xs, z, dn_gain, ssd_d, ssd_gain)


def _mixffn_kernel(x_ref, a_ref, b_ref, mod_ref, gain_ref, wo_ref, wi_ref, wf_ref, o_ref, act_ref):
    d = x_ref.shape[2]
    na = a_ref.shape[2]
    hid = wf_ref.shape[0]
    g1 = mod_ref[0, :, 2 * d:3 * d]
    shift = mod_ref[0, :, 3 * d:4 * d]
    scale = mod_ref[0, :, 4 * d:5 * d]
    g2 = mod_ref[0, :, 5 * d:6 * d]
    mixed = _dot(a_ref[0], wo_ref[0:na, :]) + _dot(b_ref[0], wo_ref[na:, :])
    x1 = x_ref[0] + g1 * mixed
    h = _norm_mod(x1, gain_ref[...], shift, scale).astype(BF16)
    half = hid // 2
    for s in range(0, hid, half):
        gate = _dot(h, wi_ref[:, s:s + half])
        up = _dot(h, wi_ref[:, hid + s:hid + s + half])
        act_ref[:, s:s + half] = (_silu(gate) * up).astype(BF16)
    o_ref[0] = x1 + g2 * _dot(act_ref[...], wf_ref[...])


def _mix_ffn(x, a, bmix, modv, gain, w_out, w_in, w_ffn_out):
    b, l, d = x.shape
    tm = min(ROW_TILE, l)
    hid = w_ffn_out.shape[0]
    row = lambda c: pl.BlockSpec((1, tm, c), lambda bi, i: (bi, i, 0))
    return pl.pallas_call(
        _mixffn_kernel,
        grid=(b, l // tm),
        in_specs=[row(d), row(a.shape[2]), row(bmix.shape[2]),
                  pl.BlockSpec((1, 1, modv.shape[2]), lambda bi, i: (bi, 0, 0)),
                  _resident(gain.shape), _resident(w_out.shape), _resident(w_in.shape), _resident(w_ffn_out.shape)],
        out_specs=row(d),
        out_shape=jax.ShapeDtypeStruct(x.shape, F32),
        scratch_shapes=[pltpu.VMEM((tm, hid), BF16)],
        compiler_params=_params("parallel", "parallel"),
        name="mix_ffn",
    )(x, a, bmix, modv, gain, w_out, w_in, w_ffn_out)


def _rope(x, cos, sin):
    lane = lax.broadcasted_iota(jnp.int32, x.shape, 1)
    partner = jnp.where(lane % 32 < 16, pltpu.roll(x, LANE - 16, 1), pltpu.roll(x, 16, 1))
    return x * cos + partner * sin


def _odin_kernel(x_ref, mod_ref, gain_ref, w_ref, qan_ref, wuq_ref, kvan_ref, wukv_ref, dqn_ref, dkn_ref,
                 mqn_ref, mkn_ref, cos_ref, sin_ref, *out_refs, with_q, with_rope):
    if with_q:
        daq_ref, dak_ref, dav_ref, mq_ref, mk_ref, mv_ref = out_refs
    else:
        dak_ref, dav_ref, mk_ref, mv_ref = out_refs
    d = x_ref.shape[2]
    h = _norm_mod(x_ref[0], gain_ref[...], mod_ref[0, :, 0:d], mod_ref[0, :, d:2 * d]).astype(BF16)
    if with_rope:
        cos = cos_ref[...]
        sin = sin_ref[...]
    lane = lax.broadcasted_iota(jnp.int32, (h.shape[0], LANE), 1)
    low = lane < DA_DIM

    def pair_norm(u, g):
        sq = u * u
        lo = jnp.sum(jnp.where(low, sq, 0.0), axis=-1, keepdims=True)
        hi = jnp.sum(sq, axis=-1, keepdims=True) - lo
        ms = jnp.where(low, lo, hi) * (1.0 / DA_DIM)
        return u * lax.rsqrt(ms + EPS) * g

    def da_part(out_ref, base, g, out_scale):
        p = _dot(h, w_ref[:, base:base + DA_QK])
        for s in range(0, DA_QK, LANE):
            u = pair_norm(p[:, s:s + LANE], g)
            if with_rope:
                u = _rope(u, cos, sin)
            out_ref[0, :, s:s + LANE] = (u * out_scale).astype(BF16)

    if with_q:
        da_part(daq_ref, 0, dqn_ref[...], DA_DIM ** -0.5 * LOG2E)
    da_part(dak_ref, DA_QK, dkn_ref[...], 1.0)
    dav_ref[0] = _dot(h, w_ref[:, 2 * DA_QK:2 * DA_QK + DA_VW]).astype(BF16)

    def low_norm(u, g):
        ms = jnp.sum(u * u, axis=-1, keepdims=True) * (1.0 / MLA_ROPE)
        return u * lax.rsqrt(ms + EPS) * g

    def full_norm(u, g):
        return u * lax.rsqrt(jnp.mean(u * u, axis=-1, keepdims=True) + EPS) * g

    c0 = 2 * DA_QK + DA_VW
    if with_q:
        cq = full_norm(_dot(h, w_ref[:, c0:c0 + MLA_Q_RANK]), qan_ref[...]).astype(BF16)
        qf = _dot(cq, wuq_ref[...])
        scale = (MLA_NOPE + MLA_ROPE) ** -0.5 * LOG2E
        for hd in range(MLA_HEADS):
            s = hd * MLA_QK_PAD
            mq_ref[0, :, s:s + LANE] = (full_norm(qf[:, s:s + LANE], mqn_ref[:, 0:LANE]) * scale).astype(BF16)
            u = low_norm(qf[:, s + LANE:s + 2 * LANE], mqn_ref[:, LANE:2 * LANE])
            if with_rope:
                u = _rope(u, cos, sin)
            mq_ref[0, :, s + LANE:s + 2 * LANE] = (u * scale).astype(BF16)
    c0 += MLA_Q_RANK
    ckv = full_norm(_dot(h, w_ref[:, c0:c0 + MLA_KV_RANK]), kvan_ref[...]).astype(BF16)
    kvu = _dot(ckv, wukv_ref[...])
    c0 += MLA_KV_RANK
    kr = low_norm(_dot(h, w_ref[:, c0:c0 + LANE]), mkn_ref[:, LANE:2 * LANE])
    if with_rope:
        kr = _rope(kr, cos, sin)
    kr = kr.astype(BF16)
    for hd in range(MLA_HEADS):
        s = hd * MLA_QK_PAD
        mk_ref[0, :, s:s + LANE] = full_norm(kvu[:, hd * LANE:(hd + 1) * LANE], mkn_ref[:, 0:LANE]).astype(BF16)
        mk_ref[0, :, s + LANE:s + 2 * LANE] = kr
    mv_ref[0] = kvu[:, MLA_HEADS * MLA_NOPE:].astype(BF16)


def _odd_in(x, modv, gain, w_in, q_a_gain, w_uq, kv_a_gain, w_ukv, dq_gain, dk_gain, mq_gain, mk_gain,
            cos_t, sin_t, with_q, with_rope):
    b, l, d = x.shape
    tm = min(ROW_TILE, l)
    row = lambda c: pl.BlockSpec((1, tm, c), lambda bi, i: (bi, i, 0))
    widths = (DA_QK, DA_VW, MLA_HEADS * MLA_QK_PAD, MLA_HEADS * MLA_V)
    if with_q:
        widths = (DA_QK,) + widths[:2] + (MLA_HEADS * MLA_QK_PAD,) + widths[2:]
    table = pl.BlockSpec((tm, LANE), lambda bi, i: (i, 0))
    res = [gain, w_in, q_a_gain, w_uq, kv_a_gain, w_ukv, dq_gain, dk_gain, mq_gain, mk_gain]
    return pl.pallas_call(
        functools.partial(_odin_kernel, with_q=with_q, with_rope=with_rope),
        grid=(b, l // tm),
        in_specs=[row(d), pl.BlockSpec((1, 1, modv.shape[2]), lambda bi, i: (bi, 0, 0))]
        + [_resident(a.shape) for a in res] + [table, table],
        out_specs=[row(c) for c in widths],
        out_shape=[jax.ShapeDtypeStruct((b, l, c), BF16) for c in widths],
        compiler_params=_params("parallel", "parallel"),
        name="odd_in_q" if with_q else "odd_in_kv",
    )(x, modv, *res, cos_t, sin_t)


def _flash_scratch(nchain, strip, tk, qk_width):
    half = [pltpu.VMEM((nchain, tk, strip), F32), pltpu.VMEM((nchain, 1, strip), F32)]
    return [pltpu.VMEM((nchain, 1, strip), F32), pltpu.VMEM((nchain, V_ROWS, strip), F32),
            pltpu.VMEM((nchain, strip, qk_width), BF16)] + half + half


def _flash_loop(q_ref, k_ref, vt_ref, scratch, strip, split_q):
    m_ref, acc_ref, qz_ref, s0_ref, mx0_ref, s1_ref, mx1_ref = scratch
    s_ref, mx_ref = (s0_ref, s1_ref), (mx0_ref, mx1_ref)
    tq = q_ref.shape[1]
    tk = vt_ref.shape[4]
    nk = vt_ref.shape[2]
    nchain = m_ref.shape[0]
    m_ref[...] = jnp.full(m_ref.shape, -jnp.inf, F32)
    acc_ref[...] = jnp.zeros(acc_ref.shape, F32)
    lane = lax.broadcasted_iota(jnp.int32, (strip, q_ref.shape[2]), 1)
    for st in range(tq // strip):
        q = q_ref[0, pl.ds(st * strip, strip), :]
        if split_q:
            zero = jnp.zeros_like(q)
            qz_ref[2 * st] = jnp.where(lane < DA_DIM, q, zero)
            qz_ref[2 * st + 1] = jnp.where(lane < DA_DIM, zero, q)
        else:
            qz_ref[st] = q

    def scores(j, slot):
        k = k_ref[0, pl.ds(pl.multiple_of(j * tk, tk), tk), :]
        for c in range(nchain):
            st = _dot_nt(k, qz_ref[c])
            s_ref[slot][c] = st
            mx_ref[slot][c] = jnp.max(st, axis=0, keepdims=True)

    def weigh(j, slot):
        vt1 = vt_ref[0, 0, j]
        alphas, pvs = [], []
        for c in range(nchain):
            m_old = m_ref[c]
            m_new = jnp.maximum(m_old, mx_ref[slot][c])
            p = jnp.exp2(s_ref[slot][c] - m_new).astype(BF16)
            alphas.append(jnp.exp2(m_old - m_new))
            m_ref[c] = m_new
            pvs.append(_dot(vt1, p))
        return alphas, pvs

    def accumulate(pending):
        for c, (alpha, pv) in enumerate(zip(*pending)):
            acc_ref[c] = alpha * acc_ref[c] + pv

    unroll = 4
    scores(0, 0)

    def body(i, carry):
        t = unroll * i
        pending = None
        for u in range(unroll):
            scores(jnp.minimum(t + u + 1, nk - 1), (u + 1) % 2)
            if pending is not None:
                accumulate(pending)
            pending = weigh(t + u, u % 2)
        accumulate(pending)
        return carry

    lax.fori_loop(0, nk // unroll, body, 0)
    first = nk - nk % unroll
    for j in range(first, nk):
        if j + 1 < nk:
            scores(j + 1, (j - first + 1) % 2)
        accumulate(weigh(j, (j - first) % 2))


def _softmax_out(acc):
    return acc[0:MLA_V] / acc[MLA_V:MLA_V + 1]


def _da_kernel(q_ref, k_ref, vt_ref, lam_ref, gain_ref, o_ref, *scratch, strip, lambda_init):
    _flash_loop(q_ref, k_ref, vt_ref, scratch, strip, True)
    acc_ref = scratch[1]
    lp = lam_ref[...]
    lam = (jnp.exp(jnp.sum(lp[0:1] * lp[1:2], axis=-1, keepdims=True))
           - jnp.exp(jnp.sum(lp[2:3] * lp[3:4], axis=-1, keepdims=True)) + lambda_init)
    for st in range(q_ref.shape[1] // strip):
        o = _softmax_out(acc_ref[2 * st]) - lam * _softmax_out(acc_ref[2 * st + 1])
        o = o * lax.rsqrt(jnp.mean(o * o, axis=0, keepdims=True) + EPS) * gain_ref[...]
        o_ref[0, pl.ds(st * strip, strip), :] = (o * (1.0 - lambda_init)).T.astype(BF16)


def _values_t(v, heads, tk):
    b, lk, _ = v.shape
    vt = jnp.transpose(v.reshape(b, lk // tk, tk, heads, MLA_V), (0, 3, 1, 4, 2))
    return jnp.concatenate([vt, jnp.ones((b, heads, lk // tk, V_ROWS - MLA_V, tk), v.dtype)], axis=3)


def _diff_attention(q, k, v, lam_p, gain_col, lambda_init):
    b, lq, _ = q.shape
    lk = k.shape[1]
    tq = min(DA_TQ, lq)
    strip = min(ATT_STRIP, tq)
    vt = _values_t(v, DA_HEADS, ATT_TK)
    qs = pl.BlockSpec((1, tq, LANE), lambda bi, h, i: (bi, i, h))
    ks = pl.BlockSpec((1, lk, LANE), lambda bi, h, i: (bi, 0, h))
    vs = pl.BlockSpec((1, 1) + vt.shape[2:], lambda bi, h, i: (bi, h, 0, 0, 0))
    nchain = 2 * (tq // strip)
    return pl.pallas_call(
        functools.partial(_da_kernel, strip=strip, lambda_init=lambda_init),
        grid=(b, DA_HEADS, lq // tq),
        in_specs=[qs, ks, vs, _resident(lam_p.shape), _resident(gain_col.shape)],
        out_specs=qs,
        out_shape=jax.ShapeDtypeStruct(q.shape, BF16),
        scratch_shapes=_flash_scratch(nchain, strip, ATT_TK, LANE),
        compiler_params=_params("parallel", "parallel", "parallel"),
        name="diff_attention",
    )(q, k, vt, lam_p, gain_col)


def _mla_kernel(q_ref, k_ref, vt_ref, o_ref, *scratch, strip):
    _flash_loop(q_ref, k_ref, vt_ref, scratch, strip, False)
    acc_ref = scratch[1]
    for st in range(q_ref.shape[1] // strip):
        o_ref[0, pl.ds(st * strip, strip), :] = _softmax_out(acc_ref[st]).T.astype(BF16)


def _mla_attention(q, k, v):
    b, lq, _ = q.shape
    lk = k.shape[1]
    tq = min(MLA_TQ, lq)
    strip = min(ATT_STRIP, tq)
    vt = _values_t(v, MLA_HEADS, ATT_TK)
    qs = pl.BlockSpec((1, tq, MLA_QK_PAD), lambda bi, h, i: (bi, i, h))
    ks = pl.BlockSpec((1, lk, MLA_QK_PAD), lambda bi, h, i: (bi, 0, h))
    vs = pl.BlockSpec((1, 1) + vt.shape[2:], lambda bi, h, i: (bi, h, 0, 0, 0))
    nchain = tq // strip
    return pl.pallas_call(
        functools.partial(_mla_kernel, strip=strip),
        grid=(b, MLA_HEADS, lq // tq),
        in_specs=[qs, ks, vs],
        out_specs=pl.BlockSpec((1, tq, MLA_V), lambda bi, h, i: (bi, i, h)),
        out_shape=jax.ShapeDtypeStruct((b, lq, MLA_HEADS * MLA_V), BF16),
        scratch_shapes=_flash_scratch(nchain, strip, ATT_TK, MLA_QK_PAD),
        compiler_params=_params("parallel", "parallel", "parallel"),
        name="mla_attention",
    )(q, k, vt)


def _scan_rows(gates_t, index, chunks):
    b, _, l = gates_t.shape
    idx = jnp.asarray(index, jnp.int32)
    rows = gates_t[:, idx, :]
    rows = rows.reshape(b, idx.shape[0], idx.shape[1], chunks, l // chunks)
    return jnp.swapaxes(rows, 2, 3)


def _even_layer(x, ctx, mod_x, mod_c, p):
    b = x.shape[0]
    nh, ns = DN_HEADS, SSD_HEADS
    dn_index = [[d * nh + h for h in range(nh)] + [2 * nh + d * nh + h for h in range(nh)] for d in range(2)]
    dt0, a0 = 4 * nh, 4 * nh + 2 * ns
    ssd_index = [[dt0 + g * SSD_GRP + r for r in range(SSD_GRP)] + [a0 + g * SSD_GRP + r for r in range(SSD_GRP)]
                 + [dt0 + ns + g * SSD_GRP + r for r in range(SSD_GRP)] + [a0 + ns + g * SSD_GRP + r for r in range(SSD_GRP)]
                 for g in range(SSD_GROUPS)]

    def prepare(t, modv):
        q, k, v, xs, bc, z, gates = _even_in(t, modv, p["gain_mix"], p["w_main"], p["w_small"], p["conv_w"], p["conv_b"],
                                             p["gate_bias"], p["gate_scale"])
        gt = jnp.swapaxes(gates, 1, 2)
        nchunk = t.shape[1] // CHUNK
        return dict(q=q, k=k, v=v, xs=xs, bc=bc, z=z, dn_rows=jnp.swapaxes(_scan_rows(gt, dn_index, nchunk), 1, 2),
                    ssd_rows=_scan_rows(gt, ssd_index, nchunk))

    pc = prepare(ctx, mod_c)
    pL = prepare(x, mod_x)
    dn0 = jnp.zeros((b, 2 * nh, DN_DK, DN_DV), F32)
    ssd0 = jnp.zeros((b, SSD_GROUPS, 2, SSD_STATE, SSD_GRP * SSD_HEAD_DIM), F32)
    ocf, ocb, dn1 = _dn_scan(pc["q"], pc["k"], pc["v"], pc["dn_rows"], dn0)
    olf, olb, _ = _dn_scan(pL["q"], pL["k"], pL["v"], pL["dn_rows"], dn1)
    ycf, ycb, ssd1 = _ssd_scan(pc["xs"], pc["bc"], pc["ssd_rows"], ssd0, p["expand"])
    ylf, ylb, _ = _ssd_scan(pL["xs"], pL["bc"], pL["ssd_rows"], ssd1, p["expand"])

    def finish(t, modv, pp, of, ob, yf, yb):
        a, bm = _even_merge(of, ob, yf, yb, pp["xs"], pp["z"], p["dn_gain"], p["ssd_d"], p["ssd_gain"])
        return _mix_ffn(t, a, bm, modv, p["gain_ffn"], p["w_out"], p["ffn_w_in"], p["ffn_w_out"])

    return finish(x, mod_x, pL, olf, olb, ylf, ylb), finish(ctx, mod_c, pc, ocf, ocb, ycf, ycb)


def _rope_tables(n_tokens):
    lane = jnp.arange(LANE)
    quarter = (lane % 64) // 16
    n_freq = DA_DIM // 4
    inv_freq = ROPE_THETA ** (-(lane % 16).astype(F32) / n_freq)
    tok = jnp.arange(n_tokens)
    pos = jnp.where(quarter[None, :] < 2, (tok // GRID_W)[:, None], (tok % GRID_W)[:, None]).astype(F32)
    ang = pos * inv_freq[None, :]
    sign = jnp.where(quarter % 2 == 0, -1.0, 1.0).astype(F32)
    return jnp.cos(ang), jnp.sin(ang) * sign[None, :]


def _odd_layer(x, ctx, mod_x, mod_c, p, lambda_init):
    cos_t, sin_t = _rope_tables(x.shape[1])
    args = (p["gain_mix"], p["w_in"], p["q_a_gain"], p["w_uq"], p["kv_a_gain"], p["w_ukv"], p["dq_gain"], p["dk_gain"],
            p["mq_gain"], p["mk_gain"])
    daq, dak, dav, mq, mk, mv = _odd_in(x, mod_x, *args, cos_t, sin_t, True, True)
    cak, cav, cmk, cmv = _odd_in(ctx, mod_c, *args, cos_t[:ctx.shape[1]], sin_t[:ctx.shape[1]], False, False)
    cat = lambda a, c: jnp.concatenate([a, c], axis=1)
    da = _diff_attention(daq, cat(dak, cak), cat(dav, cav), p["da_lambda"], p["sub_gain"], lambda_init)
    ml = _mla_attention(mq, cat(mk, cmk), cat(mv, cmv))
    return _mix_ffn(x, da, ml, mod_x, p["gain_ffn"], p["w_out"], p["ffn_w_in"], p["ffn_w_out"])


def _even_params(i, j, norm_mix, norm_ffn, ffn_w_in, ffn_w_out, ev_w_in, ev_conv_w, ev_conv_b, dn_a_log, dn_dt_bias,
                 dn_norm, ssd_a_log, ssd_dt_bias, ssd_d, ssd_norm, ev_w_out):
    d = norm_mix.shape[1]
    w = ev_w_in[j]
    small = w[:, EV_CONV + EV_GATE:]
    nh, ns = DN_HEADS, SSD_HEADS
    w_small = jnp.concatenate([small, small[:, 4 * nh:], jnp.zeros((d, LANE - 4 * nh - 4 * ns), F32)], axis=1)
    zeros = lambda n: jnp.zeros((n,), F32)
    gate_bias = jnp.concatenate([zeros(2 * nh), dn_dt_bias[j].reshape(-1), ssd_dt_bias[j].reshape(-1),
                                 ssd_dt_bias[j].reshape(-1), zeros(LANE - 4 * nh - 4 * ns)])
    gate_scale = jnp.concatenate([jnp.ones((2 * nh,), F32), -jnp.exp(dn_a_log[j].reshape(-1)), jnp.ones((2 * ns,), F32),
                                  -jnp.exp(ssd_a_log[j].reshape(-1)), zeros(LANE - 4 * nh - 4 * ns)])
    expand = jnp.repeat(jnp.eye(SSD_GRP, dtype=F32), SSD_HEAD_DIM, axis=1)
    return dict(
        gain_mix=norm_mix[i].reshape(1, d), gain_ffn=norm_ffn[i].reshape(1, d),
        w_main=w[:, :EV_CONV + EV_GATE].astype(BF16), w_small=w_small,
        conv_w=ev_conv_w[j], conv_b=ev_conv_b[j].reshape(1, -1),
        gate_bias=gate_bias.reshape(1, LANE), gate_scale=gate_scale.reshape(1, LANE), expand=expand,
        dn_gain=dn_norm[j].reshape(1, DN_DV), ssd_d=jnp.repeat(ssd_d[j], SSD_HEAD_DIM).reshape(1, SSD_INNER),
        ssd_gain=ssd_norm[j].reshape(1, SSD_INNER), w_out=ev_w_out[j].astype(BF16),
        ffn_w_in=ffn_w_in[i].astype(BF16), ffn_w_out=ffn_w_out[i].astype(BF16))


def _odd_params(i, j, norm_mix, norm_ffn, ffn_w_in, ffn_w_out, od_w_in, da_q_norm, da_k_norm, da_lambda, da_sub_norm,
                mla_q_a_norm, mla_w_uq, mla_kv_a_norm, mla_w_ukv, mla_q_norm, mla_k_norm, od_w_out):
    d = norm_mix.shape[1]
    w = od_w_in[j]
    w_in = jnp.concatenate([w, jnp.zeros((d, LANE - MLA_ROPE), F32)], axis=1).astype(BF16)
    hq = mla_w_uq[j].reshape(MLA_Q_RANK, MLA_HEADS, MLA_NOPE + MLA_ROPE)
    hq = jnp.pad(hq, ((0, 0), (0, 0), (0, MLA_QK_PAD - MLA_NOPE - MLA_ROPE)))
    hkv = mla_w_ukv[j].reshape(MLA_KV_RANK, MLA_HEADS, MLA_NOPE + MLA_V)
    w_ukv = jnp.concatenate([hkv[:, :, :MLA_NOPE].reshape(MLA_KV_RANK, -1), hkv[:, :, MLA_NOPE:].reshape(MLA_KV_RANK, -1)], axis=1)
    pad_gain = lambda g: jnp.pad(g, (0, MLA_QK_PAD - MLA_NOPE - MLA_ROPE)).reshape(1, MLA_QK_PAD)
    return dict(
        gain_mix=norm_mix[i].reshape(1, d), gain_ffn=norm_ffn[i].reshape(1, d), w_in=w_in,
        q_a_gain=mla_q_a_norm[j].reshape(1, -1), w_uq=hq.reshape(MLA_Q_RANK, -1).astype(BF16),
        kv_a_gain=mla_kv_a_norm[j].reshape(1, -1), w_ukv=w_ukv.astype(BF16),
        dq_gain=jnp.tile(da_q_norm[j], 2).reshape(1, LANE), dk_gain=jnp.tile(da_k_norm[j], 2).reshape(1, LANE),
        mq_gain=pad_gain(mla_q_norm[j]), mk_gain=pad_gain(mla_k_norm[j]),
        da_lambda=jnp.pad(da_lambda[j], ((0, 4), (0, LANE - DA_DIM))), sub_gain=da_sub_norm[j].reshape(2 * DA_DIM, 1),
        w_out=od_w_out[j].astype(BF16), ffn_w_in=ffn_w_in[i].astype(BF16), ffn_w_out=ffn_w_out[i].astype(BF16))


def kernel(x, c, ctx, c_ctx, ada_w, ada_b, norm_mix, norm_ffn, ffn_w_in, ffn_w_out, ev_w_in, ev_conv_w, ev_conv_b, dn_a_log, dn_dt_bias, dn_norm, ssd_a_log, ssd_dt_bias, ssd_d, ssd_norm, ev_w_out, od_w_in, da_q_norm, da_k_norm, da_lambda, da_sub_norm, mla_q_a_norm, mla_w_uq, mla_kv_a_norm, mla_w_ukv, mla_q_norm, mla_k_norm, od_w_out):
    b, _, d = x.shape
    depth = ada_w.shape[0]
    assert b < 8
    cvec = jnp.concatenate([c, c_ctx[None, :], jnp.zeros((8 - b - 1, d), F32)], axis=0)
    mod = _modulation(cvec, ada_w, ada_b)
    for i in range(depth):
        last = i == depth - 1
        j = i // 2
        mod_x = mod[i, :b].reshape(b, 1, 6 * d)
        mod_c = jnp.broadcast_to(mod[i, b].reshape(1, 1, 6 * d), (b, 1, 6 * d))
        if i % 2 == 0:
            p = _even_params(i, j, norm_mix, norm_ffn, ffn_w_in, ffn_w_out, ev_w_in, ev_conv_w, ev_conv_b, dn_a_log,
                             dn_dt_bias, dn_norm, ssd_a_log, ssd_dt_bias, ssd_d, ssd_norm, ev_w_out)
            x, ctx_new = _even_layer(x, ctx, mod_x, mod_c, p)
        else:
            p = _odd_params(i, j, norm_mix, norm_ffn, ffn_w_in, ffn_w_out, od_w_in, da_q_norm, da_k_norm, da_lambda,
                            da_sub_norm, mla_q_a_norm, mla_w_uq, mla_kv_a_norm, mla_w_ukv, mla_q_norm, mla_k_norm, od_w_out)
            lambda_init = 0.8 - 0.6 * math.exp(-0.3 * i)
            if last:
                x = _odd_layer(x, ctx, mod_x, mod_c, p, lambda_init)
                ctx_new = ctx
            else:
                raise NotImplementedError("context update after an attention layer is not needed for depth 2")
        ctx = ctx_new
    return x
```

```python
import functools
import math

import jax
import jax.numpy as jnp
from jax import lax
from jax.experimental import pallas as pl
from jax.experimental.pallas import tpu as pltpu

F32 = jnp.float32
BF16 = jnp.bfloat16

GRID_W = 64
EPS = 1e-6
ROPE_THETA = 10000.0
CONV_K = 5
DN_HEADS = 8
DN_DK = 128
DN_DV = 128
SSD_HEADS = 16
SSD_HEAD_DIM = 64
SSD_GROUPS = 2
SSD_STATE = 128
SSD_GRP = SSD_HEADS // SSD_GROUPS
SSD_INNER = SSD_HEADS * SSD_HEAD_DIM
DA_HEADS = 8
DA_DIM = 64
MLA_HEADS = 8
MLA_Q_RANK = 512
MLA_KV_RANK = 256
MLA_NOPE = 128
MLA_ROPE = 64
MLA_V = 128
MLA_QK_PAD = 256

DN_QK = DN_HEADS * DN_DK
DN_VW = DN_HEADS * DN_DV
SSD_BC = SSD_GROUPS * SSD_STATE
EV_CONV = 2 * DN_QK + DN_VW + SSD_INNER + 2 * SSD_BC
EV_GATE = DN_VW + SSD_INNER
DA_QK = DA_HEADS * 2 * DA_DIM
DA_VW = DA_HEADS * 2 * DA_DIM

LANE = 128
BF16_ROWS = 16
CHUNK = 128
VMEM_LIMIT = 56 * 1024 * 1024

ROW_TILE = 512
ODD_TILE = 256
SCAN_TILE = 512
DN_TILE = 256
DA_TQ = 512
MLA_TQ = 1024
ATT_TK = 256
ATT_UNROLL = 8
ATT_STRIP = 256
V_ROWS = MLA_V + BF16_ROWS
LOG2E = 1.4426950408889634


def _dot(a, b):
    return jnp.dot(a, b, preferred_element_type=F32)


def _dot_nt(a, b):
    return lax.dot_general(a, b, (((1,), (1,)), ((), ())), preferred_element_type=F32)


def _dot_tn(a, b):
    return lax.dot_general(a, b, (((0,), (0,)), ((), ())), preferred_element_type=F32)


def _sigmoid(x):
    return 1.0 / (1.0 + jnp.exp(-x))


def _silu(x):
    return x * _sigmoid(x)


def _softplus(x):
    return jnp.maximum(x, 0.0) + jnp.log(1.0 + jnp.exp(-jnp.abs(x)))


def _params(*sem):
    return pltpu.CompilerParams(dimension_semantics=sem, vmem_limit_bytes=VMEM_LIMIT)


def _resident(shape):
    nd = len(shape)
    return pl.BlockSpec(shape, lambda *_: (0,) * nd, pipeline_mode=pl.Buffered(1))


def _norm_mod(x, gain, shift, scale):
    y = x * lax.rsqrt(jnp.mean(x * x, axis=-1, keepdims=True) + EPS)
    return (y * gain) * (1.0 + scale) + shift


def _mod_kernel(c_ref, w_ref, b_ref, o_ref):
    o_ref[0] = _dot(_silu(c_ref[...]), w_ref[0]) + b_ref[0]


def _modulation(cvec, ada_w, ada_b):
    depth, d, n = ada_w.shape
    tn = n // 4
    return pl.pallas_call(
        _mod_kernel,
        grid=(depth, n // tn),
        in_specs=[
            pl.BlockSpec((8, d), lambda l, j: (0, 0)),
            pl.BlockSpec((1, d, tn), lambda l, j: (l, 0, j)),
            pl.BlockSpec((1, 1, tn), lambda l, j: (l, 0, j)),
        ],
        out_specs=pl.BlockSpec((1, 8, tn), lambda l, j: (l, 0, j)),
        out_shape=jax.ShapeDtypeStruct((depth, 8, n), F32),
        compiler_params=_params("parallel", "parallel"),
        name="modulation",
    )(cvec, ada_w, ada_b.reshape(depth, 1, n))


def _evin_kernel(xp_ref, xc_ref, xn_ref, mod_ref, gain_ref, w_ref, wsm_ref, cw_ref, cb_ref, gb_ref, gs_ref,
                 q_ref, k_ref, v_ref, xs_ref, bc_ref, z_ref, g_ref, hext_ref, pext_ref):
    i = pl.program_id(1)
    nb = pl.num_programs(1)
    tm = xc_ref.shape[1]
    d = xc_ref.shape[2]
    halo = BF16_ROWS
    gain = gain_ref[...]
    shift = mod_ref[0, :, 0:d]
    scale = mod_ref[0, :, d:2 * d]

    hc = _norm_mod(xc_ref[0], gain, shift, scale)
    hp = _norm_mod(xp_ref[0], gain, shift, scale)
    hn = _norm_mod(xn_ref[0], gain, shift, scale)
    hext_ref[0:halo, :] = jnp.where(i > 0, hp, 0.0).astype(BF16)
    hext_ref[halo:halo + tm, :] = hc.astype(BF16)
    hext_ref[halo + tm:, :] = jnp.where(i < nb - 1, hn, 0.0).astype(BF16)

    gw = pext_ref.shape[1]
    pad = CONV_K // 2
    plan = ((q_ref, 0, DN_QK, DN_DK ** -0.5), (k_ref, DN_QK, DN_QK, 1.0), (v_ref, 2 * DN_QK, DN_VW, None),
            (xs_ref, 2 * DN_QK + DN_VW, SSD_INNER, None), (bc_ref, 2 * DN_QK + DN_VW + SSD_INNER, 2 * SSD_BC, None))
    for out_ref, base, width, l2_scale in plan:
        for off in range(0, width, gw):
            c0 = base + off
            pext_ref[...] = _dot(hext_ref[...], w_ref[:, c0:c0 + gw])
            acc = cb_ref[:, c0:c0 + gw] + cw_ref[0:1, c0:c0 + gw] * pext_ref[halo - pad:halo - pad + tm, :]
            for t in range(1, CONV_K):
                acc = acc + cw_ref[t:t + 1, c0:c0 + gw] * pext_ref[halo - pad + t:halo - pad + t + tm, :]
            u = _silu(acc)
            if l2_scale is None:
                out_ref[0, :, off:off + gw] = u
            else:
                for s in range(0, gw, LANE):
                    uh = u[:, s:s + LANE]
                    inv = lax.rsqrt(jnp.sum(uh * uh, axis=-1, keepdims=True) + EPS)
                    out_ref[0, :, off + s:off + s + LANE] = uh * (inv * l2_scale)

    hcb = hext_ref[halo:halo + tm, :]
    for off in range(0, EV_GATE, gw):
        z_ref[0, :, off:off + gw] = _dot(hcb, w_ref[:, EV_CONV + off:EV_CONV + off + gw])

    p = _dot(hc, wsm_ref[...]) + gb_ref[...]
    lane = lax.broadcasted_iota(jnp.int32, p.shape, 1)
    g_ref[0] = jnp.where(lane < 2 * DN_HEADS, _sigmoid(p), _softplus(p)) * gs_ref[...]


def _even_in(x, modv, gain, w_main, w_small, conv_w, conv_b, gate_bias, gate_scale):
    b, l, d = x.shape
    tm = min(ROW_TILE, l)
    nb = l // tm
    hb = tm // BF16_ROWS
    gw = 512
    row = lambda c: pl.BlockSpec((1, tm, c), lambda bi, i: (bi, i, 0))
    outs = (DN_QK, DN_QK, DN_VW, SSD_INNER, 2 * SSD_BC, EV_GATE, LANE)
    return pl.pallas_call(
        _evin_kernel,
        grid=(b, nb),
        in_specs=[
            pl.BlockSpec((1, BF16_ROWS, d), lambda bi, i: (bi, jnp.maximum(i * hb - 1, 0), 0)),
            row(d),
            pl.BlockSpec((1, BF16_ROWS, d), lambda bi, i: (bi, jnp.minimum((i + 1) * hb, l // BF16_ROWS - 1), 0)),
            pl.BlockSpec((1, 1, modv.shape[2]), lambda bi, i: (bi, 0, 0)),
            _resident(gain.shape), _resident(w_main.shape), _resident(w_small.shape), _resident(conv_w.shape),
            _resident(conv_b.shape), _resident(gate_bias.shape), _resident(gate_scale.shape),
        ],
        out_specs=[row(c) for c in outs],
        out_shape=[jax.ShapeDtypeStruct((b, l, c), F32) for c in outs],
        scratch_shapes=[pltpu.VMEM((tm + 2 * BF16_ROWS, d), BF16), pltpu.VMEM((tm + 2 * BF16_ROWS, gw), F32)],
        compiler_params=_params("parallel", "parallel"),
        name="even_in",
    )(x, x, x, modv, gain, w_main, w_small, conv_w, conv_b, gate_bias, gate_scale)


def _tri_masks(c):
    r = lax.broadcasted_iota(jnp.int32, (c, c), 0)
    s = lax.broadcasted_iota(jnp.int32, (c, c), 1)
    return r, s


TRI_BASE = 16


def _block_masks(r, s):
    same = lambda n: (r // n) == (s // n)
    masks = [same(TRI_BASE).astype(F32)]
    n = TRI_BASE
    while n < CHUNK:
        n *= 2
        masks.append(jnp.where(same(n), 1.0, 0.0) - jnp.where(same(n // 2), 1.0, 0.0))
    return masks


def _unit_tri_inverses(a, eye, blocks):
    p = [-(a_n * blocks[0]) for a_n in a]
    x = [eye + p_n for p_n in p]
    for _ in range(int(math.log2(TRI_BASE)) - 1):
        p = [_dot(p_n, p_n) for p_n in p]
        x = [x_n + _dot(x_n, p_n) for x_n, p_n in zip(x, p)]
    for m in blocks[1:]:
        e = [_dot(a_n * m, x_n) for a_n, x_n in zip(a, x)]
        x = [x_n - _dot(x_n, e_n) for x_n, e_n in zip(x, e)]
    return x


def _dn_chunks(q, k, v, gc, beta, crow, total, state, incl, strict, eye, blocks):
    n = range(len(q))
    c = q[0].shape[0]
    decay = [jnp.where(incl[i], jnp.exp(gc[i] - crow[i]), 0.0) for i in n]
    kb = [k[i] * beta[i] for i in n]
    a = [jnp.where(strict[i], _dot_nt(kb[i], k[i]) * decay[i], 0.0) for i in n]
    t = _unit_tri_inverses(a, eye, blocks)
    egc = [jnp.exp(gc[i]) for i in n]
    uw = [_dot(t[i], jnp.concatenate([v[i] * beta[i], kb[i] * egc[i]], axis=1)) for i in n]
    ws_qs = [_dot(jnp.concatenate([uw[i][:, DN_DV:], q[i] * egc[i]], axis=0), state[i]) for i in n]
    v_new = [uw[i][:, 0:DN_DV] - ws_qs[i][0:c] for i in n]
    attn = [_dot_nt(q[i], k[i]) * decay[i] for i in n]
    o = [ws_qs[i][c:2 * c] + _dot(attn[i], v_new[i]) for i in n]
    state = [state[i] * jnp.exp(total[i]) + _dot_tn(k[i] * jnp.exp(total[i] - gc[i]), v_new[i]) for i in n]
    return o, state


def _dn_kernel(qf_ref, kf_ref, vf_ref, gf_ref, qb_ref, kb_ref, vb_ref, gb_ref, s0_ref,
               of_ref, ob_ref, sfin_ref, s_ref):
    i = pl.program_id(1)
    nck = gf_ref.shape[1]
    c = CHUNK
    nh = DN_HEADS

    @pl.when(i == 0)
    def _():
        s_ref[...] = s0_ref[0]

    r, s = _tri_masks(c)
    eye = (r == s).astype(F32)
    blocks = _block_masks(r, s)
    dirs = ((qf_ref, kf_ref, vf_ref, gf_ref, of_ref, r >= s, r > s, c - 1),
            (qb_ref, kb_ref, vb_ref, gb_ref, ob_ref, r <= s, r < s, 0))

    def body(ci, carry):
        q, k, v, gc, beta, crow, total, state, incl, strict, dest = ([] for _ in range(11))
        for d, (q_ref, k_ref, v_ref, g_ref, o_ref, inc, stri, tot_row) in enumerate(dirs):
            cc = ci if d == 0 else nck - 1 - ci
            rows = pl.ds(pl.multiple_of(cc * c, c), c)
            inclf = inc.astype(F32)
            grow = g_ref[0, cc, d]
            cum_rows = _dot_nt(grow, inclf)
            cols = _dot_nt(jnp.concatenate([inclf, eye], axis=0), grow)
            for h in range(nh):
                lanes = slice(h * LANE, (h + 1) * LANE)
                q.append(q_ref[0, rows, lanes])
                k.append(k_ref[0, rows, lanes])
                v.append(v_ref[0, rows, lanes])
                gc.append(cols[0:c, nh + h:nh + h + 1])
                beta.append(cols[c:2 * c, h:h + 1])
                crow.append(cum_rows[nh + h:nh + h + 1, :])
                total.append(cols[tot_row:tot_row + 1, nh + h:nh + h + 1])
                state.append(s_ref[d * nh + h])
                incl.append(inc)
                strict.append(stri)
                dest.append((o_ref, rows, lanes))
        o, state = _dn_chunks(q, k, v, gc, beta, crow, total, state, incl, strict, eye, blocks)
        for n, (o_ref, rows, lanes) in enumerate(dest):
            o_ref[0, rows, lanes] = o[n]
            s_ref[n] = state[n]
        return carry

    lax.fori_loop(0, nck, body, 0)

    @pl.when(i == pl.num_programs(1) - 1)
    def _():
        sfin_ref[0] = s_ref[...]


def _dn_scan(q, k, v, grows, state0):
    b, l, width = q.shape
    tl = min(DN_TILE, l)
    nb = l // tl
    nck = tl // CHUNK
    fwd = pl.BlockSpec((1, tl, width), lambda bi, i: (bi, i, 0))
    bwd = pl.BlockSpec((1, tl, width), lambda bi, i: (bi, nb - 1 - i, 0))
    gfwd = pl.BlockSpec((1, nck) + grows.shape[2:], lambda bi, i: (bi, i, 0, 0, 0))
    gbwd = pl.BlockSpec((1, nck) + grows.shape[2:], lambda bi, i: (bi, nb - 1 - i, 0, 0, 0))
    st = pl.BlockSpec((1,) + state0.shape[1:], lambda bi, i: (bi, 0, 0, 0))
    return pl.pallas_call(
        _dn_kernel,
        grid=(b, nb),
        in_specs=[fwd, fwd, fwd, gfwd, bwd, bwd, bwd, gbwd, st],
        out_specs=[fwd, bwd, st],
        out_shape=[jax.ShapeDtypeStruct(q.shape, F32), jax.ShapeDtypeStruct(q.shape, F32),
                   jax.ShapeDtypeStruct(state0.shape, F32)],
        scratch_shapes=[pltpu.VMEM(state0.shape[1:], F32)],
        compiler_params=_params("parallel", "arbitrary"),
        name="deltanet_scan",
    )(q, k, v, grows, q, k, v, grows, state0)


def _ssd_chunk(x, bm, cm, rows, state, expand, incl, inclf, eye, first_row):
    c = x.shape[0]
    nh = SSD_GRP
    p = SSD_HEAD_DIM
    cum_row = _dot_nt(rows[nh:2 * nh], inclf)
    cols = _dot_nt(jnp.concatenate([inclf, eye], axis=0), rows)
    cum = cols[0:c, nh:2 * nh]
    dt = cols[c:2 * c, 0:nh]
    total = cum[first_row:first_row + 1, :]
    xdt = x * _dot(dt, expand)
    cb = _dot_nt(cm, bm)
    from_state = _dot(cm, state) * _dot(jnp.exp(cum), expand)
    parts = []
    for h in range(nh):
        seg = jnp.where(incl, jnp.exp(cum[:, h:h + 1] - cum_row[h:h + 1, :]), 0.0)
        parts.append(_dot(cb * seg, xdt[:, h * p:(h + 1) * p]))
    y = jnp.concatenate(parts, axis=1) + from_state
    carry_in = xdt * _dot(jnp.exp(total - cum), expand)
    state = state * _dot(jnp.exp(total), expand) + _dot_tn(bm, carry_in)
    return y, state


def _ssd_kernel(xf_ref, bf_ref, cf_ref, gf_ref, xb_ref, bb_ref, cb_ref, gb_ref, s0_ref, e_ref,
                yf_ref, yb_ref, sfin_ref, s_ref):
    i = pl.program_id(2)
    nck = gf_ref.shape[2]
    c = CHUNK
    nh = SSD_GRP

    @pl.when(i == 0)
    def _():
        s_ref[...] = s0_ref[0, 0]

    r, s = _tri_masks(c)
    eye = (r == s).astype(F32)
    lower, upper = r >= s, r <= s
    lowerf, upperf = lower.astype(F32), upper.astype(F32)
    expand = e_ref[...]

    def body(ci, carry):
        cf = ci
        cb = nck - 1 - ci
        rf = pl.ds(pl.multiple_of(cf * c, c), c)
        rb = pl.ds(pl.multiple_of(cb * c, c), c)
        y, st = _ssd_chunk(xf_ref[0, rf, :], bf_ref[0, rf, :], cf_ref[0, rf, :], gf_ref[0, 0, cf][0:2 * nh],
                           s_ref[0], expand, lower, lowerf, eye, c - 1)
        yf_ref[0, rf, :] = y
        s_ref[0] = st
        y, st = _ssd_chunk(xb_ref[0, rb, :], bb_ref[0, rb, :], cb_ref[0, rb, :], gb_ref[0, 0, cb][2 * nh:4 * nh],
                           s_ref[1], expand, upper, upperf, eye, 0)
        yb_ref[0, rb, :] = y
        s_ref[1] = st
        return carry

    lax.fori_loop(0, nck, body, 0)

    @pl.when(i == pl.num_programs(2) - 1)
    def _():
        sfin_ref[0, 0] = s_ref[...]


def _ssd_scan(xs, bc, grows, state0, expand):
    b, l, _ = xs.shape
    tl = min(SCAN_TILE, l)
    nb = l // tl
    nck = tl // CHUNK
    gwid = SSD_GRP * SSD_HEAD_DIM
    ng = SSD_GROUPS
    xf = pl.BlockSpec((1, tl, gwid), lambda bi, g, i: (bi, i, g))
    xb = pl.BlockSpec((1, tl, gwid), lambda bi, g, i: (bi, nb - 1 - i, g))
    bf = pl.BlockSpec((1, tl, SSD_STATE), lambda bi, g, i: (bi, i, g))
    cf = pl.BlockSpec((1, tl, SSD_STATE), lambda bi, g, i: (bi, i, ng + g))
    bb = pl.BlockSpec((1, tl, SSD_STATE), lambda bi, g, i: (bi, nb - 1 - i, g))
    cb = pl.BlockSpec((1, tl, SSD_STATE), lambda bi, g, i: (bi, nb - 1 - i, ng + g))
    gf = pl.BlockSpec((1, 1, nck, 4 * SSD_GRP, CHUNK), lambda bi, g, i: (bi, g, i, 0, 0))
    gb = pl.BlockSpec((1, 1, nck, 4 * SSD_GRP, CHUNK), lambda bi, g, i: (bi, g, nb - 1 - i, 0, 0))
    st = pl.BlockSpec((1, 1, 2, SSD_STATE, gwid), lambda bi, g, i: (bi, g, 0, 0, 0))
    return pl.pallas_call(
        _ssd_kernel,
        grid=(b, ng, nb),
        in_specs=[xf, bf, cf, gf, xb, bb, cb, gb, st, _resident(expand.shape)],
        out_specs=[xf, xb, st],
        out_shape=[jax.ShapeDtypeStruct(xs.shape, F32), jax.ShapeDtypeStruct(xs.shape, F32),
                   jax.ShapeDtypeStruct(state0.shape, F32)],
        scratch_shapes=[pltpu.VMEM((2, SSD_STATE, gwid), F32)],
        compiler_params=_params("parallel", "parallel", "arbitrary"),
        name="ssd_scan",
    )(xs, bc, bc, grows, xs, bc, bc, grows, state0, expand)


def _evmerge_kernel(of_ref, ob_ref, yf_ref, yb_ref, xs_ref, z_ref, dng_ref, sd_ref, sg_ref, a_ref, b_ref):
    for s in range(0, DN_VW, LANE):
        o = of_ref[0, :, s:s + LANE] + ob_ref[0, :, s:s + LANE]
        o = o * lax.rsqrt(jnp.mean(o * o, axis=-1, keepdims=True) + EPS) * dng_ref[...]
        a_ref[0, :, s:s + LANE] = (o * _silu(z_ref[0, :, s:s + LANE])).astype(BF16)
    gwid = SSD_INNER // SSD_GROUPS
    for s in range(0, SSD_INNER, gwid):
        y = yf_ref[0, :, s:s + gwid] + yb_ref[0, :, s:s + gwid] + sd_ref[:, s:s + gwid] * xs_ref[0, :, s:s + gwid]
        y = y * _silu(z_ref[0, :, DN_VW + s:DN_VW + s + gwid])
        y = y * lax.rsqrt(jnp.mean(y * y, axis=-1, keepdims=True) + EPS) * sg_ref[:, s:s + gwid]
        b_ref[0, :, s:s + gwid] = y.astype(BF16)


def _even_merge(o_f, o_b, y_f, y_b, xs, z, dn_gain, ssd_d, ssd_gain):
    b, l, _ = o_f.shape
    tm = min(ROW_TILE, l)
    row = lambda c: pl.BlockSpec((1, tm, c), lambda bi, i: (bi, i, 0))
    return pl.pallas_call(
        _evmerge_kernel,
        grid=(b, l // tm),
        in_specs=[row(DN_VW), row(DN_VW), row(SSD_INNER), row(SSD_INNER), row(SSD_INNER), row(EV_GATE),
                  _resident(dn_gain.shape), _resident(ssd_d.shape), _resident(ssd_gain.shape)],
        out_specs=[row(DN_VW), row(SSD_INNER)],
        out_shape=[jax.ShapeDtypeStruct((b, l, DN_VW), BF16), jax.ShapeDtypeStruct((b, l, SSD_INNER), BF16)],
        compiler_params=_params("parallel", "parallel"),
        name="even_merge",
    )(o_f, o_b, y_f, y_b, xs, z, dn_gain, ssd_d, ssd_gain)


def _mixffn_kernel(x_ref, a_ref, b_ref, mod_ref, gain_ref, wo_ref, wi_ref, wf_ref, o_ref, act_ref):
    d = x_ref.shape[2]
    na = a_ref.shape[2]
    hid = wf_ref.shape[0]
    g1 = mod_ref[0, :, 2 * d:3 * d]
    shift = mod_ref[0, :, 3 * d:4 * d]
    scale = mod_ref[0, :, 4 * d:5 * d]
    g2 = mod_ref[0, :, 5 * d:6 * d]
    mixed = _dot(a_ref[0], wo_ref[0:na, :]) + _dot(b_ref[0], wo_ref[na:, :])
    x1 = x_ref[0] + g1 * mixed
    h = _norm_mod(x1, gain_ref[...], shift, scale).astype(BF16)
    half = hid // 2
    for s in range(0, hid, half):
        gate = _dot(h, wi_ref[:, s:s + half])
        up = _dot(h, wi_ref[:, hid + s:hid + s + half])
        act_ref[:, s:s + half] = (_silu(gate) * up).astype(BF16)
    o_ref[0] = x1 + g2 * _dot(act_ref[...], wf_ref[...])


def _mix_ffn(x, a, bmix, modv, gain, w_out, w_in, w_ffn_out):
    b, l, d = x.shape
    tm = min(ROW_TILE, l)
    hid = w_ffn_out.shape[0]
    row = lambda c: pl.BlockSpec((1, tm, c), lambda bi, i: (bi, i, 0))
    return pl.pallas_call(
        _mixffn_kernel,
        grid=(b, l // tm),
        in_specs=[row(d), row(a.shape[2]), row(bmix.shape[2]),
                  pl.BlockSpec((1, 1, modv.shape[2]), lambda bi, i: (bi, 0, 0)),
                  _resident(gain.shape), _resident(w_out.shape), _resident(w_in.shape), _resident(w_ffn_out.shape)],
        out_specs=row(d),
        out_shape=jax.ShapeDtypeStruct(x.shape, F32),
        scratch_shapes=[pltpu.VMEM((tm, hid), BF16)],
        compiler_params=_params("parallel", "parallel"),
        name="mix_ffn",
    )(x, a, bmix, modv, gain, w_out, w_in, w_ffn_out)


def _rope(x, cos, sin):
    lane = lax.broadcasted_iota(jnp.int32, x.shape, 1)
    partner = jnp.where(lane % 32 < 16, pltpu.roll(x, LANE - 16, 1), pltpu.roll(x, 16, 1))
    return x * cos + partner * sin


def _odin_kernel(x_ref, c_ref, mod_ref, gain_ref, w_ref, qan_ref, wuq_ref, kvan_ref, wukv_ref, dqn_ref, dkn_ref,
                 mqn_ref, mkn_ref, cos_ref, sin_ref, daq_ref, dak_ref, davt_ref, mq_ref, mk_ref, mvt_ref, *, n_lat):
    d = x_ref.shape[2]
    xin = jnp.where(pl.program_id(1) < n_lat, x_ref[0], c_ref[0])
    h = _norm_mod(xin, gain_ref[...], mod_ref[0, 0, :, 0:d], mod_ref[0, 0, :, d:2 * d]).astype(BF16)
    cos = cos_ref[...]
    sin = sin_ref[...]
    lane = lax.broadcasted_iota(jnp.int32, (h.shape[0], LANE), 1)
    low = lane < DA_DIM
    ones = jnp.ones((V_ROWS - MLA_V, ATT_TK), BF16)

    def store_values_t(out_ref, vals):
        for hd in range(vals.shape[1] // LANE):
            for t in range(vals.shape[0] // ATT_TK):
                tile = vals[t * ATT_TK:(t + 1) * ATT_TK, hd * LANE:(hd + 1) * LANE]
                out_ref[0, hd, t, 0:MLA_V, :] = tile.T.astype(BF16)
                out_ref[0, hd, t, MLA_V:, :] = ones

    def pair_norm(u, g):
        sq = u * u
        lo = jnp.sum(jnp.where(low, sq, 0.0), axis=-1, keepdims=True)
        hi = jnp.sum(sq, axis=-1, keepdims=True) - lo
        ms = jnp.where(low, lo, hi) * (1.0 / DA_DIM)
        return u * lax.rsqrt(ms + EPS) * g

    def da_part(out_ref, base, g, out_scale):
        p = _dot(h, w_ref[:, base:base + DA_QK])
        for s in range(0, DA_QK, LANE):
            u = _rope(pair_norm(p[:, s:s + LANE], g), cos, sin)
            out_ref[0, :, s:s + LANE] = (u * out_scale).astype(BF16)

    da_part(daq_ref, 0, dqn_ref[...], DA_DIM ** -0.5 * LOG2E)
    da_part(dak_ref, DA_QK, dkn_ref[...], 1.0)
    store_values_t(davt_ref, _dot(h, w_ref[:, 2 * DA_QK:2 * DA_QK + DA_VW]))

    def low_norm(u, g):
        ms = jnp.sum(u * u, axis=-1, keepdims=True) * (1.0 / MLA_ROPE)
        return u * lax.rsqrt(ms + EPS) * g

    def full_norm(u, g):
        return u * lax.rsqrt(jnp.mean(u * u, axis=-1, keepdims=True) + EPS) * g

    c0 = 2 * DA_QK + DA_VW
    cq = full_norm(_dot(h, w_ref[:, c0:c0 + MLA_Q_RANK]), qan_ref[...]).astype(BF16)
    qf = _dot(cq, wuq_ref[...])
    scale = (MLA_NOPE + MLA_ROPE) ** -0.5 * LOG2E
    for hd in range(MLA_HEADS):
        s = hd * MLA_QK_PAD
        mq_ref[0, :, s:s + LANE] = (full_norm(qf[:, s:s + LANE], mqn_ref[:, 0:LANE]) * scale).astype(BF16)
        u = _rope(low_norm(qf[:, s + LANE:s + 2 * LANE], mqn_ref[:, LANE:2 * LANE]), cos, sin)
        mq_ref[0, :, s + LANE:s + 2 * LANE] = (u * scale).astype(BF16)
    c0 += MLA_Q_RANK
    ckv = full_norm(_dot(h, w_ref[:, c0:c0 + MLA_KV_RANK]), kvan_ref[...]).astype(BF16)
    kvu = _dot(ckv, wukv_ref[...])
    c0 += MLA_KV_RANK
    kr = _rope(low_norm(_dot(h, w_ref[:, c0:c0 + LANE]), mkn_ref[:, LANE:2 * LANE]), cos, sin).astype(BF16)
    for hd in range(MLA_HEADS):
        s = hd * MLA_QK_PAD
        mk_ref[0, :, s:s + LANE] = full_norm(kvu[:, hd * LANE:(hd + 1) * LANE], mkn_ref[:, 0:LANE]).astype(BF16)
        mk_ref[0, :, s + LANE:s + 2 * LANE] = kr
    store_values_t(mvt_ref, kvu[:, MLA_HEADS * MLA_NOPE:])


def _odd_in(x, ctx, mod2, gain, w_in, q_a_gain, w_uq, kv_a_gain, w_ukv, dq_gain, dk_gain, mq_gain, mk_gain,
            cos_t, sin_t):
    b, l, d = x.shape
    lc = ctx.shape[1]
    tm = min(ODD_TILE, l, lc)
    n_lat, n_ctx = l // tm, lc // tm
    lk = l + lc
    tiles = tm // ATT_TK
    row = lambda c: pl.BlockSpec((1, tm, c), lambda bi, i: (bi, i, 0))
    vt = pl.BlockSpec((1, MLA_HEADS, tiles, V_ROWS, ATT_TK), lambda bi, i: (bi, 0, i, 0, 0))
    widths = (DA_QK, DA_QK, None, MLA_HEADS * MLA_QK_PAD, MLA_HEADS * MLA_QK_PAD, None)
    table = pl.BlockSpec((tm, LANE), lambda bi, i: (i, 0))
    res = [gain, w_in, q_a_gain, w_uq, kv_a_gain, w_ukv, dq_gain, dk_gain, mq_gain, mk_gain]
    vt_shape = jax.ShapeDtypeStruct((b, MLA_HEADS, lk // ATT_TK, V_ROWS, ATT_TK), BF16)
    return pl.pallas_call(
        functools.partial(_odin_kernel, n_lat=n_lat),
        grid=(b, n_lat + n_ctx),
        in_specs=[pl.BlockSpec((1, tm, d), lambda bi, i: (bi, jnp.minimum(i, n_lat - 1), 0)),
                  pl.BlockSpec((1, tm, d), lambda bi, i: (bi, jnp.maximum(i - n_lat, 0), 0)),
                  pl.BlockSpec((1, 1, 1, mod2.shape[3]), lambda bi, i: (bi, i // n_lat, 0, 0))]
        + [_resident(a.shape) for a in res] + [table, table],
        out_specs=[vt if c is None else row(c) for c in widths],
        out_shape=[vt_shape if c is None else jax.ShapeDtypeStruct((b, lk, c), BF16) for c in widths],
        compiler_params=_params("parallel", "parallel"),
        name="odd_in",
    )(x, ctx, mod2, *res, cos_t, sin_t)


def _flash_scratch(nchain, strip, tk, qk_width):
    half = [pltpu.VMEM((nchain, tk, strip), F32), pltpu.VMEM((nchain, 1, strip), F32)]
    return [pltpu.VMEM((nchain, 1, strip), F32), pltpu.VMEM((nchain, V_ROWS, strip), F32),
            pltpu.VMEM((nchain, qk_width, strip), BF16)] + half + half


def _flash_loop(q_ref, k_ref, vt_ref, scratch, strip, split_q):
    m_ref, acc_ref, qz_ref, s0_ref, mx0_ref, s1_ref, mx1_ref = scratch
    s_ref, mx_ref = (s0_ref, s1_ref), (mx0_ref, mx1_ref)
    tq = q_ref.shape[1]
    tk = vt_ref.shape[4]
    nk = vt_ref.shape[2]
    nchain = m_ref.shape[0]
    m_ref[...] = jnp.full(m_ref.shape, -jnp.inf, F32)
    acc_ref[...] = jnp.zeros(acc_ref.shape, F32)
    row = lax.broadcasted_iota(jnp.int32, (q_ref.shape[2], strip), 0)
    for st in range(tq // strip):
        qt = q_ref[0, pl.ds(st * strip, strip), :].astype(F32).T
        if split_q:
            qz_ref[2 * st] = jnp.where(row < DA_DIM, qt, 0.0).astype(BF16)
            qz_ref[2 * st + 1] = jnp.where(row < DA_DIM, 0.0, qt).astype(BF16)
        else:
            qz_ref[st] = qt.astype(BF16)

    def scores(j, slot):
        k = k_ref[0, pl.ds(pl.multiple_of(j * tk, tk), tk), :]
        for c in range(nchain):
            st = _dot(k, qz_ref[c])
            s_ref[slot][c] = st
            mx_ref[slot][c] = jnp.max(st, axis=0, keepdims=True)

    def weigh(j, slot):
        vt1 = vt_ref[0, 0, j]
        alphas, pvs = [], []
        for c in range(nchain):
            m_old = m_ref[c]
            m_new = jnp.maximum(m_old, mx_ref[slot][c])
            p = jnp.exp2((s_ref[slot][c] - m_new).astype(BF16))
            alphas.append(jnp.exp2(m_old - m_new))
            m_ref[c] = m_new
            pvs.append(_dot(vt1, p))
        return alphas, pvs

    def accumulate(pending):
        for c, (alpha, pv) in enumerate(zip(*pending)):
            acc_ref[c] = alpha * acc_ref[c] + pv

    unroll = ATT_UNROLL
    scores(0, 0)

    def body(i, carry):
        t = unroll * i
        pending = None
        for u in range(unroll):
            scores(jnp.minimum(t + u + 1, nk - 1), (u + 1) % 2)
            if pending is not None:
                accumulate(pending)
            pending = weigh(t + u, u % 2)
        accumulate(pending)
        return carry

    lax.fori_loop(0, nk // unroll, body, 0)
    first = nk - nk % unroll
    for j in range(first, nk):
        if j + 1 < nk:
            scores(j + 1, (j - first + 1) % 2)
        accumulate(weigh(j, (j - first) % 2))


def _softmax_out(acc):
    return acc[0:MLA_V] / acc[MLA_V:MLA_V + 1]


def _da_kernel(q_ref, k_ref, vt_ref, lam_ref, gain_ref, o_ref, *scratch, strip, lambda_init):
    _flash_loop(q_ref, k_ref, vt_ref, scratch, strip, True)
    acc_ref = scratch[1]
    lp = lam_ref[...]
    lam = (jnp.exp(jnp.sum(lp[0:1] * lp[1:2], axis=-1, keepdims=True))
           - jnp.exp(jnp.sum(lp[2:3] * lp[3:4], axis=-1, keepdims=True)) + lambda_init)
    for st in range(q_ref.shape[1] // strip):
        o = _softmax_out(acc_ref[2 * st]) - lam * _softmax_out(acc_ref[2 * st + 1])
        o = o * lax.rsqrt(jnp.mean(o * o, axis=0, keepdims=True) + EPS) * gain_ref[...]
        o_ref[0, pl.ds(st * strip, strip), :] = (o * (1.0 - lambda_init)).T.astype(BF16)


def _diff_attention(q, k, vt, lq, lam_p, gain_col, lambda_init):
    b, lk, _ = k.shape
    tq = min(DA_TQ, lq)
    strip = min(ATT_STRIP, tq)
    qs = pl.BlockSpec((1, tq, LANE), lambda bi, h, i: (bi, i, h))
    ks = pl.BlockSpec((1, lk, LANE), lambda bi, h, i: (bi, 0, h))
    vs = pl.BlockSpec((1, 1) + vt.shape[2:], lambda bi, h, i: (bi, h, 0, 0, 0))
    nchain = 2 * (tq // strip)
    return pl.pallas_call(
        functools.partial(_da_kernel, strip=strip, lambda_init=lambda_init),
        grid=(b, DA_HEADS, lq // tq),
        in_specs=[qs, ks, vs, _resident(lam_p.shape), _resident(gain_col.shape)],
        out_specs=qs,
        out_shape=jax.ShapeDtypeStruct((b, lq, q.shape[2]), BF16),
        scratch_shapes=_flash_scratch(nchain, strip, ATT_TK, LANE),
        compiler_params=_params("parallel", "parallel", "parallel"),
        name="diff_attention",
    )(q, k, vt, lam_p, gain_col)


def _mla_kernel(q_ref, k_ref, vt_ref, o_ref, *scratch, strip):
    _flash_loop(q_ref, k_ref, vt_ref, scratch, strip, False)
    acc_ref = scratch[1]
    for st in range(q_ref.shape[1] // strip):
        o_ref[0, pl.ds(st * strip, strip), :] = _softmax_out(acc_ref[st]).T.astype(BF16)


def _mla_attention(q, k, vt, lq):
    b, lk, _ = k.shape
    tq = min(MLA_TQ, lq)
    strip = min(ATT_STRIP, tq)
    qs = pl.BlockSpec((1, tq, MLA_QK_PAD), lambda bi, h, i: (bi, i, h))
    ks = pl.BlockSpec((1, lk, MLA_QK_PAD), lambda bi, h, i: (bi, 0, h))
    vs = pl.BlockSpec((1, 1) + vt.shape[2:], lambda bi, h, i: (bi, h, 0, 0, 0))
    nchain = tq // strip
    return pl.pallas_call(
        functools.partial(_mla_kernel, strip=strip),
        grid=(b, MLA_HEADS, lq // tq),
        in_specs=[qs, ks, vs],
        out_specs=pl.BlockSpec((1, tq, MLA_V), lambda bi, h, i: (bi, i, h)),
        out_shape=jax.ShapeDtypeStruct((b, lq, MLA_HEADS * MLA_V), BF16),
        scratch_shapes=_flash_scratch(nchain, strip, ATT_TK, MLA_QK_PAD),
        compiler_params=_params("parallel", "parallel", "parallel"),
        name="mla_attention",
    )(q, k, vt)


def _scan_rows(gates_t, index, chunks):
    b, _, l = gates_t.shape
    idx = jnp.asarray(index, jnp.int32)
    rows = gates_t[:, idx, :]
    rows = rows.reshape(b, idx.shape[0], idx.shape[1], chunks, l // chunks)
    return jnp.swapaxes(rows, 2, 3)


def _even_layer(x, ctx, mod_x, mod_c, p):
    b = x.shape[0]
    nh, ns = DN_HEADS, SSD_HEADS
    dn_index = [[d * nh + h for h in range(nh)] + [2 * nh + d * nh + h for h in range(nh)] for d in range(2)]
    dt0, a0 = 4 * nh, 4 * nh + 2 * ns
    ssd_index = [[dt0 + g * SSD_GRP + r for r in range(SSD_GRP)] + [a0 + g * SSD_GRP + r for r in range(SSD_GRP)]
                 + [dt0 + ns + g * SSD_GRP + r for r in range(SSD_GRP)] + [a0 + ns + g * SSD_GRP + r for r in range(SSD_GRP)]
                 for g in range(SSD_GROUPS)]

    def prepare(t, modv):
        q, k, v, xs, bc, z, gates = _even_in(t, modv, p["gain_mix"], p["w_main"], p["w_small"], p["conv_w"], p["conv_b"],
                                             p["gate_bias"], p["gate_scale"])
        gt = jnp.swapaxes(gates, 1, 2)
        nchunk = t.shape[1] // CHUNK
        return dict(q=q, k=k, v=v, xs=xs, bc=bc, z=z, dn_rows=jnp.swapaxes(_scan_rows(gt, dn_index, nchunk), 1, 2),
                    ssd_rows=_scan_rows(gt, ssd_index, nchunk))

    pc = prepare(ctx, mod_c)
    pL = prepare(x, mod_x)
    dn0 = jnp.zeros((b, 2 * nh, DN_DK, DN_DV), F32)
    ssd0 = jnp.zeros((b, SSD_GROUPS, 2, SSD_STATE, SSD_GRP * SSD_HEAD_DIM), F32)
    ocf, ocb, dn1 = _dn_scan(pc["q"], pc["k"], pc["v"], pc["dn_rows"], dn0)
    olf, olb, _ = _dn_scan(pL["q"], pL["k"], pL["v"], pL["dn_rows"], dn1)
    ycf, ycb, ssd1 = _ssd_scan(pc["xs"], pc["bc"], pc["ssd_rows"], ssd0, p["expand"])
    ylf, ylb, _ = _ssd_scan(pL["xs"], pL["bc"], pL["ssd_rows"], ssd1, p["expand"])

    def finish(t, modv, pp, of, ob, yf, yb):
        a, bm = _even_merge(of, ob, yf, yb, pp["xs"], pp["z"], p["dn_gain"], p["ssd_d"], p["ssd_gain"])
        return _mix_ffn(t, a, bm, modv, p["gain_ffn"], p["w_out"], p["ffn_w_in"], p["ffn_w_out"])

    return finish(x, mod_x, pL, olf, olb, ylf, ylb), finish(ctx, mod_c, pc, ocf, ocb, ycf, ycb)


def _rope_tables(n_tokens):
    lane = jnp.arange(LANE)
    quarter = (lane % 64) // 16
    n_freq = DA_DIM // 4
    inv_freq = ROPE_THETA ** (-(lane % 16).astype(F32) / n_freq)
    tok = jnp.arange(n_tokens)
    pos = jnp.where(quarter[None, :] < 2, (tok // GRID_W)[:, None], (tok % GRID_W)[:, None]).astype(F32)
    ang = pos * inv_freq[None, :]
    sign = jnp.where(quarter % 2 == 0, -1.0, 1.0).astype(F32)
    return jnp.cos(ang), jnp.sin(ang) * sign[None, :]


def _odd_layer(x, ctx, mod_x, mod_c, p, lambda_init):
    lq, lc = x.shape[1], ctx.shape[1]
    cos_t, sin_t = _rope_tables(lq)
    cos_t = jnp.concatenate([cos_t, jnp.ones((lc, LANE), F32)], axis=0)
    sin_t = jnp.concatenate([sin_t, jnp.zeros((lc, LANE), F32)], axis=0)
    mod2 = jnp.stack([mod_x, mod_c], axis=1)
    daq, dak, davt, mq, mk, mvt = _odd_in(x, ctx, mod2, p["gain_mix"], p["w_in"], p["q_a_gain"], p["w_uq"], p["kv_a_gain"],
                                          p["w_ukv"], p["dq_gain"], p["dk_gain"], p["mq_gain"], p["mk_gain"], cos_t, sin_t)
    da = _diff_attention(daq, dak, davt, lq, p["da_lambda"], p["sub_gain"], lambda_init)
    ml = _mla_attention(mq, mk, mvt, lq)
    return _mix_ffn(x, da, ml, mod_x, p["gain_ffn"], p["w_out"], p["ffn_w_in"], p["ffn_w_out"])


def _even_params(i, j, norm_mix, norm_ffn, ffn_w_in, ffn_w_out, ev_w_in, ev_conv_w, ev_conv_b, dn_a_log, dn_dt_bias,
                 dn_norm, ssd_a_log, ssd_dt_bias, ssd_d, ssd_norm, ev_w_out):
    d = norm_mix.shape[1]
    w = ev_w_in[j]
    small = w[:, EV_CONV + EV_GATE:]
    nh, ns = DN_HEADS, SSD_HEADS
    w_small = jnp.concatenate([small, small[:, 4 * nh:], jnp.zeros((d, LANE - 4 * nh - 4 * ns), F32)], axis=1)
    zeros = lambda n: jnp.zeros((n,), F32)
    gate_bias = jnp.concatenate([zeros(2 * nh), dn_dt_bias[j].reshape(-1), ssd_dt_bias[j].reshape(-1),
                                 ssd_dt_bias[j].reshape(-1), zeros(LANE - 4 * nh - 4 * ns)])
    gate_scale = jnp.concatenate([jnp.ones((2 * nh,), F32), -jnp.exp(dn_a_log[j].reshape(-1)), jnp.ones((2 * ns,), F32),
                                  -jnp.exp(ssd_a_log[j].reshape(-1)), zeros(LANE - 4 * nh - 4 * ns)])
    expand = jnp.repeat(jnp.eye(SSD_GRP, dtype=F32), SSD_HEAD_DIM, axis=1)
    return dict(
        gain_mix=norm_mix[i].reshape(1, d), gain_ffn=norm_ffn[i].reshape(1, d),
        w_main=w[:, :EV_CONV + EV_GATE].astype(BF16), w_small=w_small,
        conv_w=ev_conv_w[j], conv_b=ev_conv_b[j].reshape(1, -1),
        gate_bias=gate_bias.reshape(1, LANE), gate_scale=gate_scale.reshape(1, LANE), expand=expand,
        dn_gain=dn_norm[j].reshape(1, DN_DV), ssd_d=jnp.repeat(ssd_d[j], SSD_HEAD_DIM).reshape(1, SSD_INNER),
        ssd_gain=ssd_norm[j].reshape(1, SSD_INNER), w_out=ev_w_out[j].astype(BF16),
        ffn_w_in=ffn_w_in[i].astype(BF16), ffn_w_out=ffn_w_out[i].astype(BF16))


def _odd_params(i, j, norm_mix, norm_ffn, ffn_w_in, ffn_w_out, od_w_in, da_q_norm, da_k_norm, da_lambda, da_sub_norm,
                mla_q_a_norm, mla_w_uq, mla_kv_a_norm, mla_w_ukv, mla_q_norm, mla_k_norm, od_w_out):
    d = norm_mix.shape[1]
    w = od_w_in[j]
    w_in = jnp.concatenate([w, jnp.zeros((d, LANE - MLA_ROPE), F32)], axis=1).astype(BF16)
    hq = mla_w_uq[j].reshape(MLA_Q_RANK, MLA_HEADS, MLA_NOPE + MLA_ROPE)
    hq = jnp.pad(hq, ((0, 0), (0, 0), (0, MLA_QK_PAD - MLA_NOPE - MLA_ROPE)))
    hkv = mla_w_ukv[j].reshape(MLA_KV_RANK, MLA_HEADS, MLA_NOPE + MLA_V)
    w_ukv = jnp.concatenate([hkv[:, :, :MLA_NOPE].reshape(MLA_KV_RANK, -1), hkv[:, :, MLA_NOPE:].reshape(MLA_KV_RANK, -1)], axis=1)
    pad_gain = lambda g: jnp.pad(g, (0, MLA_QK_PAD - MLA_NOPE - MLA_ROPE)).reshape(1, MLA_QK_PAD)
    return dict(
        gain_mix=norm_mix[i].reshape(1, d), gain_ffn=norm_ffn[i].reshape(1, d), w_in=w_in,
        q_a_gain=mla_q_a_norm[j].reshape(1, -1), w_uq=hq.reshape(MLA_Q_RANK, -1).astype(BF16),
        kv_a_gain=mla_kv_a_norm[j].reshape(1, -1), w_ukv=w_ukv.astype(BF16),
        dq_gain=jnp.tile(da_q_norm[j], 2).reshape(1, LANE), dk_gain=jnp.tile(da_k_norm[j], 2).reshape(1, LANE),
        mq_gain=pad_gain(mla_q_norm[j]), mk_gain=pad_gain(mla_k_norm[j]),
        da_lambda=jnp.pad(da_lambda[j], ((0, 4), (0, LANE - DA_DIM))), sub_gain=da_sub_norm[j].reshape(2 * DA_DIM, 1),
        w_out=od_w_out[j].astype(BF16), ffn_w_in=ffn_w_in[i].astype(BF16), ffn_w_out=ffn_w_out[i].astype(BF16))


def kernel(x, c, ctx, c_ctx, ada_w, ada_b, norm_mix, norm_ffn, ffn_w_in, ffn_w_out, ev_w_in, ev_conv_w, ev_conv_b, dn_a_log, dn_dt_bias, dn_norm, ssd_a_log, ssd_dt_bias, ssd_d, ssd_norm, ev_w_out, od_w_in, da_q_norm, da_k_norm, da_lambda, da_sub_norm, mla_q_a_norm, mla_w_uq, mla_kv_a_norm, mla_w_ukv, mla_q_norm, mla_k_norm, od_w_out):
    b, _, d = x.shape
    depth = ada_w.shape[0]
    assert b < 8
    cvec = jnp.concatenate([c, c_ctx[None, :], jnp.zeros((8 - b - 1, d), F32)], axis=0)
    mod = _modulation(cvec, ada_w, ada_b)
    for i in range(depth):
        last = i == depth - 1
        j = i // 2
        mod_x = mod[i, :b].reshape(b, 1, 6 * d)
        mod_c = jnp.broadcast_to(mod[i, b].reshape(1, 1, 6 * d), (b, 1, 6 * d))
        if i % 2 == 0:
            p = _even_params(i, j, norm_mix, norm_ffn, ffn_w_in, ffn_w_out, ev_w_in, ev_conv_w, ev_conv_b, dn_a_log,
                             dn_dt_bias, dn_norm, ssd_a_log, ssd_dt_bias, ssd_d, ssd_norm, ev_w_out)
            x, ctx_new = _even_layer(x, ctx, mod_x, mod_c, p)
        else:
            p = _odd_params(i, j, norm_mix, norm_ffn, ffn_w_in, ffn_w_out, od_w_in, da_q_norm, da_k_norm, da_lambda,
                            da_sub_norm, mla_q_a_norm, mla_w_uq, mla_kv_a_norm, mla_w_ukv, mla_q_norm, mla_k_norm, od_w_out)
            lambda_init = 0.8 - 0.6 * math.exp(-0.3 * i)
            if last:
                x = _odd_layer(x, ctx, mod_x, mod_c, p, lambda_init)
                ctx_new = ctx
            else:
                raise NotImplementedError("context update after an attention layer is not needed for depth 2")
        ctx = ctx_new
    return x
```

```python
import functools
import math

import jax
import jax.numpy as jnp
from jax import lax
from jax.experimental import pallas as pl
from jax.experimental.pallas import tpu as pltpu

F32 = jnp.float32
BF16 = jnp.bfloat16

GRID_W = 64
EPS = 1e-6
ROPE_THETA = 10000.0
CONV_K = 5
DN_HEADS = 8
DN_DK = 128
DN_DV = 128
SSD_HEADS = 16
SSD_HEAD_DIM = 64
SSD_GROUPS = 2
SSD_STATE = 128
SSD_GRP = SSD_HEADS // SSD_GROUPS
SSD_INNER = SSD_HEADS * SSD_HEAD_DIM
DA_HEADS = 8
DA_DIM = 64
MLA_HEADS = 8
MLA_Q_RANK = 512
MLA_KV_RANK = 256
MLA_NOPE = 128
MLA_ROPE = 64
MLA_V = 128
MLA_QK_PAD = 256

DN_QK = DN_HEADS * DN_DK
DN_VW = DN_HEADS * DN_DV
SSD_BC = SSD_GROUPS * SSD_STATE
EV_CONV = 2 * DN_QK + DN_VW + SSD_INNER + 2 * SSD_BC
EV_GATE = DN_VW + SSD_INNER
DA_QK = DA_HEADS * 2 * DA_DIM
DA_VW = DA_HEADS * 2 * DA_DIM

LANE = 128
BF16_ROWS = 16
CHUNK = 128
VMEM_LIMIT = 56 * 1024 * 1024

ROW_TILE = 512
ODD_TILE = 256
SCAN_TILE = 512
DN_TILE = 256
DA_TQ = 512
MLA_TQ = 1024
ATT_TK = 256
ATT_STRIP = 256
V_ROWS = MLA_V + BF16_ROWS
LOG2E = 1.4426950408889634


def _dot(a, b):
    return jnp.dot(a, b, preferred_element_type=F32)


def _dot_nt(a, b):
    return lax.dot_general(a, b, (((1,), (1,)), ((), ())), preferred_element_type=F32)


def _dot_tn(a, b):
    return lax.dot_general(a, b, (((0,), (0,)), ((), ())), preferred_element_type=F32)


def _sigmoid(x):
    return 1.0 / (1.0 + jnp.exp(-x))


def _silu(x):
    return x * _sigmoid(x)


def _softplus(x):
    return jnp.maximum(x, 0.0) + jnp.log(1.0 + jnp.exp(-jnp.abs(x)))


def _params(*sem):
    return pltpu.CompilerParams(dimension_semantics=sem, vmem_limit_bytes=VMEM_LIMIT)


def _resident(shape):
    nd = len(shape)
    return pl.BlockSpec(shape, lambda *_: (0,) * nd, pipeline_mode=pl.Buffered(1))


def _norm_mod(x, gain, shift, scale):
    y = x * lax.rsqrt(jnp.mean(x * x, axis=-1, keepdims=True) + EPS)
    return (y * gain) * (1.0 + scale) + shift


def _mod_kernel(c_ref, w_ref, b_ref, o_ref):
    o_ref[0] = _dot(_silu(c_ref[...]), w_ref[0]) + b_ref[0]


def _modulation(cvec, ada_w, ada_b):
    depth, d, n = ada_w.shape
    tn = n // 4
    return pl.pallas_call(
        _mod_kernel,
        grid=(depth, n // tn),
        in_specs=[
            pl.BlockSpec((8, d), lambda l, j: (0, 0)),
            pl.BlockSpec((1, d, tn), lambda l, j: (l, 0, j)),
            pl.BlockSpec((1, 1, tn), lambda l, j: (l, 0, j)),
        ],
        out_specs=pl.BlockSpec((1, 8, tn), lambda l, j: (l, 0, j)),
        out_shape=jax.ShapeDtypeStruct((depth, 8, n), F32),
        compiler_params=_params("parallel", "parallel"),
        name="modulation",
    )(cvec, ada_w, ada_b.reshape(depth, 1, n))


def _evin_kernel(xp_ref, xc_ref, xn_ref, mod_ref, gain_ref, w_ref, wsm_ref, cw_ref, cb_ref, gb_ref, gs_ref,
                 q_ref, k_ref, v_ref, xs_ref, bc_ref, z_ref, g_ref, hext_ref, pext_ref):
    i = pl.program_id(1)
    nb = pl.num_programs(1)
    tm = xc_ref.shape[1]
    d = xc_ref.shape[2]
    halo = BF16_ROWS
    gain = gain_ref[...]
    shift = mod_ref[0, :, 0:d]
    scale = mod_ref[0, :, d:2 * d]

    hc = _norm_mod(xc_ref[0], gain, shift, scale)
    hp = _norm_mod(xp_ref[0], gain, shift, scale)
    hn = _norm_mod(xn_ref[0], gain, shift, scale)
    hext_ref[0:halo, :] = jnp.where(i > 0, hp, 0.0).astype(BF16)
    hext_ref[halo:halo + tm, :] = hc.astype(BF16)
    hext_ref[halo + tm:, :] = jnp.where(i < nb - 1, hn, 0.0).astype(BF16)

    gw = pext_ref.shape[2]
    pad = CONV_K // 2
    plan = ((q_ref, 0, DN_QK, DN_DK ** -0.5), (k_ref, DN_QK, DN_QK, 1.0), (v_ref, 2 * DN_QK, DN_VW, None),
            (xs_ref, 2 * DN_QK + DN_VW, SSD_INNER, None), (bc_ref, 2 * DN_QK + DN_VW + SSD_INNER, 2 * SSD_BC, None))
    group = 0
    for out_ref, base, width, l2_scale in plan:
        for off in range(0, width, gw):
            c0 = base + off
            pbuf = pext_ref.at[group % 2]
            group += 1
            pbuf[...] = _dot(hext_ref[...], w_ref[:, c0:c0 + gw])
            acc = cb_ref[:, c0:c0 + gw] + cw_ref[0:1, c0:c0 + gw] * pbuf[halo - pad:halo - pad + tm, :]
            for t in range(1, CONV_K):
                acc = acc + cw_ref[t:t + 1, c0:c0 + gw] * pbuf[halo - pad + t:halo - pad + t + tm, :]
            u = _silu(acc)
            if l2_scale is None:
                out_ref[0, :, off:off + gw] = u
            else:
                for s in range(0, gw, LANE):
                    uh = u[:, s:s + LANE]
                    inv = lax.rsqrt(jnp.sum(uh * uh, axis=-1, keepdims=True) + EPS)
                    out_ref[0, :, off + s:off + s + LANE] = uh * (inv * l2_scale)

    hcb = hext_ref[halo:halo + tm, :]
    for off in range(0, EV_GATE, gw):
        z_ref[0, :, off:off + gw] = _dot(hcb, w_ref[:, EV_CONV + off:EV_CONV + off + gw])

    p = _dot(hc, wsm_ref[...]) + gb_ref[...]
    lane = lax.broadcasted_iota(jnp.int32, p.shape, 1)
    g_ref[0] = jnp.where(lane < 2 * DN_HEADS, _sigmoid(p), _softplus(p)) * gs_ref[...]


def _even_in(x, modv, gain, w_main, w_small, conv_w, conv_b, gate_bias, gate_scale):
    b, l, d = x.shape
    tm = min(ROW_TILE, l)
    nb = l // tm
    hb = tm // BF16_ROWS
    gw = 512
    row = lambda c: pl.BlockSpec((1, tm, c), lambda bi, i: (bi, i, 0))
    outs = (DN_QK, DN_QK, DN_VW, SSD_INNER, 2 * SSD_BC, EV_GATE, LANE)
    return pl.pallas_call(
        _evin_kernel,
        grid=(b, nb),
        in_specs=[
            pl.BlockSpec((1, BF16_ROWS, d), lambda bi, i: (bi, jnp.maximum(i * hb - 1, 0), 0)),
            row(d),
            pl.BlockSpec((1, BF16_ROWS, d), lambda bi, i: (bi, jnp.minimum((i + 1) * hb, l // BF16_ROWS - 1), 0)),
            pl.BlockSpec((1, 1, modv.shape[2]), lambda bi, i: (bi, 0, 0)),
            _resident(gain.shape), _resident(w_main.shape), _resident(w_small.shape), _resident(conv_w.shape),
            _resident(conv_b.shape), _resident(gate_bias.shape), _resident(gate_scale.shape),
        ],
        out_specs=[row(c) for c in outs],
        out_shape=[jax.ShapeDtypeStruct((b, l, c), F32) for c in outs],
        scratch_shapes=[pltpu.VMEM((tm + 2 * BF16_ROWS, d), BF16), pltpu.VMEM((2, tm + 2 * BF16_ROWS, gw), F32)],
        compiler_params=_params("parallel", "parallel"),
        name="even_in",
    )(x, x, x, modv, gain, w_main, w_small, conv_w, conv_b, gate_bias, gate_scale)


def _tri_masks(c):
    r = lax.broadcasted_iota(jnp.int32, (c, c), 0)
    s = lax.broadcasted_iota(jnp.int32, (c, c), 1)
    return r, s


TRI_BASE = 16


def _block_masks(r, s):
    same = lambda n: (r // n) == (s // n)
    masks = [same(TRI_BASE).astype(F32)]
    n = TRI_BASE
    while n < CHUNK:
        n *= 2
        masks.append(jnp.where(same(n), 1.0, 0.0) - jnp.where(same(n // 2), 1.0, 0.0))
    return masks


def _unit_tri_inverses(a, eye, blocks):
    p = [-(a_n * blocks[0]) for a_n in a]
    x = [eye + p_n for p_n in p]
    for _ in range(int(math.log2(TRI_BASE)) - 1):
        p = [_dot(p_n, p_n) for p_n in p]
        x = [x_n + _dot(x_n, p_n) for x_n, p_n in zip(x, p)]
    for m in blocks[1:]:
        e = [_dot(a_n * m, x_n) for a_n, x_n in zip(a, x)]
        x = [x_n - _dot(x_n, e_n) for x_n, e_n in zip(x, e)]
    return x


def _dn_chunks(q, k, v, gc, beta, crow, total, state, incl, strict, eye, blocks):
    n = range(len(q))
    c = q[0].shape[0]
    decay = [jnp.where(incl[i], jnp.exp(gc[i] - crow[i]), 0.0) for i in n]
    kb = [k[i] * beta[i] for i in n]
    a = [jnp.where(strict[i], _dot_nt(kb[i], k[i]) * decay[i], 0.0) for i in n]
    t = _unit_tri_inverses(a, eye, blocks)
    egc = [jnp.exp(gc[i]) for i in n]
    uw = [_dot(t[i], jnp.concatenate([v[i] * beta[i], kb[i] * egc[i]], axis=1)) for i in n]
    ws_qs = [_dot(jnp.concatenate([uw[i][:, DN_DV:], q[i] * egc[i]], axis=0), state[i]) for i in n]
    v_new = [uw[i][:, 0:DN_DV] - ws_qs[i][0:c] for i in n]
    attn = [_dot_nt(q[i], k[i]) * decay[i] for i in n]
    o = [ws_qs[i][c:2 * c] + _dot(attn[i], v_new[i]) for i in n]
    state = [state[i] * jnp.exp(total[i]) + _dot_tn(k[i] * jnp.exp(total[i] - gc[i]), v_new[i]) for i in n]
    return o, state


def _dn_kernel(qf_ref, kf_ref, vf_ref, gf_ref, qb_ref, kb_ref, vb_ref, gb_ref, s0_ref,
               of_ref, ob_ref, sfin_ref, s_ref):
    i = pl.program_id(1)
    nck = gf_ref.shape[1]
    c = CHUNK
    nh = DN_HEADS

    @pl.when(i == 0)
    def _():
        s_ref[...] = s0_ref[0]

    r, s = _tri_masks(c)
    eye = (r == s).astype(F32)
    blocks = _block_masks(r, s)
    dirs = ((qf_ref, kf_ref, vf_ref, gf_ref, of_ref, r >= s, r > s, c - 1),
            (qb_ref, kb_ref, vb_ref, gb_ref, ob_ref, r <= s, r < s, 0))

    def body(ci, carry):
        q, k, v, gc, beta, crow, total, state, incl, strict, dest = ([] for _ in range(11))
        for d, (q_ref, k_ref, v_ref, g_ref, o_ref, inc, stri, tot_row) in enumerate(dirs):
            cc = ci if d == 0 else nck - 1 - ci
            rows = pl.ds(pl.multiple_of(cc * c, c), c)
            inclf = inc.astype(F32)
            grow = g_ref[0, cc, d]
            cum_rows = _dot_nt(grow, inclf)
            cols = _dot_nt(jnp.concatenate([inclf, eye], axis=0), grow)
            for h in range(nh):
                lanes = slice(h * LANE, (h + 1) * LANE)
                q.append(q_ref[0, rows, lanes])
                k.append(k_ref[0, rows, lanes])
                v.append(v_ref[0, rows, lanes])
                gc.append(cols[0:c, nh + h:nh + h + 1])
                beta.append(cols[c:2 * c, h:h + 1])
                crow.append(cum_rows[nh + h:nh + h + 1, :])
                total.append(cols[tot_row:tot_row + 1, nh + h:nh + h + 1])
                state.append(s_ref[d * nh + h])
                incl.append(inc)
                strict.append(stri)
                dest.append((o_ref, rows, lanes))
        o, state = _dn_chunks(q, k, v, gc, beta, crow, total, state, incl, strict, eye, blocks)
        for n, (o_ref, rows, lanes) in enumerate(dest):
            o_ref[0, rows, lanes] = o[n]
            s_ref[n] = state[n]
        return carry

    lax.fori_loop(0, nck, body, 0)

    @pl.when(i == pl.num_programs(1) - 1)
    def _():
        sfin_ref[0] = s_ref[...]


def _dn_scan(q, k, v, grows, state0):
    b, l, width = q.shape
    tl = min(DN_TILE, l)
    nb = l // tl
    nck = tl // CHUNK
    fwd = pl.BlockSpec((1, tl, width), lambda bi, i: (bi, i, 0))
    bwd = pl.BlockSpec((1, tl, width), lambda bi, i: (bi, nb - 1 - i, 0))
    gfwd = pl.BlockSpec((1, nck) + grows.shape[2:], lambda bi, i: (bi, i, 0, 0, 0))
    gbwd = pl.BlockSpec((1, nck) + grows.shape[2:], lambda bi, i: (bi, nb - 1 - i, 0, 0, 0))
    st = pl.BlockSpec((1,) + state0.shape[1:], lambda bi, i: (bi, 0, 0, 0))
    return pl.pallas_call(
        _dn_kernel,
        grid=(b, nb),
        in_specs=[fwd, fwd, fwd, gfwd, bwd, bwd, bwd, gbwd, st],
        out_specs=[fwd, bwd, st],
        out_shape=[jax.ShapeDtypeStruct(q.shape, F32), jax.ShapeDtypeStruct(q.shape, F32),
                   jax.ShapeDtypeStruct(state0.shape, F32)],
        scratch_shapes=[pltpu.VMEM(state0.shape[1:], F32)],
        compiler_params=_params("parallel", "arbitrary"),
        name="deltanet_scan",
    )(q, k, v, grows, q, k, v, grows, state0)


def _ssd_chunks(x, bm, cm, rows, state, expand, incl, eye, first_row):
    n = range(len(x))
    c = x[0].shape[0]
    nh = SSD_GRP
    p = SSD_HEAD_DIM
    inclf = [incl[i].astype(F32) for i in n]
    cum_row = [_dot_nt(rows[i][nh:2 * nh], inclf[i]) for i in n]
    cols = [_dot_nt(jnp.concatenate([inclf[i], eye], axis=0), rows[i]) for i in n]
    cum = [cols[i][0:c, nh:2 * nh] for i in n]
    total = [cum[i][first_row[i]:first_row[i] + 1, :] for i in n]
    xdt = [x[i] * _dot(cols[i][c:2 * c, 0:nh], expand) for i in n]
    cb = [_dot_nt(cm[i], bm[i]) for i in n]
    from_state = [_dot(cm[i], state[i]) * _dot(jnp.exp(cum[i]), expand) for i in n]
    parts = [[] for _ in n]
    for h in range(nh):
        seg = [jnp.where(incl[i], jnp.exp(cum[i][:, h:h + 1] - cum_row[i][h:h + 1, :]), 0.0) for i in n]
        for i in n:
            parts[i].append(_dot(cb[i] * seg[i], xdt[i][:, h * p:(h + 1) * p]))
    y = [jnp.concatenate(parts[i], axis=1) + from_state[i] for i in n]
    carry_in = [xdt[i] * _dot(jnp.exp(total[i] - cum[i]), expand) for i in n]
    state = [state[i] * _dot(jnp.exp(total[i]), expand) + _dot_tn(bm[i], carry_in[i]) for i in n]
    return y, state


def _ssd_kernel(xf_ref, bcf_ref, gf_ref, xb_ref, bcb_ref, gb_ref, s0_ref, e_ref, yf_ref, yb_ref, sfin_ref, s_ref):
    i = pl.program_id(1)
    nck = gf_ref.shape[1]
    c = CHUNK
    nh = SSD_GRP
    ng = SSD_GROUPS
    gwid = nh * SSD_HEAD_DIM

    @pl.when(i == 0)
    def _():
        s_ref[...] = s0_ref[0]

    r, s = _tri_masks(c)
    eye = (r == s).astype(F32)
    expand = e_ref[...]
    dirs = ((xf_ref, bcf_ref, gf_ref, yf_ref, r >= s, c - 1), (xb_ref, bcb_ref, gb_ref, yb_ref, r <= s, 0))

    def body(ci, carry):
        x, bm, cm, rows, state, incl, first, dest = ([] for _ in range(8))
        for d, (x_ref, bc_ref, g_ref, y_ref, inc, tot_row) in enumerate(dirs):
            cc = ci if d == 0 else nck - 1 - ci
            tok = pl.ds(pl.multiple_of(cc * c, c), c)
            for g in range(ng):
                lanes = slice(g * gwid, (g + 1) * gwid)
                x.append(x_ref[0, tok, lanes])
                bm.append(bc_ref[0, tok, g * SSD_STATE:(g + 1) * SSD_STATE])
                cm.append(bc_ref[0, tok, (ng + g) * SSD_STATE:(ng + g + 1) * SSD_STATE])
                rows.append(g_ref[0, cc, g][2 * nh * d:2 * nh * (d + 1)])
                state.append(s_ref[g, d])
                incl.append(inc)
                first.append(tot_row)
                dest.append((y_ref, tok, lanes, g, d))
        y, state = _ssd_chunks(x, bm, cm, rows, state, expand, incl, eye, first)
        for n, (y_ref, tok, lanes, g, d) in enumerate(dest):
            y_ref[0, tok, lanes] = y[n]
            s_ref[g, d] = state[n]
        return carry

    lax.fori_loop(0, nck, body, 0)

    @pl.when(i == pl.num_programs(1) - 1)
    def _():
        sfin_ref[0] = s_ref[...]


def _ssd_scan(xs, bc, grows, state0, expand):
    b, l, width = xs.shape
    tl = min(SCAN_TILE, l)
    nb = l // tl
    nck = tl // CHUNK
    fwd = lambda w: pl.BlockSpec((1, tl, w), lambda bi, i: (bi, i, 0))
    bwd = lambda w: pl.BlockSpec((1, tl, w), lambda bi, i: (bi, nb - 1 - i, 0))
    gf = pl.BlockSpec((1, nck) + grows.shape[2:], lambda bi, i: (bi, i, 0, 0, 0))
    gb = pl.BlockSpec((1, nck) + grows.shape[2:], lambda bi, i: (bi, nb - 1 - i, 0, 0, 0))
    st = pl.BlockSpec((1,) + state0.shape[1:], lambda bi, i: (bi, 0, 0, 0, 0))
    return pl.pallas_call(
        _ssd_kernel,
        grid=(b, nb),
        in_specs=[fwd(width), fwd(bc.shape[2]), gf, bwd(width), bwd(bc.shape[2]), gb, st, _resident(expand.shape)],
        out_specs=[fwd(width), bwd(width), st],
        out_shape=[jax.ShapeDtypeStruct(xs.shape, F32), jax.ShapeDtypeStruct(xs.shape, F32),
                   jax.ShapeDtypeStruct(state0.shape, F32)],
        scratch_shapes=[pltpu.VMEM(state0.shape[1:], F32)],
        compiler_params=_params("parallel", "arbitrary"),
        name="ssd_scan",
    )(xs, bc, grows, xs, bc, grows, state0, expand)


def _evmerge_kernel(of_ref, ob_ref, yf_ref, yb_ref, xs_ref, z_ref, dng_ref, sd_ref, sg_ref, a_ref, b_ref):
    for s in range(0, DN_VW, LANE):
        o = of_ref[0, :, s:s + LANE] + ob_ref[0, :, s:s + LANE]
        o = o * lax.rsqrt(jnp.mean(o * o, axis=-1, keepdims=True) + EPS) * dng_ref[...]
        a_ref[0, :, s:s + LANE] = (o * _silu(z_ref[0, :, s:s + LANE])).astype(BF16)
    gwid = SSD_INNER // SSD_GROUPS
    for s in range(0, SSD_INNER, gwid):
        y = yf_ref[0, :, s:s + gwid] + yb_ref[0, :, s:s + gwid] + sd_ref[:, s:s + gwid] * xs_ref[0, :, s:s + gwid]
        y = y * _silu(z_ref[0, :, DN_VW + s:DN_VW + s + gwid])
        y = y * lax.rsqrt(jnp.mean(y * y, axis=-1, keepdims=True) + EPS) * sg_ref[:, s:s + gwid]
        b_ref[0, :, s:s + gwid] = y.astype(BF16)


def _even_merge(o_f, o_b, y_f, y_b, xs, z, dn_gain, ssd_d, ssd_gain):
    b, l, _ = o_f.shape
    tm = min(ROW_TILE, l)
    row = lambda c: pl.BlockSpec((1, tm, c), lambda bi, i: (bi, i, 0))
    return pl.pallas_call(
        _evmerge_kernel,
        grid=(b, l // tm),
        in_specs=[row(DN_VW), row(DN_VW), row(SSD_INNER), row(SSD_INNER), row(SSD_INNER), row(EV_GATE),
                  _resident(dn_gain.shape), _resident(ssd_d.shape), _resident(ssd_gain.shape)],
        out_specs=[row(DN_VW), row(SSD_INNER)],
        out_shape=[jax.ShapeDtypeStruct((b, l, DN_VW), BF16), jax.ShapeDtypeStruct((b, l, SSD_INNER), BF16)],
        compiler_params=_params("parallel", "parallel"),
        name="even_merge",
    )(o_f, o_b, y_f, y_b, xs, z, dn_gain, ssd_d, ssd_gain)


def _mixffn_kernel(x_ref, a_ref, b_ref, mod_ref, gain_ref, wo_ref, wi_ref, wf_ref, o_ref, act_ref):
    d = x_ref.shape[2]
    na = a_ref.shape[2]
    hid = wf_ref.shape[0]
    g1 = mod_ref[0, :, 2 * d:3 * d]
    shift = mod_ref[0, :, 3 * d:4 * d]
    scale = mod_ref[0, :, 4 * d:5 * d]
    g2 = mod_ref[0, :, 5 * d:6 * d]
    mixed = _dot(a_ref[0], wo_ref[0:na, :]) + _dot(b_ref[0], wo_ref[na:, :])
    x1 = x_ref[0] + g1 * mixed
    h = _norm_mod(x1, gain_ref[...], shift, scale).astype(BF16)
    half = hid // 2
    for s in range(0, hid, half):
        gate = _dot(h, wi_ref[:, s:s + half])
        up = _dot(h, wi_ref[:, hid + s:hid + s + half])
        act_ref[:, s:s + half] = (_silu(gate) * up).astype(BF16)
    o_ref[0] = x1 + g2 * _dot(act_ref[...], wf_ref[...])


def _mix_ffn(x, a, bmix, modv, gain, w_out, w_in, w_ffn_out):
    b, l, d = x.shape
    tm = min(ROW_TILE, l)
    hid = w_ffn_out.shape[0]
    row = lambda c: pl.BlockSpec((1, tm, c), lambda bi, i: (bi, i, 0))
    return pl.pallas_call(
        _mixffn_kernel,
        grid=(b, l // tm),
        in_specs=[row(d), row(a.shape[2]), row(bmix.shape[2]),
                  pl.BlockSpec((1, 1, modv.shape[2]), lambda bi, i: (bi, 0, 0)),
                  _resident(gain.shape), _resident(w_out.shape), _resident(w_in.shape), _resident(w_ffn_out.shape)],
        out_specs=row(d),
        out_shape=jax.ShapeDtypeStruct(x.shape, F32),
        scratch_shapes=[pltpu.VMEM((tm, hid), BF16)],
        compiler_params=_params("parallel", "parallel"),
        name="mix_ffn",
    )(x, a, bmix, modv, gain, w_out, w_in, w_ffn_out)


def _rope(x, cos, sin):
    lane = lax.broadcasted_iota(jnp.int32, x.shape, 1)
    partner = jnp.where(lane % 32 < 16, pltpu.roll(x, LANE - 16, 1), pltpu.roll(x, 16, 1))
    return x * cos + partner * sin


def _odin_kernel(x_ref, c_ref, mod_ref, gain_ref, w_ref, qan_ref, wuq_ref, kvan_ref, wukv_ref, dqn_ref, dkn_ref,
                 mqn_ref, mkn_ref, nm_ref, cos_ref, sin_ref, daq_ref, dak_ref, davt_ref, mq_ref, mk_ref, mvt_ref,
                 *, n_lat):
    d = x_ref.shape[2]
    xin = jnp.where(pl.program_id(1) < n_lat, x_ref[0], c_ref[0])
    h = _norm_mod(xin, gain_ref[...], mod_ref[0, 0, :, 0:d], mod_ref[0, 0, :, d:2 * d]).astype(BF16)
    cos = cos_ref[...]
    sin = sin_ref[...]
    ones = jnp.ones((V_ROWS - MLA_V, ATT_TK), BF16)

    def store_values_t(out_ref, vals):
        for hd in range(vals.shape[1] // LANE):
            for t in range(vals.shape[0] // ATT_TK):
                tile = vals[t * ATT_TK:(t + 1) * ATT_TK, hd * LANE:(hd + 1) * LANE]
                out_ref[0, hd, t, 0:MLA_V, :] = tile.T.astype(BF16)
                out_ref[0, hd, t, MLA_V:, :] = ones

    def slab_norm(u, g, kind):
        return u * lax.rsqrt(_dot(u * u, nm_ref[kind]) + EPS) * g

    pair_norm = functools.partial(slab_norm, kind=0)
    full_norm = functools.partial(slab_norm, kind=1)
    low_norm = functools.partial(slab_norm, kind=2)

    def da_part(out_ref, base, g, out_scale):
        p = _dot(h, w_ref[:, base:base + DA_QK])
        for s in range(0, DA_QK, LANE):
            u = _rope(pair_norm(p[:, s:s + LANE], g), cos, sin)
            out_ref[0, :, s:s + LANE] = (u * out_scale).astype(BF16)

    da_part(daq_ref, 0, dqn_ref[...], DA_DIM ** -0.5 * LOG2E)
    da_part(dak_ref, DA_QK, dkn_ref[...], 1.0)
    store_values_t(davt_ref, _dot(h, w_ref[:, 2 * DA_QK:2 * DA_QK + DA_VW]))

    def wide_norm(u, g):
        return u * lax.rsqrt(jnp.mean(u * u, axis=-1, keepdims=True) + EPS) * g

    c0 = 2 * DA_QK + DA_VW
    cq = wide_norm(_dot(h, w_ref[:, c0:c0 + MLA_Q_RANK]), qan_ref[...]).astype(BF16)
    qf = _dot(cq, wuq_ref[...])
    scale = (MLA_NOPE + MLA_ROPE) ** -0.5 * LOG2E
    for hd in range(MLA_HEADS):
        s = hd * MLA_QK_PAD
        mq_ref[0, :, s:s + LANE] = (full_norm(qf[:, s:s + LANE], mqn_ref[:, 0:LANE]) * scale).astype(BF16)
        u = _rope(low_norm(qf[:, s + LANE:s + 2 * LANE], mqn_ref[:, LANE:2 * LANE]), cos, sin)
        mq_ref[0, :, s + LANE:s + 2 * LANE] = (u * scale).astype(BF16)
    c0 += MLA_Q_RANK
    ckv = wide_norm(_dot(h, w_ref[:, c0:c0 + MLA_KV_RANK]), kvan_ref[...]).astype(BF16)
    kvu = _dot(ckv, wukv_ref[...])
    c0 += MLA_KV_RANK
    kr = _rope(low_norm(_dot(h, w_ref[:, c0:c0 + LANE]), mkn_ref[:, LANE:2 * LANE]), cos, sin).astype(BF16)
    for hd in range(MLA_HEADS):
        s = hd * MLA_QK_PAD
        mk_ref[0, :, s:s + LANE] = full_norm(kvu[:, hd * LANE:(hd + 1) * LANE], mkn_ref[:, 0:LANE]).astype(BF16)
        mk_ref[0, :, s + LANE:s + 2 * LANE] = kr
    store_values_t(mvt_ref, kvu[:, MLA_HEADS * MLA_NOPE:])


def _odd_in(x, ctx, mod2, gain, w_in, q_a_gain, w_uq, kv_a_gain, w_ukv, dq_gain, dk_gain, mq_gain, mk_gain,
            cos_t, sin_t):
    b, l, d = x.shape
    lc = ctx.shape[1]
    tm = min(ODD_TILE, l, lc)
    n_lat, n_ctx = l // tm, lc // tm
    lk = l + lc
    tiles = tm // ATT_TK
    row = lambda c: pl.BlockSpec((1, tm, c), lambda bi, i: (bi, i, 0))
    vt = pl.BlockSpec((1, MLA_HEADS, tiles, V_ROWS, ATT_TK), lambda bi, i: (bi, 0, i, 0, 0))
    widths = (DA_QK, DA_QK, None, MLA_HEADS * MLA_QK_PAD, MLA_HEADS * MLA_QK_PAD, None)
    table = pl.BlockSpec((tm, LANE), lambda bi, i: (i, 0))
    li = jnp.arange(LANE)
    half = (li[:, None] // DA_DIM == li[None, :] // DA_DIM).astype(F32) / DA_DIM
    low = jnp.broadcast_to((li[:, None] < MLA_ROPE).astype(F32) / MLA_ROPE, (LANE, LANE))
    norm_mats = jnp.stack([half, jnp.full((LANE, LANE), 1.0 / LANE, F32), low])
    res = [gain, w_in, q_a_gain, w_uq, kv_a_gain, w_ukv, dq_gain, dk_gain, mq_gain, mk_gain, norm_mats]
    vt_shape = jax.ShapeDtypeStruct((b, MLA_HEADS, lk // ATT_TK, V_ROWS, ATT_TK), BF16)
    return pl.pallas_call(
        functools.partial(_odin_kernel, n_lat=n_lat),
        grid=(b, n_lat + n_ctx),
        in_specs=[pl.BlockSpec((1, tm, d), lambda bi, i: (bi, jnp.minimum(i, n_lat - 1), 0)),
                  pl.BlockSpec((1, tm, d), lambda bi, i: (bi, jnp.maximum(i - n_lat, 0), 0)),
                  pl.BlockSpec((1, 1, 1, mod2.shape[3]), lambda bi, i: (bi, i // n_lat, 0, 0))]
        + [_resident(a.shape) for a in res] + [table, table],
        out_specs=[vt if c is None else row(c) for c in widths],
        out_shape=[vt_shape if c is None else jax.ShapeDtypeStruct((b, lk, c), BF16) for c in widths],
        compiler_params=_params("parallel", "parallel"),
        name="odd_in",
    )(x, ctx, mod2, *res, cos_t, sin_t)


def _flash_scratch(nchain, strip, tk, qk_width):
    half = [pltpu.VMEM((nchain, tk, strip), F32), pltpu.VMEM((nchain, 1, strip), F32)]
    return [pltpu.VMEM((nchain, 1, strip), F32), pltpu.VMEM((nchain, V_ROWS, strip), F32),
            pltpu.VMEM((nchain, qk_width, strip), BF16)] + half + half


def _flash_loop(q_ref, k_ref, vt_ref, scratch, strip, split_q):
    m_ref, acc_ref, qz_ref, s0_ref, mx0_ref, s1_ref, mx1_ref = scratch
    s_ref, mx_ref = (s0_ref, s1_ref), (mx0_ref, mx1_ref)
    tq = q_ref.shape[1]
    tk = vt_ref.shape[4]
    nk = vt_ref.shape[2]
    nchain = m_ref.shape[0]
    m_ref[...] = jnp.full(m_ref.shape, -jnp.inf, F32)
    acc_ref[...] = jnp.zeros(acc_ref.shape, F32)
    row = lax.broadcasted_iota(jnp.int32, (q_ref.shape[2], strip), 0)
    for st in range(tq // strip):
        qt = q_ref[0, pl.ds(st * strip, strip), :].astype(F32).T
        if split_q:
            qz_ref[2 * st] = jnp.where(row < DA_DIM, qt, 0.0).astype(BF16)
            qz_ref[2 * st + 1] = jnp.where(row < DA_DIM, 0.0, qt).astype(BF16)
        else:
            qz_ref[st] = qt.astype(BF16)

    def scores(j, slot):
        k = k_ref[0, j * tk:(j + 1) * tk, :]
        for c in range(nchain):
            st = _dot(k, qz_ref[c])
            s_ref[slot][c] = st
            mx_ref[slot][c] = jnp.max(st, axis=0, keepdims=True)

    def weigh(j, slot):
        vt1 = vt_ref[0, 0, j]
        alphas, pvs = [], []
        for c in range(nchain):
            m_old = m_ref[c]
            m_new = jnp.maximum(m_old, mx_ref[slot][c])
            p = jnp.exp2((s_ref[slot][c] - m_new).astype(BF16))
            alphas.append(jnp.exp2(m_old - m_new))
            m_ref[c] = m_new
            pvs.append(_dot(vt1, p))
        return alphas, pvs

    def accumulate(pending):
        for c, (alpha, pv) in enumerate(zip(*pending)):
            acc_ref[c] = alpha * acc_ref[c] + pv

    scores(0, 0)
    pending = None
    for j in range(nk):
        if j + 1 < nk:
            scores(j + 1, (j + 1) % 2)
        if pending is not None:
            accumulate(pending)
        pending = weigh(j, j % 2)
    accumulate(pending)


def _softmax_out(acc):
    return acc[0:MLA_V] / acc[MLA_V:MLA_V + 1]


def _da_kernel(q_ref, k_ref, vt_ref, lam_ref, gain_ref, o_ref, *scratch, strip, lambda_init):
    _flash_loop(q_ref, k_ref, vt_ref, scratch, strip, True)
    acc_ref = scratch[1]
    lp = lam_ref[...]
    lam = (jnp.exp(jnp.sum(lp[0:1] * lp[1:2], axis=-1, keepdims=True))
           - jnp.exp(jnp.sum(lp[2:3] * lp[3:4], axis=-1, keepdims=True)) + lambda_init)
    for st in range(q_ref.shape[1] // strip):
        o = _softmax_out(acc_ref[2 * st]) - lam * _softmax_out(acc_ref[2 * st + 1])
        o = o * lax.rsqrt(jnp.mean(o * o, axis=0, keepdims=True) + EPS) * gain_ref[...]
        o_ref[0, pl.ds(st * strip, strip), :] = (o * (1.0 - lambda_init)).T.astype(BF16)


def _diff_attention(q, k, vt, lq, lam_p, gain_col, lambda_init):
    b, lk, _ = k.shape
    tq = min(DA_TQ, lq)
    strip = min(ATT_STRIP, tq)
    qs = pl.BlockSpec((1, tq, LANE), lambda bi, h, i: (bi, i, h))
    ks = pl.BlockSpec((1, lk, LANE), lambda bi, h, i: (bi, 0, h))
    vs = pl.BlockSpec((1, 1) + vt.shape[2:], lambda bi, h, i: (bi, h, 0, 0, 0))
    nchain = 2 * (tq // strip)
    return pl.pallas_call(
        functools.partial(_da_kernel, strip=strip, lambda_init=lambda_init),
        grid=(b, DA_HEADS, lq // tq),
        in_specs=[qs, ks, vs, _resident(lam_p.shape), _resident(gain_col.shape)],
        out_specs=qs,
        out_shape=jax.ShapeDtypeStruct((b, lq, q.shape[2]), BF16),
        scratch_shapes=_flash_scratch(nchain, strip, ATT_TK, LANE),
        compiler_params=_params("parallel", "parallel", "parallel"),
        name="diff_attention",
    )(q, k, vt, lam_p, gain_col)


def _mla_kernel(q_ref, k_ref, vt_ref, o_ref, *scratch, strip):
    _flash_loop(q_ref, k_ref, vt_ref, scratch, strip, False)
    acc_ref = scratch[1]
    for st in range(q_ref.shape[1] // strip):
        o_ref[0, pl.ds(st * strip, strip), :] = _softmax_out(acc_ref[st]).T.astype(BF16)


def _mla_attention(q, k, vt, lq):
    b, lk, _ = k.shape
    tq = min(MLA_TQ, lq)
    strip = min(ATT_STRIP, tq)
    qs = pl.BlockSpec((1, tq, MLA_QK_PAD), lambda bi, h, i: (bi, i, h))
    ks = pl.BlockSpec((1, lk, MLA_QK_PAD), lambda bi, h, i: (bi, 0, h))
    vs = pl.BlockSpec((1, 1) + vt.shape[2:], lambda bi, h, i: (bi, h, 0, 0, 0))
    nchain = tq // strip
    return pl.pallas_call(
        functools.partial(_mla_kernel, strip=strip),
        grid=(b, MLA_HEADS, lq // tq),
        in_specs=[qs, ks, vs],
        out_specs=pl.BlockSpec((1, tq, MLA_V), lambda bi, h, i: (bi, i, h)),
        out_shape=jax.ShapeDtypeStruct((b, lq, MLA_HEADS * MLA_V), BF16),
        scratch_shapes=_flash_scratch(nchain, strip, ATT_TK, MLA_QK_PAD),
        compiler_params=_params("parallel", "parallel", "parallel"),
        name="mla_attention",
    )(q, k, vt)


def _scan_rows(gates_t, index, chunks):
    b, _, l = gates_t.shape
    idx = jnp.asarray(index, jnp.int32)
    rows = gates_t[:, idx, :]
    rows = rows.reshape(b, idx.shape[0], idx.shape[1], chunks, l // chunks)
    return jnp.swapaxes(rows, 2, 3)


def _even_layer(x, ctx, mod_x, mod_c, p):
    b = x.shape[0]
    nh, ns = DN_HEADS, SSD_HEADS
    dn_index = [[d * nh + h for h in range(nh)] + [2 * nh + d * nh + h for h in range(nh)] for d in range(2)]
    dt0, a0 = 4 * nh, 4 * nh + 2 * ns
    ssd_index = [[dt0 + g * SSD_GRP + r for r in range(SSD_GRP)] + [a0 + g * SSD_GRP + r for r in range(SSD_GRP)]
                 + [dt0 + ns + g * SSD_GRP + r for r in range(SSD_GRP)] + [a0 + ns + g * SSD_GRP + r for r in range(SSD_GRP)]
                 for g in range(SSD_GROUPS)]

    def prepare(t, modv):
        q, k, v, xs, bc, z, gates = _even_in(t, modv, p["gain_mix"], p["w_main"], p["w_small"], p["conv_w"], p["conv_b"],
                                             p["gate_bias"], p["gate_scale"])
        gt = jnp.swapaxes(gates, 1, 2)
        nchunk = t.shape[1] // CHUNK
        return dict(q=q, k=k, v=v, xs=xs, bc=bc, z=z, dn_rows=jnp.swapaxes(_scan_rows(gt, dn_index, nchunk), 1, 2),
                    ssd_rows=jnp.swapaxes(_scan_rows(gt, ssd_index, nchunk), 1, 2))

    pc = prepare(ctx, mod_c)
    pL = prepare(x, mod_x)
    dn0 = jnp.zeros((b, 2 * nh, DN_DK, DN_DV), F32)
    ssd0 = jnp.zeros((b, SSD_GROUPS, 2, SSD_STATE, SSD_GRP * SSD_HEAD_DIM), F32)
    ocf, ocb, dn1 = _dn_scan(pc["q"], pc["k"], pc["v"], pc["dn_rows"], dn0)
    olf, olb, _ = _dn_scan(pL["q"], pL["k"], pL["v"], pL["dn_rows"], dn1)
    ycf, ycb, ssd1 = _ssd_scan(pc["xs"], pc["bc"], pc["ssd_rows"], ssd0, p["expand"])
    ylf, ylb, _ = _ssd_scan(pL["xs"], pL["bc"], pL["ssd_rows"], ssd1, p["expand"])

    def finish(t, modv, pp, of, ob, yf, yb):
        a, bm = _even_merge(of, ob, yf, yb, pp["xs"], pp["z"], p["dn_gain"], p["ssd_d"], p["ssd_gain"])
        return _mix_ffn(t, a, bm, modv, p["gain_ffn"], p["w_out"], p["ffn_w_in"], p["ffn_w_out"])

    return finish(x, mod_x, pL, olf, olb, ylf, ylb), finish(ctx, mod_c, pc, ocf, ocb, ycf, ycb)


def _rope_tables(n_tokens):
    lane = jnp.arange(LANE)
    quarter = (lane % 64) // 16
    n_freq = DA_DIM // 4
    inv_freq = ROPE_THETA ** (-(lane % 16).astype(F32) / n_freq)
    tok = jnp.arange(n_tokens)
    pos = jnp.where(quarter[None, :] < 2, (tok // GRID_W)[:, None], (tok % GRID_W)[:, None]).astype(F32)
    ang = pos * inv_freq[None, :]
    sign = jnp.where(quarter % 2 == 0, -1.0, 1.0).astype(F32)
    return jnp.cos(ang), jnp.sin(ang) * sign[None, :]


def _odd_layer(x, ctx, mod_x, mod_c, p, lambda_init):
    lq, lc = x.shape[1], ctx.shape[1]
    cos_t, sin_t = _rope_tables(lq)
    cos_t = jnp.concatenate([cos_t, jnp.ones((lc, LANE), F32)], axis=0)
    sin_t = jnp.concatenate([sin_t, jnp.zeros((lc, LANE), F32)], axis=0)
    mod2 = jnp.stack([mod_x, mod_c], axis=1)
    daq, dak, davt, mq, mk, mvt = _odd_in(x, ctx, mod2, p["gain_mix"], p["w_in"], p["q_a_gain"], p["w_uq"], p["kv_a_gain"],
                                          p["w_ukv"], p["dq_gain"], p["dk_gain"], p["mq_gain"], p["mk_gain"], cos_t, sin_t)
    da = _diff_attention(daq, dak, davt, lq, p["da_lambda"], p["sub_gain"], lambda_init)
    ml = _mla_attention(mq, mk, mvt, lq)
    return _mix_ffn(x, da, ml, mod_x, p["gain_ffn"], p["w_out"], p["ffn_w_in"], p["ffn_w_out"])


def _even_params(i, j, norm_mix, norm_ffn, ffn_w_in, ffn_w_out, ev_w_in, ev_conv_w, ev_conv_b, dn_a_log, dn_dt_bias,
                 dn_norm, ssd_a_log, ssd_dt_bias, ssd_d, ssd_norm, ev_w_out):
    d = norm_mix.shape[1]
    w = ev_w_in[j]
    small = w[:, EV_CONV + EV_GATE:]
    nh, ns = DN_HEADS, SSD_HEADS
    w_small = jnp.concatenate([small, small[:, 4 * nh:], jnp.zeros((d, LANE - 4 * nh - 4 * ns), F32)], axis=1)
    zeros = lambda n: jnp.zeros((n,), F32)
    gate_bias = jnp.concatenate([zeros(2 * nh), dn_dt_bias[j].reshape(-1), ssd_dt_bias[j].reshape(-1),
                                 ssd_dt_bias[j].reshape(-1), zeros(LANE - 4 * nh - 4 * ns)])
    gate_scale = jnp.concatenate([jnp.ones((2 * nh,), F32), -jnp.exp(dn_a_log[j].reshape(-1)), jnp.ones((2 * ns,), F32),
                                  -jnp.exp(ssd_a_log[j].reshape(-1)), zeros(LANE - 4 * nh - 4 * ns)])
    expand = jnp.repeat(jnp.eye(SSD_GRP, dtype=F32), SSD_HEAD_DIM, axis=1)
    return dict(
        gain_mix=norm_mix[i].reshape(1, d), gain_ffn=norm_ffn[i].reshape(1, d),
        w_main=w[:, :EV_CONV + EV_GATE].astype(BF16), w_small=w_small,
        conv_w=ev_conv_w[j], conv_b=ev_conv_b[j].reshape(1, -1),
        gate_bias=gate_bias.reshape(1, LANE), gate_scale=gate_scale.reshape(1, LANE), expand=expand,
        dn_gain=dn_norm[j].reshape(1, DN_DV), ssd_d=jnp.repeat(ssd_d[j], SSD_HEAD_DIM).reshape(1, SSD_INNER),
        ssd_gain=ssd_norm[j].reshape(1, SSD_INNER), w_out=ev_w_out[j].astype(BF16),
        ffn_w_in=ffn_w_in[i].astype(BF16), ffn_w_out=ffn_w_out[i].astype(BF16))


def _odd_params(i, j, norm_mix, norm_ffn, ffn_w_in, ffn_w_out, od_w_in, da_q_norm, da_k_norm, da_lambda, da_sub_norm,
                mla_q_a_norm, mla_w_uq, mla_kv_a_norm, mla_w_ukv, mla_q_norm, mla_k_norm, od_w_out):
    d = norm_mix.shape[1]
    w = od_w_in[j]
    w_in = jnp.concatenate([w, jnp.zeros((d, LANE - MLA_ROPE), F32)], axis=1).astype(BF16)
    hq = mla_w_uq[j].reshape(MLA_Q_RANK, MLA_HEADS, MLA_NOPE + MLA_ROPE)
    hq = jnp.pad(hq, ((0, 0), (0, 0), (0, MLA_QK_PAD - MLA_NOPE - MLA_ROPE)))
    hkv = mla_w_ukv[j].reshape(MLA_KV_RANK, MLA_HEADS, MLA_NOPE + MLA_V)
    w_ukv = jnp.concatenate([hkv[:, :, :MLA_NOPE].reshape(MLA_KV_RANK, -1), hkv[:, :, MLA_NOPE:].reshape(MLA_KV_RANK, -1)], axis=1)
    pad_gain = lambda g: jnp.pad(g, (0, MLA_QK_PAD - MLA_NOPE - MLA_ROPE)).reshape(1, MLA_QK_PAD)
    return dict(
        gain_mix=norm_mix[i].reshape(1, d), gain_ffn=norm_ffn[i].reshape(1, d), w_in=w_in,
        q_a_gain=mla_q_a_norm[j].reshape(1, -1), w_uq=hq.reshape(MLA_Q_RANK, -1).astype(BF16),
        kv_a_gain=mla_kv_a_norm[j].reshape(1, -1), w_ukv=w_ukv.astype(BF16),
        dq_gain=jnp.tile(da_q_norm[j], 2).reshape(1, LANE), dk_gain=jnp.tile(da_k_norm[j], 2).reshape(1, LANE),
        mq_gain=pad_gain(mla_q_norm[j]), mk_gain=pad_gain(mla_k_norm[j]),
        da_lambda=jnp.pad(da_lambda[j], ((0, 4), (0, LANE - DA_DIM))), sub_gain=da_sub_norm[j].reshape(2 * DA_DIM, 1),
        w_out=od_w_out[j].astype(BF16), ffn_w_in=ffn_w_in[i].astype(BF16), ffn_w_out=ffn_w_out[i].astype(BF16))


def kernel(x, c, ctx, c_ctx, ada_w, ada_b, norm_mix, norm_ffn, ffn_w_in, ffn_w_out, ev_w_in, ev_conv_w, ev_conv_b, dn_a_log, dn_dt_bias, dn_norm, ssd_a_log, ssd_dt_bias, ssd_d, ssd_norm, ev_w_out, od_w_in, da_q_norm, da_k_norm, da_lambda, da_sub_norm, mla_q_a_norm, mla_w_uq, mla_kv_a_norm, mla_w_ukv, mla_q_norm, mla_k_norm, od_w_out):
    b, _, d = x.shape
    depth = ada_w.shape[0]
    assert b < 8
    cvec = jnp.concatenate([c, c_ctx[None, :], jnp.zeros((8 - b - 1, d), F32)], axis=0)
    mod = _modulation(cvec, ada_w, ada_b)
    for i in range(depth):
        last = i == depth - 1
        j = i // 2
        mod_x = mod[i, :b].reshape(b, 1, 6 * d)
        mod_c = jnp.broadcast_to(mod[i, b].reshape(1, 1, 6 * d), (b, 1, 6 * d))
        if i % 2 == 0:
            p = _even_params(i, j, norm_mix, norm_ffn, ffn_w_in, ffn_w_out, ev_w_in, ev_conv_w, ev_conv_b, dn_a_log,
                             dn_dt_bias, dn_norm, ssd_a_log, ssd_dt_bias, ssd_d, ssd_norm, ev_w_out)
            x, ctx_new = _even_layer(x, ctx, mod_x, mod_c, p)
        else:
            p = _odd_params(i, j, norm_mix, norm_ffn, ffn_w_in, ffn_w_out, od_w_in, da_q_norm, da_k_norm, da_lambda,
                            da_sub_norm, mla_q_a_norm, mla_w_uq, mla_kv_a_norm, mla_w_ukv, mla_q_norm, mla_k_norm, od_w_out)
            lambda_init = 0.8 - 0.6 * math.exp(-0.3 * i)
            if last:
                x = _odd_layer(x, ctx, mod_x, mod_c, p, lambda_init)
                ctx_new = ctx
            else:
                raise NotImplementedError("context update after an attention layer is not needed for depth 2")
        ctx = ctx_new
    return x
```

```python
import functools
import math

import jax
import jax.numpy as jnp
from jax import lax
from jax.experimental import pallas as pl
from jax.experimental.pallas import tpu as pltpu

F32 = jnp.float32
BF16 = jnp.bfloat16

GRID_W = 64
EPS = 1e-6
ROPE_THETA = 10000.0
CONV_K = 5
DN_HEADS = 8
DN_DK = 128
DN_DV = 128
SSD_HEADS = 16
SSD_HEAD_DIM = 64
SSD_GROUPS = 2
SSD_STATE = 128
SSD_GRP = SSD_HEADS // SSD_GROUPS
SSD_INNER = SSD_HEADS * SSD_HEAD_DIM
DA_HEADS = 8
DA_DIM = 64
MLA_HEADS = 8
MLA_Q_RANK = 512
MLA_KV_RANK = 256
MLA_NOPE = 128
MLA_ROPE = 64
MLA_V = 128
MLA_QK_PAD = 256

DN_QK = DN_HEADS * DN_DK
DN_VW = DN_HEADS * DN_DV
SSD_BC = SSD_GROUPS * SSD_STATE
EV_CONV = 2 * DN_QK + DN_VW + SSD_INNER + 2 * SSD_BC
EV_GATE = DN_VW + SSD_INNER
DA_QK = DA_HEADS * 2 * DA_DIM
DA_VW = DA_HEADS * 2 * DA_DIM

LANE = 128
BF16_ROWS = 16
CHUNK = 128
VMEM_LIMIT = 56 * 1024 * 1024

ROW_TILE = 512
ODD_TILE = 256
SCAN_TILE = 512
DN_TILE = 256
DA_TQ = 512
MLA_TQ = 1024
ATT_TK = 256
ATT_STRIP = 256
V_ROWS = MLA_V + BF16_ROWS
LOG2E = 1.4426950408889634


def _dot(a, b):
    return jnp.dot(a, b, preferred_element_type=F32)


def _dot_nt(a, b):
    return lax.dot_general(a, b, (((1,), (1,)), ((), ())), preferred_element_type=F32)


def _dot_tn(a, b):
    return lax.dot_general(a, b, (((0,), (0,)), ((), ())), preferred_element_type=F32)


def _sigmoid(x):
    return 1.0 / (1.0 + jnp.exp(-x))


def _silu(x):
    return x * _sigmoid(x)


def _softplus(x):
    return jnp.maximum(x, 0.0) + jnp.log(1.0 + jnp.exp(-jnp.abs(x)))


def _params(*sem):
    return pltpu.CompilerParams(dimension_semantics=sem, vmem_limit_bytes=VMEM_LIMIT)


def _resident(shape):
    nd = len(shape)
    return pl.BlockSpec(shape, lambda *_: (0,) * nd, pipeline_mode=pl.Buffered(1))


def _norm_mod(x, gain, shift, scale):
    y = x * lax.rsqrt(jnp.mean(x * x, axis=-1, keepdims=True) + EPS)
    return (y * gain) * (1.0 + scale) + shift


def _mod_kernel(c_ref, w_ref, b_ref, o_ref):
    o_ref[0] = _dot(_silu(c_ref[...]), w_ref[0]) + b_ref[0]


def _modulation(cvec, ada_w, ada_b):
    depth, d, n = ada_w.shape
    tn = n // 4
    return pl.pallas_call(
        _mod_kernel,
        grid=(depth, n // tn),
        in_specs=[
            pl.BlockSpec((8, d), lambda l, j: (0, 0)),
            pl.BlockSpec((1, d, tn), lambda l, j: (l, 0, j)),
            pl.BlockSpec((1, 1, tn), lambda l, j: (l, 0, j)),
        ],
        out_specs=pl.BlockSpec((1, 8, tn), lambda l, j: (l, 0, j)),
        out_shape=jax.ShapeDtypeStruct((depth, 8, n), F32),
        compiler_params=_params("parallel", "parallel"),
        name="modulation",
    )(cvec, ada_w, ada_b.reshape(depth, 1, n))


def _evin_kernel(xp_ref, xc_ref, xn_ref, mod_ref, gain_ref, w_ref, wsm_ref, cw_ref, cb_ref, gb_ref, gs_ref,
                 q_ref, k_ref, v_ref, xs_ref, bc_ref, z_ref, g_ref, hext_ref, pext_ref):
    i = pl.program_id(1)
    nb = pl.num_programs(1)
    tm = xc_ref.shape[1]
    d = xc_ref.shape[2]
    halo = BF16_ROWS
    gain = gain_ref[...]
    shift = mod_ref[0, :, 0:d]
    scale = mod_ref[0, :, d:2 * d]

    hc = _norm_mod(xc_ref[0], gain, shift, scale)
    hp = _norm_mod(xp_ref[0], gain, shift, scale)
    hn = _norm_mod(xn_ref[0], gain, shift, scale)
    hext_ref[0:halo, :] = jnp.where(i > 0, hp, 0.0).astype(BF16)
    hext_ref[halo:halo + tm, :] = hc.astype(BF16)
    hext_ref[halo + tm:, :] = jnp.where(i < nb - 1, hn, 0.0).astype(BF16)

    gw = pext_ref.shape[1]
    pad = CONV_K // 2
    plan = ((q_ref, 0, DN_QK, DN_DK ** -0.5), (k_ref, DN_QK, DN_QK, 1.0), (v_ref, 2 * DN_QK, DN_VW, None),
            (xs_ref, 2 * DN_QK + DN_VW, SSD_INNER, None), (bc_ref, 2 * DN_QK + DN_VW + SSD_INNER, 2 * SSD_BC, None))
    for out_ref, base, width, l2_scale in plan:
        for off in range(0, width, gw):
            c0 = base + off
            pext_ref[...] = _dot(hext_ref[...], w_ref[:, c0:c0 + gw])
            acc = cb_ref[:, c0:c0 + gw] + cw_ref[0:1, c0:c0 + gw] * pext_ref[halo - pad:halo - pad + tm, :]
            for t in range(1, CONV_K):
                acc = acc + cw_ref[t:t + 1, c0:c0 + gw] * pext_ref[halo - pad + t:halo - pad + t + tm, :]
            u = _silu(acc)
            if l2_scale is None:
                out_ref[0, :, off:off + gw] = u
            else:
                for s in range(0, gw, LANE):
                    uh = u[:, s:s + LANE]
                    inv = lax.rsqrt(jnp.sum(uh * uh, axis=-1, keepdims=True) + EPS)
                    out_ref[0, :, off + s:off + s + LANE] = uh * (inv * l2_scale)

    hcb = hext_ref[halo:halo + tm, :]
    for off in range(0, EV_GATE, gw):
        z_ref[0, :, off:off + gw] = _dot(hcb, w_ref[:, EV_CONV + off:EV_CONV + off + gw]).astype(z_ref.dtype)

    p = _dot(hc, wsm_ref[...]) + gb_ref[...]
    lane = lax.broadcasted_iota(jnp.int32, p.shape, 1)
    g_ref[0] = jnp.where(lane < 2 * DN_HEADS, _sigmoid(p), _softplus(p)) * gs_ref[...]


def _even_in(x, modv, gain, w_main, w_small, conv_w, conv_b, gate_bias, gate_scale):
    b, l, d = x.shape
    tm = min(ROW_TILE, l)
    nb = l // tm
    hb = tm // BF16_ROWS
    gw = 512
    row = lambda c: pl.BlockSpec((1, tm, c), lambda bi, i: (bi, i, 0))
    outs = (DN_QK, DN_QK, DN_VW, SSD_INNER, 2 * SSD_BC, EV_GATE, LANE)
    return pl.pallas_call(
        _evin_kernel,
        grid=(b, nb),
        in_specs=[
            pl.BlockSpec((1, BF16_ROWS, d), lambda bi, i: (bi, jnp.maximum(i * hb - 1, 0), 0)),
            row(d),
            pl.BlockSpec((1, BF16_ROWS, d), lambda bi, i: (bi, jnp.minimum((i + 1) * hb, l // BF16_ROWS - 1), 0)),
            pl.BlockSpec((1, 1, modv.shape[2]), lambda bi, i: (bi, 0, 0)),
            _resident(gain.shape), _resident(w_main.shape), _resident(w_small.shape), _resident(conv_w.shape),
            _resident(conv_b.shape), _resident(gate_bias.shape), _resident(gate_scale.shape),
        ],
        out_specs=[row(c) for c in outs],
        out_shape=[jax.ShapeDtypeStruct((b, l, c), BF16 if c == EV_GATE else F32) for c in outs],
        scratch_shapes=[pltpu.VMEM((tm + 2 * BF16_ROWS, d), BF16), pltpu.VMEM((tm + 2 * BF16_ROWS, gw), F32)],
        compiler_params=_params("parallel", "parallel"),
        name="even_in",
    )(x, x, x, modv, gain, w_main, w_small, conv_w, conv_b, gate_bias, gate_scale)


def _tri_masks(c):
    r = lax.broadcasted_iota(jnp.int32, (c, c), 0)
    s = lax.broadcasted_iota(jnp.int32, (c, c), 1)
    return r, s


TRI_BASE = 16


def _block_masks(r, s):
    same = lambda n: (r // n) == (s // n)
    masks = [same(TRI_BASE).astype(F32)]
    n = TRI_BASE
    while n < CHUNK:
        n *= 2
        masks.append(jnp.where(same(n), 1.0, 0.0) - jnp.where(same(n // 2), 1.0, 0.0))
    return masks


def _unit_tri_inverses(a, eye, blocks):
    p = [-(a_n * blocks[0]) for a_n in a]
    x = [eye + p_n for p_n in p]
    for _ in range(int(math.log2(TRI_BASE)) - 1):
        p = [_dot(p_n, p_n) for p_n in p]
        x = [x_n + _dot(x_n, p_n) for x_n, p_n in zip(x, p)]
    for m in blocks[1:]:
        e = [_dot(a_n * m, x_n) for a_n, x_n in zip(a, x)]
        x = [x_n - _dot(x_n, e_n) for x_n, e_n in zip(x, e)]
    return x


def _dn_chunks(q, k, v, gc, beta, crow, total, state, incl, strict, eye, blocks):
    n = range(len(q))
    c = q[0].shape[0]
    decay = [jnp.where(incl[i], jnp.exp(gc[i] - crow[i]), 0.0) for i in n]
    kb = [k[i] * beta[i] for i in n]
    a = [jnp.where(strict[i], _dot_nt(kb[i], k[i]) * decay[i], 0.0) for i in n]
    t = _unit_tri_inverses(a, eye, blocks)
    egc = [jnp.exp(gc[i]) for i in n]
    uw = [_dot(t[i], jnp.concatenate([v[i] * beta[i], kb[i] * egc[i]], axis=1)) for i in n]
    ws_qs = [_dot(jnp.concatenate([uw[i][:, DN_DV:], q[i] * egc[i]], axis=0), state[i]) for i in n]
    v_new = [uw[i][:, 0:DN_DV] - ws_qs[i][0:c] for i in n]
    attn = [_dot_nt(q[i], k[i]) * decay[i] for i in n]
    o = [ws_qs[i][c:2 * c] + _dot(attn[i], v_new[i]) for i in n]
    state = [state[i] * jnp.exp(total[i]) + _dot_tn(k[i] * jnp.exp(total[i] - gc[i]), v_new[i]) for i in n]
    return o, state


def _dn_kernel(qf_ref, kf_ref, vf_ref, gf_ref, qb_ref, kb_ref, vb_ref, gb_ref, s0_ref,
               of_ref, ob_ref, sfin_ref, s_ref):
    i = pl.program_id(1)
    nck = gf_ref.shape[1]
    c = CHUNK
    nh = DN_HEADS

    @pl.when(i == 0)
    def _():
        s_ref[...] = s0_ref[0]

    r, s = _tri_masks(c)
    eye = (r == s).astype(F32)
    blocks = _block_masks(r, s)
    dirs = ((qf_ref, kf_ref, vf_ref, gf_ref, of_ref, r >= s, r > s, c - 1),
            (qb_ref, kb_ref, vb_ref, gb_ref, ob_ref, r <= s, r < s, 0))

    def body(ci, carry):
        q, k, v, gc, beta, crow, total, state, incl, strict, dest = ([] for _ in range(11))
        for d, (q_ref, k_ref, v_ref, g_ref, o_ref, inc, stri, tot_row) in enumerate(dirs):
            cc = ci if d == 0 else nck - 1 - ci
            rows = pl.ds(pl.multiple_of(cc * c, c), c)
            inclf = inc.astype(F32)
            grow = g_ref[0, cc, d]
            cum_rows = _dot_nt(grow, inclf)
            cols = _dot_nt(jnp.concatenate([inclf, eye], axis=0), grow)
            for h in range(nh):
                lanes = slice(h * LANE, (h + 1) * LANE)
                q.append(q_ref[0, rows, lanes])
                k.append(k_ref[0, rows, lanes])
                v.append(v_ref[0, rows, lanes])
                gc.append(cols[0:c, nh + h:nh + h + 1])
                beta.append(cols[c:2 * c, h:h + 1])
                crow.append(cum_rows[nh + h:nh + h + 1, :])
                total.append(cols[tot_row:tot_row + 1, nh + h:nh + h + 1])
                state.append(s_ref[d * nh + h])
                incl.append(inc)
                strict.append(stri)
                dest.append((o_ref, rows, lanes))
        o, state = _dn_chunks(q, k, v, gc, beta, crow, total, state, incl, strict, eye, blocks)
        for n, (o_ref, rows, lanes) in enumerate(dest):
            o_ref[0, rows, lanes] = o[n].astype(o_ref.dtype)
            s_ref[n] = state[n]
        return carry

    lax.fori_loop(0, nck, body, 0)

    @pl.when(i == pl.num_programs(1) - 1)
    def _():
        sfin_ref[0] = s_ref[...]


def _dn_scan(q, k, v, grows, state0):
    b, l, width = q.shape
    tl = min(DN_TILE, l)
    nb = l // tl
    nck = tl // CHUNK
    fwd = pl.BlockSpec((1, tl, width), lambda bi, i: (bi, i, 0))
    bwd = pl.BlockSpec((1, tl, width), lambda bi, i: (bi, nb - 1 - i, 0))
    gfwd = pl.BlockSpec((1, nck) + grows.shape[2:], lambda bi, i: (bi, i, 0, 0, 0))
    gbwd = pl.BlockSpec((1, nck) + grows.shape[2:], lambda bi, i: (bi, nb - 1 - i, 0, 0, 0))
    st = pl.BlockSpec((1,) + state0.shape[1:], lambda bi, i: (bi, 0, 0, 0))
    return pl.pallas_call(
        _dn_kernel,
        grid=(b, nb),
        in_specs=[fwd, fwd, fwd, gfwd, bwd, bwd, bwd, gbwd, st],
        out_specs=[fwd, bwd, st],
        out_shape=[jax.ShapeDtypeStruct(q.shape, BF16), jax.ShapeDtypeStruct(q.shape, BF16),
                   jax.ShapeDtypeStruct(state0.shape, F32)],
        scratch_shapes=[pltpu.VMEM(state0.shape[1:], F32)],
        compiler_params=_params("parallel", "arbitrary"),
        name="deltanet_scan",
    )(q, k, v, grows, q, k, v, grows, state0)


def _ssd_chunks(x, bm, cm, rows, state, expand, incl, eye, first_row):
    n = range(len(x))
    c = x[0].shape[0]
    nh = SSD_GRP
    p = SSD_HEAD_DIM
    inclf = [incl[i].astype(F32) for i in n]
    cum_row = [_dot_nt(rows[i][nh:2 * nh], inclf[i]) for i in n]
    cols = [_dot_nt(jnp.concatenate([inclf[i], eye], axis=0), rows[i]) for i in n]
    cum = [cols[i][0:c, nh:2 * nh] for i in n]
    total = [cum[i][first_row[i]:first_row[i] + 1, :] for i in n]
    xdt = [x[i] * _dot(cols[i][c:2 * c, 0:nh], expand) for i in n]
    cb = [_dot_nt(cm[i], bm[i]) for i in n]
    from_state = [_dot(cm[i], state[i]) * _dot(jnp.exp(cum[i]), expand) for i in n]
    parts = [[] for _ in n]
    for h in range(nh):
        seg = [jnp.where(incl[i], jnp.exp(cum[i][:, h:h + 1] - cum_row[i][h:h + 1, :]), 0.0) for i in n]
        for i in n:
            parts[i].append(_dot(cb[i] * seg[i], xdt[i][:, h * p:(h + 1) * p]))
    y = [jnp.concatenate(parts[i], axis=1) + from_state[i] for i in n]
    carry_in = [xdt[i] * _dot(jnp.exp(total[i] - cum[i]), expand) for i in n]
    state = [state[i] * _dot(jnp.exp(total[i]), expand) + _dot_tn(bm[i], carry_in[i]) for i in n]
    return y, state


def _ssd_kernel(xf_ref, bcf_ref, gf_ref, xb_ref, bcb_ref, gb_ref, s0_ref, e_ref, yf_ref, yb_ref, sfin_ref, s_ref):
    i = pl.program_id(1)
    nck = gf_ref.shape[1]
    c = CHUNK
    nh = SSD_GRP
    ng = SSD_GROUPS
    gwid = nh * SSD_HEAD_DIM

    @pl.when(i == 0)
    def _():
        s_ref[...] = s0_ref[0]

    r, s = _tri_masks(c)
    eye = (r == s).astype(F32)
    expand = e_ref[...]
    dirs = ((xf_ref, bcf_ref, gf_ref, yf_ref, r >= s, c - 1), (xb_ref, bcb_ref, gb_ref, yb_ref, r <= s, 0))

    def body(ci, carry):
        x, bm, cm, rows, state, incl, first, dest = ([] for _ in range(8))
        for d, (x_ref, bc_ref, g_ref, y_ref, inc, tot_row) in enumerate(dirs):
            cc = ci if d == 0 else nck - 1 - ci
            tok = pl.ds(pl.multiple_of(cc * c, c), c)
            for g in range(ng):
                lanes = slice(g * gwid, (g + 1) * gwid)
                x.append(x_ref[0, tok, lanes])
                bm.append(bc_ref[0, tok, g * SSD_STATE:(g + 1) * SSD_STATE])
                cm.append(bc_ref[0, tok, (ng + g) * SSD_STATE:(ng + g + 1) * SSD_STATE])
                rows.append(g_ref[0, cc, g][2 * nh * d:2 * nh * (d + 1)])
                state.append(s_ref[g, d])
                incl.append(inc)
                first.append(tot_row)
                dest.append((y_ref, tok, lanes, g, d))
        y, state = _ssd_chunks(x, bm, cm, rows, state, expand, incl, eye, first)
        for n, (y_ref, tok, lanes, g, d) in enumerate(dest):
            y_ref[0, tok, lanes] = y[n].astype(y_ref.dtype)
            s_ref[g, d] = state[n]
        return carry

    lax.fori_loop(0, nck, body, 0)

    @pl.when(i == pl.num_programs(1) - 1)
    def _():
        sfin_ref[0] = s_ref[...]


def _ssd_scan(xs, bc, grows, state0, expand):
    b, l, width = xs.shape
    tl = min(SCAN_TILE, l)
    nb = l // tl
    nck = tl // CHUNK
    fwd = lambda w: pl.BlockSpec((1, tl, w), lambda bi, i: (bi, i, 0))
    bwd = lambda w: pl.BlockSpec((1, tl, w), lambda bi, i: (bi, nb - 1 - i, 0))
    gf = pl.BlockSpec((1, nck) + grows.shape[2:], lambda bi, i: (bi, i, 0, 0, 0))
    gb = pl.BlockSpec((1, nck) + grows.shape[2:], lambda bi, i: (bi, nb - 1 - i, 0, 0, 0))
    st = pl.BlockSpec((1,) + state0.shape[1:], lambda bi, i: (bi, 0, 0, 0, 0))
    return pl.pallas_call(
        _ssd_kernel,
        grid=(b, nb),
        in_specs=[fwd(width), fwd(bc.shape[2]), gf, bwd(width), bwd(bc.shape[2]), gb, st, _resident(expand.shape)],
        out_specs=[fwd(width), bwd(width), st],
        out_shape=[jax.ShapeDtypeStruct(xs.shape, BF16), jax.ShapeDtypeStruct(xs.shape, BF16),
                   jax.ShapeDtypeStruct(state0.shape, F32)],
        scratch_shapes=[pltpu.VMEM(state0.shape[1:], F32)],
        compiler_params=_params("parallel", "arbitrary"),
        name="ssd_scan",
    )(xs, bc, grows, xs, bc, grows, state0, expand)


def _evmerge_kernel(of_ref, ob_ref, yf_ref, yb_ref, xs_ref, z_ref, dng_ref, sd_ref, sg_ref, a_ref, b_ref):
    f32 = lambda ref, lo, width: ref[0, :, lo:lo + width].astype(F32)
    for s in range(0, DN_VW, LANE):
        o = f32(of_ref, s, LANE) + f32(ob_ref, s, LANE)
        o = o * lax.rsqrt(jnp.mean(o * o, axis=-1, keepdims=True) + EPS) * dng_ref[...]
        a_ref[0, :, s:s + LANE] = (o * _silu(f32(z_ref, s, LANE))).astype(BF16)
    gwid = SSD_INNER // SSD_GROUPS
    for s in range(0, SSD_INNER, gwid):
        y = f32(yf_ref, s, gwid) + f32(yb_ref, s, gwid) + sd_ref[:, s:s + gwid] * xs_ref[0, :, s:s + gwid]
        y = y * _silu(f32(z_ref, DN_VW + s, gwid))
        y = y * lax.rsqrt(jnp.mean(y * y, axis=-1, keepdims=True) + EPS) * sg_ref[:, s:s + gwid]
        b_ref[0, :, s:s + gwid] = y.astype(BF16)


def _even_merge(o_f, o_b, y_f, y_b, xs, z, dn_gain, ssd_d, ssd_gain):
    b, l, _ = o_f.shape
    tm = min(ROW_TILE, l)
    row = lambda c: pl.BlockSpec((1, tm, c), lambda bi, i: (bi, i, 0))
    return pl.pallas_call(
        _evmerge_kernel,
        grid=(b, l // tm),
        in_specs=[row(DN_VW), row(DN_VW), row(SSD_INNER), row(SSD_INNER), row(SSD_INNER), row(EV_GATE),
                  _resident(dn_gain.shape), _resident(ssd_d.shape), _resident(ssd_gain.shape)],
        out_specs=[row(DN_VW), row(SSD_INNER)],
        out_shape=[jax.ShapeDtypeStruct((b, l, DN_VW), BF16), jax.ShapeDtypeStruct((b, l, SSD_INNER), BF16)],
        compiler_params=_params("parallel", "parallel"),
        name="even_merge",
    )(o_f, o_b, y_f, y_b, xs, z, dn_gain, ssd_d, ssd_gain)


def _mixffn_kernel(x_ref, a_ref, b_ref, mod_ref, gain_ref, wo_ref, wi_ref, wf_ref, o_ref, act_ref):
    d = x_ref.shape[2]
    na = a_ref.shape[2]
    hid = wf_ref.shape[0]
    g1 = mod_ref[0, :, 2 * d:3 * d]
    shift = mod_ref[0, :, 3 * d:4 * d]
    scale = mod_ref[0, :, 4 * d:5 * d]
    g2 = mod_ref[0, :, 5 * d:6 * d]
    mixed = _dot(a_ref[0], wo_ref[0:na, :]) + _dot(b_ref[0], wo_ref[na:, :])
    x1 = x_ref[0] + g1 * mixed
    h = _norm_mod(x1, gain_ref[...], shift, scale).astype(BF16)
    half = hid // 2
    for s in range(0, hid, half):
        gate = _dot(h, wi_ref[:, s:s + half])
        up = _dot(h, wi_ref[:, hid + s:hid + s + half])
        act_ref[:, s:s + half] = (_silu(gate) * up).astype(BF16)
    o_ref[0] = x1 + g2 * _dot(act_ref[...], wf_ref[...])


def _mix_ffn(x, a, bmix, modv, gain, w_out, w_in, w_ffn_out):
    b, l, d = x.shape
    tm = min(ROW_TILE, l)
    hid = w_ffn_out.shape[0]
    row = lambda c: pl.BlockSpec((1, tm, c), lambda bi, i: (bi, i, 0))
    return pl.pallas_call(
        _mixffn_kernel,
        grid=(b, l // tm),
        in_specs=[row(d), row(a.shape[2]), row(bmix.shape[2]),
                  pl.BlockSpec((1, 1, modv.shape[2]), lambda bi, i: (bi, 0, 0)),
                  _resident(gain.shape), _resident(w_out.shape), _resident(w_in.shape), _resident(w_ffn_out.shape)],
        out_specs=row(d),
        out_shape=jax.ShapeDtypeStruct(x.shape, F32),
        scratch_shapes=[pltpu.VMEM((tm, hid), BF16)],
        compiler_params=_params("parallel", "parallel"),
        name="mix_ffn",
    )(x, a, bmix, modv, gain, w_out, w_in, w_ffn_out)


def _rope(x, cos, sin):
    lane = lax.broadcasted_iota(jnp.int32, x.shape, 1)
    partner = jnp.where(lane % 32 < 16, pltpu.roll(x, LANE - 16, 1), pltpu.roll(x, 16, 1))
    return x * cos + partner * sin


def _odin_kernel(x_ref, c_ref, mod_ref, gain_ref, w_ref, qan_ref, wuq_ref, kvan_ref, wukv_ref, dqn_ref, dkn_ref,
                 mqn_ref, mkn_ref, nm_ref, cos_ref, sin_ref, daq_ref, dak_ref, davt_ref, mq_ref, mk_ref, mvt_ref,
                 *, n_lat):
    d = x_ref.shape[2]
    xin = jnp.where(pl.program_id(1) < n_lat, x_ref[0], c_ref[0])
    h = _norm_mod(xin, gain_ref[...], mod_ref[0, 0, :, 0:d], mod_ref[0, 0, :, d:2 * d]).astype(BF16)
    cos = cos_ref[...]
    sin = sin_ref[...]
    ones = jnp.ones((V_ROWS - MLA_V, ATT_TK), BF16)

    def store_values_t(out_ref, vals):
        for hd in range(vals.shape[1] // LANE):
            for t in range(vals.shape[0] // ATT_TK):
                tile = vals[t * ATT_TK:(t + 1) * ATT_TK, hd * LANE:(hd + 1) * LANE]
                out_ref[0, hd, t, 0:MLA_V, :] = tile.T.astype(BF16)
                out_ref[0, hd, t, MLA_V:, :] = ones

    def slab_norm(u, g, kind):
        return u * lax.rsqrt(_dot(u * u, nm_ref[kind]) + EPS) * g

    pair_norm = functools.partial(slab_norm, kind=0)
    full_norm = functools.partial(slab_norm, kind=1)
    low_norm = functools.partial(slab_norm, kind=2)

    def da_part(out_ref, base, g, out_scale):
        p = _dot(h, w_ref[:, base:base + DA_QK])
        for s in range(0, DA_QK, LANE):
            u = _rope(pair_norm(p[:, s:s + LANE], g), cos, sin)
            out_ref[0, :, s:s + LANE] = (u * out_scale).astype(BF16)

    da_part(daq_ref, 0, dqn_ref[...], DA_DIM ** -0.5 * LOG2E)
    da_part(dak_ref, DA_QK, dkn_ref[...], 1.0)
    store_values_t(davt_ref, _dot(h, w_ref[:, 2 * DA_QK:2 * DA_QK + DA_VW]))

    def wide_norm(u, g):
        return u * lax.rsqrt(jnp.mean(u * u, axis=-1, keepdims=True) + EPS) * g

    c0 = 2 * DA_QK + DA_VW
    cq = wide_norm(_dot(h, w_ref[:, c0:c0 + MLA_Q_RANK]), qan_ref[...]).astype(BF16)
    qf = _dot(cq, wuq_ref[...])
    scale = (MLA_NOPE + MLA_ROPE) ** -0.5 * LOG2E
    for hd in range(MLA_HEADS):
        s = hd * MLA_QK_PAD
        mq_ref[0, :, s:s + LANE] = (full_norm(qf[:, s:s + LANE], mqn_ref[:, 0:LANE]) * scale).astype(BF16)
        u = _rope(low_norm(qf[:, s + LANE:s + 2 * LANE], mqn_ref[:, LANE:2 * LANE]), cos, sin)
        mq_ref[0, :, s + LANE:s + 2 * LANE] = (u * scale).astype(BF16)
    c0 += MLA_Q_RANK
    ckv = wide_norm(_dot(h, w_ref[:, c0:c0 + MLA_KV_RANK]), kvan_ref[...]).astype(BF16)
    kvu = _dot(ckv, wukv_ref[...])
    c0 += MLA_KV_RANK
    kr = _rope(low_norm(_dot(h, w_ref[:, c0:c0 + LANE]), mkn_ref[:, LANE:2 * LANE]), cos, sin).astype(BF16)
    for hd in range(MLA_HEADS):
        s = hd * MLA_QK_PAD
        mk_ref[0, :, s:s + LANE] = full_norm(kvu[:, hd * LANE:(hd + 1) * LANE], mkn_ref[:, 0:LANE]).astype(BF16)
        mk_ref[0, :, s + LANE:s + 2 * LANE] = kr
    store_values_t(mvt_ref, kvu[:, MLA_HEADS * MLA_NOPE:])


def _odd_in(x, ctx, mod2, gain, w_in, q_a_gain, w_uq, kv_a_gain, w_ukv, dq_gain, dk_gain, mq_gain, mk_gain,
            cos_t, sin_t):
    b, l, d = x.shape
    lc = ctx.shape[1]
    tm = min(ODD_TILE, l, lc)
    n_lat, n_ctx = l // tm, lc // tm
    lk = l + lc
    tiles = tm // ATT_TK
    row = lambda c: pl.BlockSpec((1, tm, c), lambda bi, i: (bi, i, 0))
    vt = pl.BlockSpec((1, MLA_HEADS, tiles, V_ROWS, ATT_TK), lambda bi, i: (bi, 0, i, 0, 0))
    widths = (DA_QK, DA_QK, None, MLA_HEADS * MLA_QK_PAD, MLA_HEADS * MLA_QK_PAD, None)
    table = pl.BlockSpec((tm, LANE), lambda bi, i: (i, 0))
    li = jnp.arange(LANE)
    half = (li[:, None] // DA_DIM == li[None, :] // DA_DIM).astype(F32) / DA_DIM
    low = jnp.broadcast_to((li[:, None] < MLA_ROPE).astype(F32) / MLA_ROPE, (LANE, LANE))
    norm_mats = jnp.stack([half, jnp.full((LANE, LANE), 1.0 / LANE, F32), low])
    res = [gain, w_in, q_a_gain, w_uq, kv_a_gain, w_ukv, dq_gain, dk_gain, mq_gain, mk_gain, norm_mats]
    vt_shape = jax.ShapeDtypeStruct((b, MLA_HEADS, lk // ATT_TK, V_ROWS, ATT_TK), BF16)
    return pl.pallas_call(
        functools.partial(_odin_kernel, n_lat=n_lat),
        grid=(b, n_lat + n_ctx),
        in_specs=[pl.BlockSpec((1, tm, d), lambda bi, i: (bi, jnp.minimum(i, n_lat - 1), 0)),
                  pl.BlockSpec((1, tm, d), lambda bi, i: (bi, jnp.maximum(i - n_lat, 0), 0)),
                  pl.BlockSpec((1, 1, 1, mod2.shape[3]), lambda bi, i: (bi, i // n_lat, 0, 0))]
        + [_resident(a.shape) for a in res] + [table, table],
        out_specs=[vt if c is None else row(c) for c in widths],
        out_shape=[vt_shape if c is None else jax.ShapeDtypeStruct((b, lk, c), BF16) for c in widths],
        compiler_params=_params("parallel", "parallel"),
        name="odd_in",
    )(x, ctx, mod2, *res, cos_t, sin_t)


def _flash_scratch(nchain, strip, tk, qk_width):
    half = [pltpu.VMEM((nchain, tk, strip), F32), pltpu.VMEM((nchain, 1, strip), F32)]
    return [pltpu.VMEM((nchain, 1, strip), F32), pltpu.VMEM((nchain, V_ROWS, strip), F32),
            pltpu.VMEM((nchain, qk_width, strip), BF16)] + half + half


def _flash_loop(q_ref, k_ref, vt_ref, scratch, strip, split_q):
    m_ref, acc_ref, qz_ref, s0_ref, mx0_ref, s1_ref, mx1_ref = scratch
    s_ref, mx_ref = (s0_ref, s1_ref), (mx0_ref, mx1_ref)
    tq = q_ref.shape[1]
    tk = vt_ref.shape[4]
    nk = vt_ref.shape[2]
    nchain = m_ref.shape[0]
    m_ref[...] = jnp.full(m_ref.shape, -jnp.inf, F32)
    acc_ref[...] = jnp.zeros(acc_ref.shape, F32)
    row = lax.broadcasted_iota(jnp.int32, (q_ref.shape[2], strip), 0)
    for st in range(tq // strip):
        qt = q_ref[0, pl.ds(st * strip, strip), :].astype(F32).T
        if split_q:
            qz_ref[2 * st] = jnp.where(row < DA_DIM, qt, 0.0).astype(BF16)
            qz_ref[2 * st + 1] = jnp.where(row < DA_DIM, 0.0, qt).astype(BF16)
        else:
            qz_ref[st] = qt.astype(BF16)

    def scores(j, c):
        st = _dot(k_ref[0, j * tk:(j + 1) * tk, :], qz_ref[c])
        s_ref[j % 2][c] = st
        mx_ref[j % 2][c] = jnp.max(st, axis=0, keepdims=True)

    def weigh(j, c):
        m_old = m_ref[c]
        m_new = jnp.maximum(m_old, mx_ref[j % 2][c])
        p = jnp.exp2((s_ref[j % 2][c] - m_new).astype(BF16))
        m_ref[c] = m_new
        return c, jnp.exp2(m_old - m_new), _dot(vt_ref[0, 0, j], p)

    def accumulate(pending):
        c, alpha, pv = pending
        acc_ref[c] = alpha * acc_ref[c] + pv

    for c in range(nchain):
        scores(0, c)
    pending = None
    for j in range(nk):
        for c in range(nchain):
            if j + 1 < nk:
                scores(j + 1, c)
            done, pending = pending, weigh(j, c)
            if done is not None:
                accumulate(done)
    accumulate(pending)


def _softmax_out(acc):
    return acc[0:MLA_V] / acc[MLA_V:MLA_V + 1]


def _da_kernel(q_ref, k_ref, vt_ref, lam_ref, gain_ref, o_ref, *scratch, strip, lambda_init):
    _flash_loop(q_ref, k_ref, vt_ref, scratch, strip, True)
    acc_ref = scratch[1]
    lp = lam_ref[...]
    lam = (jnp.exp(jnp.sum(lp[0:1] * lp[1:2], axis=-1, keepdims=True))
           - jnp.exp(jnp.sum(lp[2:3] * lp[3:4], axis=-1, keepdims=True)) + lambda_init)
    for st in range(q_ref.shape[1] // strip):
        o = _softmax_out(acc_ref[2 * st]) - lam * _softmax_out(acc_ref[2 * st + 1])
        o = o * lax.rsqrt(jnp.mean(o * o, axis=0, keepdims=True) + EPS) * gain_ref[...]
        o_ref[0, pl.ds(st * strip, strip), :] = (o * (1.0 - lambda_init)).T.astype(BF16)


def _diff_attention(q, k, vt, lq, lam_p, gain_col, lambda_init):
    b, lk, _ = k.shape
    tq = min(DA_TQ, lq)
    strip = min(ATT_STRIP, tq)
    qs = pl.BlockSpec((1, tq, LANE), lambda bi, h, i: (bi, i, h))
    ks = pl.BlockSpec((1, lk, LANE), lambda bi, h, i: (bi, 0, h))
    vs = pl.BlockSpec((1, 1) + vt.shape[2:], lambda bi, h, i: (bi, h, 0, 0, 0))
    nchain = 2 * (tq // strip)
    return pl.pallas_call(
        functools.partial(_da_kernel, strip=strip, lambda_init=lambda_init),
        grid=(b, DA_HEADS, lq // tq),
        in_specs=[qs, ks, vs, _resident(lam_p.shape), _resident(gain_col.shape)],
        out_specs=qs,
        out_shape=jax.ShapeDtypeStruct((b, lq, q.shape[2]), BF16),
        scratch_shapes=_flash_scratch(nchain, strip, ATT_TK, LANE),
        compiler_params=_params("parallel", "parallel", "parallel"),
        name="diff_attention",
    )(q, k, vt, lam_p, gain_col)


def _mla_kernel(q_ref, k_ref, vt_ref, o_ref, *scratch, strip):
    _flash_loop(q_ref, k_ref, vt_ref, scratch, strip, False)
    acc_ref = scratch[1]
    for st in range(q_ref.shape[1] // strip):
        o_ref[0, pl.ds(st * strip, strip), :] = _softmax_out(acc_ref[st]).T.astype(BF16)


def _mla_attention(q, k, vt, lq):
    b, lk, _ = k.shape
    tq = min(MLA_TQ, lq)
    strip = min(ATT_STRIP, tq)
    qs = pl.BlockSpec((1, tq, MLA_QK_PAD), lambda bi, h, i: (bi, i, h))
    ks = pl.BlockSpec((1, lk, MLA_QK_PAD), lambda bi, h, i: (bi, 0, h))
    vs = pl.BlockSpec((1, 1) + vt.shape[2:], lambda bi, h, i: (bi, h, 0, 0, 0))
    nchain = tq // strip
    return pl.pallas_call(
        functools.partial(_mla_kernel, strip=strip),
        grid=(b, MLA_HEADS, lq // tq),
        in_specs=[qs, ks, vs],
        out_specs=pl.BlockSpec((1, tq, MLA_V), lambda bi, h, i: (bi, i, h)),
        out_shape=jax.ShapeDtypeStruct((b, lq, MLA_HEADS * MLA_V), BF16),
        scratch_shapes=_flash_scratch(nchain, strip, ATT_TK, MLA_QK_PAD),
        compiler_params=_params("parallel", "parallel", "parallel"),
        name="mla_attention",
    )(q, k, vt)


def _scan_rows(gates_t, index, chunks):
    b, _, l = gates_t.shape
    idx = jnp.asarray(index, jnp.int32)
    rows = gates_t[:, idx, :]
    rows = rows.reshape(b, idx.shape[0], idx.shape[1], chunks, l // chunks)
    return jnp.swapaxes(rows, 2, 3)


def _even_layer(x, ctx, mod_x, mod_c, p):
    b = x.shape[0]
    nh, ns = DN_HEADS, SSD_HEADS
    dn_index = [[d * nh + h for h in range(nh)] + [2 * nh + d * nh + h for h in range(nh)] for d in range(2)]
    dt0, a0 = 4 * nh, 4 * nh + 2 * ns
    ssd_index = [[dt0 + g * SSD_GRP + r for r in range(SSD_GRP)] + [a0 + g * SSD_GRP + r for r in range(SSD_GRP)]
                 + [dt0 + ns + g * SSD_GRP + r for r in range(SSD_GRP)] + [a0 + ns + g * SSD_GRP + r for r in range(SSD_GRP)]
                 for g in range(SSD_GROUPS)]

    def prepare(t, modv):
        q, k, v, xs, bc, z, gates = _even_in(t, modv, p["gain_mix"], p["w_main"], p["w_small"], p["conv_w"], p["conv_b"],
                                             p["gate_bias"], p["gate_scale"])
        gt = jnp.swapaxes(gates, 1, 2)
        nchunk = t.shape[1] // CHUNK
        return dict(q=q, k=k, v=v, xs=xs, bc=bc, z=z, dn_rows=jnp.swapaxes(_scan_rows(gt, dn_index, nchunk), 1, 2),
                    ssd_rows=jnp.swapaxes(_scan_rows(gt, ssd_index, nchunk), 1, 2))

    pc = prepare(ctx, mod_c)
    pL = prepare(x, mod_x)
    dn0 = jnp.zeros((b, 2 * nh, DN_DK, DN_DV), F32)
    ssd0 = jnp.zeros((b, SSD_GROUPS, 2, SSD_STATE, SSD_GRP * SSD_HEAD_DIM), F32)
    ocf, ocb, dn1 = _dn_scan(pc["q"], pc["k"], pc["v"], pc["dn_rows"], dn0)
    olf, olb, _ = _dn_scan(pL["q"], pL["k"], pL["v"], pL["dn_rows"], dn1)
    ycf, ycb, ssd1 = _ssd_scan(pc["xs"], pc["bc"], pc["ssd_rows"], ssd0, p["expand"])
    ylf, ylb, _ = _ssd_scan(pL["xs"], pL["bc"], pL["ssd_rows"], ssd1, p["expand"])

    def finish(t, modv, pp, of, ob, yf, yb):
        a, bm = _even_merge(of, ob, yf, yb, pp["xs"], pp["z"], p["dn_gain"], p["ssd_d"], p["ssd_gain"])
        return _mix_ffn(t, a, bm, modv, p["gain_ffn"], p["w_out"], p["ffn_w_in"], p["ffn_w_out"])

    return finish(x, mod_x, pL, olf, olb, ylf, ylb), finish(ctx, mod_c, pc, ocf, ocb, ycf, ycb)


def _rope_tables(n_tokens):
    lane = jnp.arange(LANE)
    quarter = (lane % 64) // 16
    n_freq = DA_DIM // 4
    inv_freq = ROPE_THETA ** (-(lane % 16).astype(F32) / n_freq)
    tok = jnp.arange(n_tokens)
    pos = jnp.where(quarter[None, :] < 2, (tok // GRID_W)[:, None], (tok % GRID_W)[:, None]).astype(F32)
    ang = pos * inv_freq[None, :]
    sign = jnp.where(quarter % 2 == 0, -1.0, 1.0).astype(F32)
    return jnp.cos(ang), jnp.sin(ang) * sign[None, :]


def _odd_layer(x, ctx, mod_x, mod_c, p, lambda_init):
    lq, lc = x.shape[1], ctx.shape[1]
    cos_t, sin_t = _rope_tables(lq)
    cos_t = jnp.concatenate([cos_t, jnp.ones((lc, LANE), F32)], axis=0)
    sin_t = jnp.concatenate([sin_t, jnp.zeros((lc, LANE), F32)], axis=0)
    mod2 = jnp.stack([mod_x, mod_c], axis=1)
    daq, dak, davt, mq, mk, mvt = _odd_in(x, ctx, mod2, p["gain_mix"], p["w_in"], p["q_a_gain"], p["w_uq"], p["kv_a_gain"],
                                          p["w_ukv"], p["dq_gain"], p["dk_gain"], p["mq_gain"], p["mk_gain"], cos_t, sin_t)
    da = _diff_attention(daq, dak, davt, lq, p["da_lambda"], p["sub_gain"], lambda_init)
    ml = _mla_attention(mq, mk, mvt, lq)
    return _mix_ffn(x, da, ml, mod_x, p["gain_ffn"], p["w_out"], p["ffn_w_in"], p["ffn_w_out"])


def _even_params(i, j, norm_mix, norm_ffn, ffn_w_in, ffn_w_out, ev_w_in, ev_conv_w, ev_conv_b, dn_a_log, dn_dt_bias,
                 dn_norm, ssd_a_log, ssd_dt_bias, ssd_d, ssd_norm, ev_w_out):
    d = norm_mix.shape[1]
    w = ev_w_in[j]
    small = w[:, EV_CONV + EV_GATE:]
    nh, ns = DN_HEADS, SSD_HEADS
    w_small = jnp.concatenate([small, small[:, 4 * nh:], jnp.zeros((d, LANE - 4 * nh - 4 * ns), F32)], axis=1)
    zeros = lambda n: jnp.zeros((n,), F32)
    gate_bias = jnp.concatenate([zeros(2 * nh), dn_dt_bias[j].reshape(-1), ssd_dt_bias[j].reshape(-1),
                                 ssd_dt_bias[j].reshape(-1), zeros(LANE - 4 * nh - 4 * ns)])
    gate_scale = jnp.concatenate([jnp.ones((2 * nh,), F32), -jnp.exp(dn_a_log[j].reshape(-1)), jnp.ones((2 * ns,), F32),
                                  -jnp.exp(ssd_a_log[j].reshape(-1)), zeros(LANE - 4 * nh - 4 * ns)])
    expand = jnp.repeat(jnp.eye(SSD_GRP, dtype=F32), SSD_HEAD_DIM, axis=1)
    return dict(
        gain_mix=norm_mix[i].reshape(1, d), gain_ffn=norm_ffn[i].reshape(1, d),
        w_main=w[:, :EV_CONV + EV_GATE].astype(BF16), w_small=w_small,
        conv_w=ev_conv_w[j], conv_b=ev_conv_b[j].reshape(1, -1),
        gate_bias=gate_bias.reshape(1, LANE), gate_scale=gate_scale.reshape(1, LANE), expand=expand,
        dn_gain=dn_norm[j].reshape(1, DN_DV), ssd_d=jnp.repeat(ssd_d[j], SSD_HEAD_DIM).reshape(1, SSD_INNER),
        ssd_gain=ssd_norm[j].reshape(1, SSD_INNER), w_out=ev_w_out[j].astype(BF16),
        ffn_w_in=ffn_w_in[i].astype(BF16), ffn_w_out=ffn_w_out[i].astype(BF16))


def _odd_params(i, j, norm_mix, norm_ffn, ffn_w_in, ffn_w_out, od_w_in, da_q_norm, da_k_norm, da_lambda, da_sub_norm,
                mla_q_a_norm, mla_w_uq, mla_kv_a_norm, mla_w_ukv, mla_q_norm, mla_k_norm, od_w_out):
    d = norm_mix.shape[1]
    w = od_w_in[j]
    w_in = jnp.concatenate([w, jnp.zeros((d, LANE - MLA_ROPE), F32)], axis=1).astype(BF16)
    hq = mla_w_uq[j].reshape(MLA_Q_RANK, MLA_HEADS, MLA_NOPE + MLA_ROPE)
    hq = jnp.pad(hq, ((0, 0), (0, 0), (0, MLA_QK_PAD - MLA_NOPE - MLA_ROPE)))
    hkv = mla_w_ukv[j].reshape(MLA_KV_RANK, MLA_HEADS, MLA_NOPE + MLA_V)
    w_ukv = jnp.concatenate([hkv[:, :, :MLA_NOPE].reshape(MLA_KV_RANK, -1), hkv[:, :, MLA_NOPE:].reshape(MLA_KV_RANK, -1)], axis=1)
    pad_gain = lambda g: jnp.pad(g, (0, MLA_QK_PAD - MLA_NOPE - MLA_ROPE)).reshape(1, MLA_QK_PAD)
    return dict(
        gain_mix=norm_mix[i].reshape(1, d), gain_ffn=norm_ffn[i].reshape(1, d), w_in=w_in,
        q_a_gain=mla_q_a_norm[j].reshape(1, -1), w_uq=hq.reshape(MLA_Q_RANK, -1).astype(BF16),
        kv_a_gain=mla_kv_a_norm[j].reshape(1, -1), w_ukv=w_ukv.astype(BF16),
        dq_gain=jnp.tile(da_q_norm[j], 2).reshape(1, LANE), dk_gain=jnp.tile(da_k_norm[j], 2).reshape(1, LANE),
        mq_gain=pad_gain(mla_q_norm[j]), mk_gain=pad_gain(mla_k_norm[j]),
        da_lambda=jnp.pad(da_lambda[j], ((0, 4), (0, LANE - DA_DIM))), sub_gain=da_sub_norm[j].reshape(2 * DA_DIM, 1),
        w_out=od_w_out[j].astype(BF16), ffn_w_in=ffn_w_in[i].astype(BF16), ffn_w_out=ffn_w_out[i].astype(BF16))


def kernel(x, c, ctx, c_ctx, ada_w, ada_b, norm_mix, norm_ffn, ffn_w_in, ffn_w_out, ev_w_in, ev_conv_w, ev_conv_b, dn_a_log, dn_dt_bias, dn_norm, ssd_a_log, ssd_dt_bias, ssd_d, ssd_norm, ev_w_out, od_w_in, da_q_norm, da_k_norm, da_lambda, da_sub_norm, mla_q_a_norm, mla_w_uq, mla_kv_a_norm, mla_w_ukv, mla_q_norm, mla_k_norm, od_w_out):
    b, _, d = x.shape
    depth = ada_w.shape[0]
    assert b < 8
    cvec = jnp.concatenate([c, c_ctx[None, :], jnp.zeros((8 - b - 1, d), F32)], axis=0)
    mod = _modulation(cvec, ada_w, ada_b)
    for i in range(depth):
        last = i == depth - 1
        j = i // 2
        mod_x = mod[i, :b].reshape(b, 1, 6 * d)
        mod_c = jnp.broadcast_to(mod[i, b].reshape(1, 1, 6 * d), (b, 1, 6 * d))
        if i % 2 == 0:
            p = _even_params(i, j, norm_mix, norm_ffn, ffn_w_in, ffn_w_out, ev_w_in, ev_conv_w, ev_conv_b, dn_a_log,
                             dn_dt_bias, dn_norm, ssd_a_log, ssd_dt_bias, ssd_d, ssd_norm, ev_w_out)
            x, ctx_new = _even_layer(x, ctx, mod_x, mod_c, p)
        else:
            p = _odd_params(i, j, norm_mix, norm_ffn, ffn_w_in, ffn_w_out, od_w_in, da_q_norm, da_k_norm, da_lambda,
                            da_sub_norm, mla_q_a_norm, mla_w_uq, mla_kv_a_norm, mla_w_ukv, mla_q_norm, mla_k_norm, od_w_out)
            lambda_init = 0.8 - 0.6 * math.exp(-0.3 * i)
            if last:
                x = _odd_layer(x, ctx, mod_x, mod_c, p, lambda_init)
                ctx_new = ctx
            else:
                raise NotImplementedError("context update after an attention layer is not needed for depth 2")
        ctx = ctx_new
    return x
```

```python
import functools
import math

import jax
import jax.numpy as jnp
from jax import lax
from jax.experimental import pallas as pl
from jax.experimental.pallas import tpu as pltpu

F32 = jnp.float32
BF16 = jnp.bfloat16

GRID_W = 64
EPS = 1e-6
ROPE_THETA = 10000.0
CONV_K = 5
DN_HEADS = 8
DN_DK = 128
DN_DV = 128
SSD_HEADS = 16
SSD_HEAD_DIM = 64
SSD_GROUPS = 2
SSD_STATE = 128
SSD_GRP = SSD_HEADS // SSD_GROUPS
SSD_INNER = SSD_HEADS * SSD_HEAD_DIM
DA_HEADS = 8
DA_DIM = 64
MLA_HEADS = 8
MLA_Q_RANK = 512
MLA_KV_RANK = 256
MLA_NOPE = 128
MLA_ROPE = 64
MLA_V = 128
MLA_QK_PAD = 256

DN_QK = DN_HEADS * DN_DK
DN_VW = DN_HEADS * DN_DV
SSD_BC = SSD_GROUPS * SSD_STATE
EV_CONV = 2 * DN_QK + DN_VW + SSD_INNER + 2 * SSD_BC
EV_GATE = DN_VW + SSD_INNER
DA_QK = DA_HEADS * 2 * DA_DIM
DA_VW = DA_HEADS * 2 * DA_DIM

LANE = 128
BF16_ROWS = 16
CHUNK = 128
VMEM_LIMIT = 56 * 1024 * 1024

ROW_TILE = 512
ODD_TILE = 256
SCAN_TILE = 512
DN_TILE = 256
DA_TQ = 1024
MLA_TQ = 1024
ATT_TK = 256
ATT_STRIP = 256
V_ROWS = MLA_V + BF16_ROWS
LOG2E = 1.4426950408889634


def _dot(a, b):
    return jnp.dot(a, b, preferred_element_type=F32)


def _dot_nt(a, b):
    return lax.dot_general(a, b, (((1,), (1,)), ((), ())), preferred_element_type=F32)


def _dot_tn(a, b):
    return lax.dot_general(a, b, (((0,), (0,)), ((), ())), preferred_element_type=F32)


def _sigmoid(x):
    return 1.0 / (1.0 + jnp.exp(-x))


def _silu(x):
    return x * _sigmoid(x)


def _softplus(x):
    return jnp.maximum(x, 0.0) + jnp.log(1.0 + jnp.exp(-jnp.abs(x)))


def _params(*sem):
    return pltpu.CompilerParams(dimension_semantics=sem, vmem_limit_bytes=VMEM_LIMIT)


def _resident(shape):
    nd = len(shape)
    return pl.BlockSpec(shape, lambda *_: (0,) * nd, pipeline_mode=pl.Buffered(1))


def _norm_mod(x, gain, shift, scale):
    y = x * lax.rsqrt(jnp.mean(x * x, axis=-1, keepdims=True) + EPS)
    return (y * gain) * (1.0 + scale) + shift


def _mod_kernel(c_ref, w_ref, b_ref, o_ref):
    o_ref[0] = _dot(_silu(c_ref[...]), w_ref[0]) + b_ref[0]


def _modulation(cvec, ada_w, ada_b):
    depth, d, n = ada_w.shape
    tn = n // 4
    return pl.pallas_call(
        _mod_kernel,
        grid=(depth, n // tn),
        in_specs=[
            pl.BlockSpec((8, d), lambda l, j: (0, 0)),
            pl.BlockSpec((1, d, tn), lambda l, j: (l, 0, j)),
            pl.BlockSpec((1, 1, tn), lambda l, j: (l, 0, j)),
        ],
        out_specs=pl.BlockSpec((1, 8, tn), lambda l, j: (l, 0, j)),
        out_shape=jax.ShapeDtypeStruct((depth, 8, n), F32),
        compiler_params=_params("parallel", "parallel"),
        name="modulation",
    )(cvec, ada_w, ada_b.reshape(depth, 1, n))


def _evin_kernel(xp_ref, xc_ref, xn_ref, mod_ref, gain_ref, w_ref, wsm_ref, cw_ref, cb_ref, gb_ref, gs_ref,
                 q_ref, k_ref, v_ref, xs_ref, bc_ref, z_ref, g_ref, hext_ref, pext_ref):
    i = pl.program_id(1)
    nb = pl.num_programs(1)
    tm = xc_ref.shape[1]
    d = xc_ref.shape[2]
    halo = BF16_ROWS
    gain = gain_ref[...]
    shift = mod_ref[0, :, 0:d]
    scale = mod_ref[0, :, d:2 * d]

    hc = _norm_mod(xc_ref[0], gain, shift, scale)
    hp = _norm_mod(xp_ref[0], gain, shift, scale)
    hn = _norm_mod(xn_ref[0], gain, shift, scale)
    hext_ref[0:halo, :] = jnp.where(i > 0, hp, 0.0).astype(BF16)
    hext_ref[halo:halo + tm, :] = hc.astype(BF16)
    hext_ref[halo + tm:, :] = jnp.where(i < nb - 1, hn, 0.0).astype(BF16)

    gw = pext_ref.shape[1]
    pad = CONV_K // 2
    plan = ((q_ref, 0, DN_QK, DN_DK ** -0.5), (k_ref, DN_QK, DN_QK, 1.0), (v_ref, 2 * DN_QK, DN_VW, None),
            (xs_ref, 2 * DN_QK + DN_VW, SSD_INNER, None), (bc_ref, 2 * DN_QK + DN_VW + SSD_INNER, 2 * SSD_BC, None))
    for out_ref, base, width, l2_scale in plan:
        for off in range(0, width, gw):
            c0 = base + off
            pext_ref[...] = _dot(hext_ref[...], w_ref[:, c0:c0 + gw])
            acc = cb_ref[:, c0:c0 + gw] + cw_ref[0:1, c0:c0 + gw] * pext_ref[halo - pad:halo - pad + tm, :]
            for t in range(1, CONV_K):
                acc = acc + cw_ref[t:t + 1, c0:c0 + gw] * pext_ref[halo - pad + t:halo - pad + t + tm, :]
            u = _silu(acc)
            if l2_scale is None:
                out_ref[0, :, off:off + gw] = u
            else:
                for s in range(0, gw, LANE):
                    uh = u[:, s:s + LANE]
                    inv = lax.rsqrt(jnp.sum(uh * uh, axis=-1, keepdims=True) + EPS)
                    out_ref[0, :, off + s:off + s + LANE] = uh * (inv * l2_scale)

    hcb = hext_ref[halo:halo + tm, :]
    for off in range(0, EV_GATE, gw):
        z_ref[0, :, off:off + gw] = _dot(hcb, w_ref[:, EV_CONV + off:EV_CONV + off + gw]).astype(z_ref.dtype)

    p = _dot(hc, wsm_ref[...]) + gb_ref[...]
    lane = lax.broadcasted_iota(jnp.int32, p.shape, 1)
    g_ref[0] = jnp.where(lane < 2 * DN_HEADS, _sigmoid(p), _softplus(p)) * gs_ref[...]


def _even_in(x, modv, gain, w_main, w_small, conv_w, conv_b, gate_bias, gate_scale):
    b, l, d = x.shape
    tm = min(ROW_TILE, l)
    nb = l // tm
    hb = tm // BF16_ROWS
    gw = 512
    row = lambda c: pl.BlockSpec((1, tm, c), lambda bi, i: (bi, i, 0))
    outs = (DN_QK, DN_QK, DN_VW, SSD_INNER, 2 * SSD_BC, EV_GATE, LANE)
    return pl.pallas_call(
        _evin_kernel,
        grid=(b, nb),
        in_specs=[
            pl.BlockSpec((1, BF16_ROWS, d), lambda bi, i: (bi, jnp.maximum(i * hb - 1, 0), 0)),
            row(d),
            pl.BlockSpec((1, BF16_ROWS, d), lambda bi, i: (bi, jnp.minimum((i + 1) * hb, l // BF16_ROWS - 1), 0)),
            pl.BlockSpec((1, 1, modv.shape[2]), lambda bi, i: (bi, 0, 0)),
            _resident(gain.shape), _resident(w_main.shape), _resident(w_small.shape), _resident(conv_w.shape),
            _resident(conv_b.shape), _resident(gate_bias.shape), _resident(gate_scale.shape),
        ],
        out_specs=[row(c) for c in outs],
        out_shape=[jax.ShapeDtypeStruct((b, l, c), BF16 if c == EV_GATE else F32) for c in outs],
        scratch_shapes=[pltpu.VMEM((tm + 2 * BF16_ROWS, d), BF16), pltpu.VMEM((tm + 2 * BF16_ROWS, gw), F32)],
        compiler_params=_params("parallel", "parallel"),
        name="even_in",
    )(x, x, x, modv, gain, w_main, w_small, conv_w, conv_b, gate_bias, gate_scale)


def _tri_masks(c):
    r = lax.broadcasted_iota(jnp.int32, (c, c), 0)
    s = lax.broadcasted_iota(jnp.int32, (c, c), 1)
    return r, s


TRI_BASE = 16


def _block_masks(r, s):
    same = lambda n: (r // n) == (s // n)
    masks = [same(TRI_BASE).astype(F32)]
    n = TRI_BASE
    while n < CHUNK:
        n *= 2
        masks.append(jnp.where(same(n), 1.0, 0.0) - jnp.where(same(n // 2), 1.0, 0.0))
    return masks


def _unit_tri_inverses(a, eye, blocks):
    p = [-(a_n * blocks[0]) for a_n in a]
    x = [eye + p_n for p_n in p]
    for _ in range(int(math.log2(TRI_BASE)) - 1):
        p = [_dot(p_n, p_n) for p_n in p]
        x = [x_n + _dot(x_n, p_n) for x_n, p_n in zip(x, p)]
    for m in blocks[1:]:
        e = [_dot(a_n * m, x_n) for a_n, x_n in zip(a, x)]
        x = [x_n - _dot(x_n, e_n) for x_n, e_n in zip(x, e)]
    return x


def _dn_chunks(q, k, v, gc, beta, crow, total, state, incl, strict, eye, blocks):
    n = range(len(q))
    c = q[0].shape[0]
    decay = [jnp.where(incl[i], jnp.exp(gc[i] - crow[i]), 0.0) for i in n]
    kb = [k[i] * beta[i] for i in n]
    a = [jnp.where(strict[i], _dot_nt(kb[i], k[i]) * decay[i], 0.0) for i in n]
    t = _unit_tri_inverses(a, eye, blocks)
    egc = [jnp.exp(gc[i]) for i in n]
    uw = [_dot(t[i], jnp.concatenate([v[i] * beta[i], kb[i] * egc[i]], axis=1)) for i in n]
    ws_qs = [_dot(jnp.concatenate([uw[i][:, DN_DV:], q[i] * egc[i]], axis=0), state[i]) for i in n]
    v_new = [uw[i][:, 0:DN_DV] - ws_qs[i][0:c] for i in n]
    attn = [_dot_nt(q[i], k[i]) * decay[i] for i in n]
    o = [ws_qs[i][c:2 * c] + _dot(attn[i], v_new[i]) for i in n]
    state = [state[i] * jnp.exp(total[i]) + _dot_tn(k[i] * jnp.exp(total[i] - gc[i]), v_new[i]) for i in n]
    return o, state


def _dn_kernel(qf_ref, kf_ref, vf_ref, gf_ref, qb_ref, kb_ref, vb_ref, gb_ref, s0_ref,
               of_ref, ob_ref, sfin_ref, s_ref):
    i = pl.program_id(1)
    nck = gf_ref.shape[1]
    c = CHUNK
    nh = DN_HEADS

    @pl.when(i == 0)
    def _():
        s_ref[...] = s0_ref[0]

    r, s = _tri_masks(c)
    eye = (r == s).astype(F32)
    blocks = _block_masks(r, s)
    dirs = ((qf_ref, kf_ref, vf_ref, gf_ref, of_ref, r >= s, r > s, c - 1),
            (qb_ref, kb_ref, vb_ref, gb_ref, ob_ref, r <= s, r < s, 0))

    def body(ci, carry):
        q, k, v, gc, beta, crow, total, state, incl, strict, dest = ([] for _ in range(11))
        for d, (q_ref, k_ref, v_ref, g_ref, o_ref, inc, stri, tot_row) in enumerate(dirs):
            cc = ci if d == 0 else nck - 1 - ci
            rows = pl.ds(pl.multiple_of(cc * c, c), c)
            inclf = inc.astype(F32)
            grow = g_ref[0, cc, d]
            cum_rows = _dot_nt(grow, inclf)
            cols = _dot_nt(jnp.concatenate([inclf, eye], axis=0), grow)
            for h in range(nh):
                lanes = slice(h * LANE, (h + 1) * LANE)
                q.append(q_ref[0, rows, lanes])
                k.append(k_ref[0, rows, lanes])
                v.append(v_ref[0, rows, lanes])
                gc.append(cols[0:c, nh + h:nh + h + 1])
                beta.append(cols[c:2 * c, h:h + 1])
                crow.append(cum_rows[nh + h:nh + h + 1, :])
                total.append(cols[tot_row:tot_row + 1, nh + h:nh + h + 1])
                state.append(s_ref[d * nh + h])
                incl.append(inc)
                strict.append(stri)
                dest.append((o_ref, rows, lanes))
        o, state = _dn_chunks(q, k, v, gc, beta, crow, total, state, incl, strict, eye, blocks)
        for n, (o_ref, rows, lanes) in enumerate(dest):
            o_ref[0, rows, lanes] = o[n].astype(o_ref.dtype)
            s_ref[n] = state[n]
        return carry

    lax.fori_loop(0, nck, body, 0)

    @pl.when(i == pl.num_programs(1) - 1)
    def _():
        sfin_ref[0] = s_ref[...]


def _dn_scan(q, k, v, grows, state0):
    b, l, width = q.shape
    tl = min(DN_TILE, l)
    nb = l // tl
    nck = tl // CHUNK
    fwd = pl.BlockSpec((1, tl, width), lambda bi, i: (bi, i, 0))
    bwd = pl.BlockSpec((1, tl, width), lambda bi, i: (bi, nb - 1 - i, 0))
    gfwd = pl.BlockSpec((1, nck) + grows.shape[2:], lambda bi, i: (bi, i, 0, 0, 0))
    gbwd = pl.BlockSpec((1, nck) + grows.shape[2:], lambda bi, i: (bi, nb - 1 - i, 0, 0, 0))
    st = pl.BlockSpec((1,) + state0.shape[1:], lambda bi, i: (bi, 0, 0, 0))
    return pl.pallas_call(
        _dn_kernel,
        grid=(b, nb),
        in_specs=[fwd, fwd, fwd, gfwd, bwd, bwd, bwd, gbwd, st],
        out_specs=[fwd, bwd, st],
        out_shape=[jax.ShapeDtypeStruct(q.shape, BF16), jax.ShapeDtypeStruct(q.shape, BF16),
                   jax.ShapeDtypeStruct(state0.shape, F32)],
        scratch_shapes=[pltpu.VMEM(state0.shape[1:], F32)],
        compiler_params=_params("parallel", "arbitrary"),
        name="deltanet_scan",
    )(q, k, v, grows, q, k, v, grows, state0)


def _ssd_chunks(x, bm, cm, rows, state, expand, incl, eye, first_row):
    n = range(len(x))
    c = x[0].shape[0]
    nh = SSD_GRP
    p = SSD_HEAD_DIM
    inclf = [incl[i].astype(F32) for i in n]
    cum_row = [_dot_nt(rows[i][nh:2 * nh], inclf[i]) for i in n]
    cols = [_dot_nt(jnp.concatenate([inclf[i], eye], axis=0), rows[i]) for i in n]
    cum = [cols[i][0:c, nh:2 * nh] for i in n]
    total = [cum[i][first_row[i]:first_row[i] + 1, :] for i in n]
    xdt = [x[i] * _dot(cols[i][c:2 * c, 0:nh], expand) for i in n]
    cb = [_dot_nt(cm[i], bm[i]) for i in n]
    from_state = [_dot(cm[i], state[i]) * _dot(jnp.exp(cum[i]), expand) for i in n]
    parts = [[] for _ in n]
    for h in range(nh):
        seg = [jnp.where(incl[i], jnp.exp(cum[i][:, h:h + 1] - cum_row[i][h:h + 1, :]), 0.0) for i in n]
        for i in n:
            parts[i].append(_dot(cb[i] * seg[i], xdt[i][:, h * p:(h + 1) * p]))
    y = [jnp.concatenate(parts[i], axis=1) + from_state[i] for i in n]
    carry_in = [xdt[i] * _dot(jnp.exp(total[i] - cum[i]), expand) for i in n]
    state = [state[i] * _dot(jnp.exp(total[i]), expand) + _dot_tn(bm[i], carry_in[i]) for i in n]
    return y, state


def _ssd_kernel(xf_ref, bcf_ref, gf_ref, xb_ref, bcb_ref, gb_ref, s0_ref, e_ref, yf_ref, yb_ref, sfin_ref, s_ref):
    i = pl.program_id(1)
    nck = gf_ref.shape[1]
    c = CHUNK
    nh = SSD_GRP
    ng = SSD_GROUPS
    gwid = nh * SSD_HEAD_DIM

    @pl.when(i == 0)
    def _():
        s_ref[...] = s0_ref[0]

    r, s = _tri_masks(c)
    eye = (r == s).astype(F32)
    expand = e_ref[...]
    dirs = ((xf_ref, bcf_ref, gf_ref, yf_ref, r >= s, c - 1), (xb_ref, bcb_ref, gb_ref, yb_ref, r <= s, 0))

    def body(ci, carry):
        x, bm, cm, rows, state, incl, first, dest = ([] for _ in range(8))
        for d, (x_ref, bc_ref, g_ref, y_ref, inc, tot_row) in enumerate(dirs):
            cc = ci if d == 0 else nck - 1 - ci
            tok = pl.ds(pl.multiple_of(cc * c, c), c)
            for g in range(ng):
                lanes = slice(g * gwid, (g + 1) * gwid)
                x.append(x_ref[0, tok, lanes])
                bm.append(bc_ref[0, tok, g * SSD_STATE:(g + 1) * SSD_STATE])
                cm.append(bc_ref[0, tok, (ng + g) * SSD_STATE:(ng + g + 1) * SSD_STATE])
                rows.append(g_ref[0, cc, g][2 * nh * d:2 * nh * (d + 1)])
                state.append(s_ref[g, d])
                incl.append(inc)
                first.append(tot_row)
                dest.append((y_ref, tok, lanes, g, d))
        y, state = _ssd_chunks(x, bm, cm, rows, state, expand, incl, eye, first)
        for n, (y_ref, tok, lanes, g, d) in enumerate(dest):
            y_ref[0, tok, lanes] = y[n].astype(y_ref.dtype)
            s_ref[g, d] = state[n]
        return carry

    lax.fori_loop(0, nck, body, 0)

    @pl.when(i == pl.num_programs(1) - 1)
    def _():
        sfin_ref[0] = s_ref[...]


def _ssd_scan(xs, bc, grows, state0, expand):
    b, l, width = xs.shape
    tl = min(SCAN_TILE, l)
    nb = l // tl
    nck = tl // CHUNK
    fwd = lambda w: pl.BlockSpec((1, tl, w), lambda bi, i: (bi, i, 0))
    bwd = lambda w: pl.BlockSpec((1, tl, w), lambda bi, i: (bi, nb - 1 - i, 0))
    gf = pl.BlockSpec((1, nck) + grows.shape[2:], lambda bi, i: (bi, i, 0, 0, 0))
    gb = pl.BlockSpec((1, nck) + grows.shape[2:], lambda bi, i: (bi, nb - 1 - i, 0, 0, 0))
    st = pl.BlockSpec((1,) + state0.shape[1:], lambda bi, i: (bi, 0, 0, 0, 0))
    return pl.pallas_call(
        _ssd_kernel,
        grid=(b, nb),
        in_specs=[fwd(width), fwd(bc.shape[2]), gf, bwd(width), bwd(bc.shape[2]), gb, st, _resident(expand.shape)],
        out_specs=[fwd(width), bwd(width), st],
        out_shape=[jax.ShapeDtypeStruct(xs.shape, BF16), jax.ShapeDtypeStruct(xs.shape, BF16),
                   jax.ShapeDtypeStruct(state0.shape, F32)],
        scratch_shapes=[pltpu.VMEM(state0.shape[1:], F32)],
        compiler_params=_params("parallel", "arbitrary"),
        name="ssd_scan",
    )(xs, bc, grows, xs, bc, grows, state0, expand)


def _evmerge_kernel(of_ref, ob_ref, yf_ref, yb_ref, xs_ref, z_ref, dng_ref, sd_ref, sg_ref, a_ref, b_ref):
    f32 = lambda ref, lo, width: ref[0, :, lo:lo + width].astype(F32)
    for s in range(0, DN_VW, LANE):
        o = f32(of_ref, s, LANE) + f32(ob_ref, s, LANE)
        o = o * lax.rsqrt(jnp.mean(o * o, axis=-1, keepdims=True) + EPS) * dng_ref[...]
        a_ref[0, :, s:s + LANE] = (o * _silu(f32(z_ref, s, LANE))).astype(BF16)
    gwid = SSD_INNER // SSD_GROUPS
    for s in range(0, SSD_INNER, gwid):
        y = f32(yf_ref, s, gwid) + f32(yb_ref, s, gwid) + sd_ref[:, s:s + gwid] * xs_ref[0, :, s:s + gwid]
        y = y * _silu(f32(z_ref, DN_VW + s, gwid))
        y = y * lax.rsqrt(jnp.mean(y * y, axis=-1, keepdims=True) + EPS) * sg_ref[:, s:s + gwid]
        b_ref[0, :, s:s + gwid] = y.astype(BF16)


def _even_merge(o_f, o_b, y_f, y_b, xs, z, dn_gain, ssd_d, ssd_gain):
    b, l, _ = o_f.shape
    tm = min(ROW_TILE, l)
    row = lambda c: pl.BlockSpec((1, tm, c), lambda bi, i: (bi, i, 0))
    return pl.pallas_call(
        _evmerge_kernel,
        grid=(b, l // tm),
        in_specs=[row(DN_VW), row(DN_VW), row(SSD_INNER), row(SSD_INNER), row(SSD_INNER), row(EV_GATE),
                  _resident(dn_gain.shape), _resident(ssd_d.shape), _resident(ssd_gain.shape)],
        out_specs=[row(DN_VW), row(SSD_INNER)],
        out_shape=[jax.ShapeDtypeStruct((b, l, DN_VW), BF16), jax.ShapeDtypeStruct((b, l, SSD_INNER), BF16)],
        compiler_params=_params("parallel", "parallel"),
        name="even_merge",
    )(o_f, o_b, y_f, y_b, xs, z, dn_gain, ssd_d, ssd_gain)


def _mixffn_kernel(x_ref, a_ref, b_ref, mod_ref, gain_ref, wo_ref, wi_ref, wf_ref, o_ref, act_ref):
    d = x_ref.shape[2]
    na = a_ref.shape[2]
    hid = wf_ref.shape[0]
    g1 = mod_ref[0, :, 2 * d:3 * d]
    shift = mod_ref[0, :, 3 * d:4 * d]
    scale = mod_ref[0, :, 4 * d:5 * d]
    g2 = mod_ref[0, :, 5 * d:6 * d]
    mixed = _dot(a_ref[0], wo_ref[0:na, :]) + _dot(b_ref[0], wo_ref[na:, :])
    x1 = x_ref[0] + g1 * mixed
    h = _norm_mod(x1, gain_ref[...], shift, scale).astype(BF16)
    half = hid // 2
    for s in range(0, hid, half):
        gate = _dot(h, wi_ref[:, s:s + half])
        up = _dot(h, wi_ref[:, hid + s:hid + s + half])
        act_ref[:, s:s + half] = (_silu(gate) * up).astype(BF16)
    o_ref[0] = x1 + g2 * _dot(act_ref[...], wf_ref[...])


def _mix_ffn(x, a, bmix, modv, gain, w_out, w_in, w_ffn_out):
    b, l, d = x.shape
    tm = min(ROW_TILE, l)
    hid = w_ffn_out.shape[0]
    row = lambda c: pl.BlockSpec((1, tm, c), lambda bi, i: (bi, i, 0))
    return pl.pallas_call(
        _mixffn_kernel,
        grid=(b, l // tm),
        in_specs=[row(d), row(a.shape[2]), row(bmix.shape[2]),
                  pl.BlockSpec((1, 1, modv.shape[2]), lambda bi, i: (bi, 0, 0)),
                  _resident(gain.shape), _resident(w_out.shape), _resident(w_in.shape), _resident(w_ffn_out.shape)],
        out_specs=row(d),
        out_shape=jax.ShapeDtypeStruct(x.shape, F32),
        scratch_shapes=[pltpu.VMEM((tm, hid), BF16)],
        compiler_params=_params("parallel", "parallel"),
        name="mix_ffn",
    )(x, a, bmix, modv, gain, w_out, w_in, w_ffn_out)


def _rope(x, cos, sin):
    lane = lax.broadcasted_iota(jnp.int32, x.shape, 1)
    partner = jnp.where(lane % 32 < 16, pltpu.roll(x, LANE - 16, 1), pltpu.roll(x, 16, 1))
    return x * cos + partner * sin


def _odin_kernel(x_ref, c_ref, mod_ref, gain_ref, w_ref, qan_ref, wuq_ref, kvan_ref, wukv_ref, dqn_ref, dkn_ref,
                 mqn_ref, mkn_ref, nm_ref, cos_ref, sin_ref, daq_ref, dak_ref, davt_ref, mq_ref, mk_ref, mvt_ref,
                 *, n_lat):
    d = x_ref.shape[2]
    xin = jnp.where(pl.program_id(1) < n_lat, x_ref[0], c_ref[0])
    h = _norm_mod(xin, gain_ref[...], mod_ref[0, 0, :, 0:d], mod_ref[0, 0, :, d:2 * d]).astype(BF16)
    cos = cos_ref[...]
    sin = sin_ref[...]
    ones = jnp.ones((V_ROWS - MLA_V, ATT_TK), BF16)

    def store_values_t(out_ref, vals):
        for hd in range(vals.shape[1] // LANE):
            for t in range(vals.shape[0] // ATT_TK):
                tile = vals[t * ATT_TK:(t + 1) * ATT_TK, hd * LANE:(hd + 1) * LANE]
                out_ref[0, hd, t, 0:MLA_V, :] = tile.T.astype(BF16)
                out_ref[0, hd, t, MLA_V:, :] = ones

    def slab_norm(u, g, kind):
        return u * lax.rsqrt(_dot(u * u, nm_ref[kind]) + EPS) * g

    pair_norm = functools.partial(slab_norm, kind=0)
    full_norm = functools.partial(slab_norm, kind=1)
    low_norm = functools.partial(slab_norm, kind=2)

    def da_part(out_ref, base, g, out_scale):
        p = _dot(h, w_ref[:, base:base + DA_QK])
        for s in range(0, DA_QK, LANE):
            u = _rope(pair_norm(p[:, s:s + LANE], g), cos, sin)
            out_ref[0, :, s:s + LANE] = (u * out_scale).astype(BF16)

    da_part(daq_ref, 0, dqn_ref[...], DA_DIM ** -0.5 * LOG2E)
    da_part(dak_ref, DA_QK, dkn_ref[...], 1.0)
    store_values_t(davt_ref, _dot(h, w_ref[:, 2 * DA_QK:2 * DA_QK + DA_VW]))

    def wide_norm(u, g):
        return u * lax.rsqrt(jnp.mean(u * u, axis=-1, keepdims=True) + EPS) * g

    c0 = 2 * DA_QK + DA_VW
    cq = wide_norm(_dot(h, w_ref[:, c0:c0 + MLA_Q_RANK]), qan_ref[...]).astype(BF16)
    qf = _dot(cq, wuq_ref[...])
    scale = (MLA_NOPE + MLA_ROPE) ** -0.5 * LOG2E
    for hd in range(MLA_HEADS):
        s = hd * MLA_QK_PAD
        mq_ref[0, :, s:s + LANE] = (full_norm(qf[:, s:s + LANE], mqn_ref[:, 0:LANE]) * scale).astype(BF16)
        u = _rope(low_norm(qf[:, s + LANE:s + 2 * LANE], mqn_ref[:, LANE:2 * LANE]), cos, sin)
        mq_ref[0, :, s + LANE:s + 2 * LANE] = (u * scale).astype(BF16)
    c0 += MLA_Q_RANK
    ckv = wide_norm(_dot(h, w_ref[:, c0:c0 + MLA_KV_RANK]), kvan_ref[...]).astype(BF16)
    kvu = _dot(ckv, wukv_ref[...])
    c0 += MLA_KV_RANK
    kr = _rope(low_norm(_dot(h, w_ref[:, c0:c0 + LANE]), mkn_ref[:, LANE:2 * LANE]), cos, sin).astype(BF16)
    for hd in range(MLA_HEADS):
        s = hd * MLA_QK_PAD
        mk_ref[0, :, s:s + LANE] = full_norm(kvu[:, hd * LANE:(hd + 1) * LANE], mkn_ref[:, 0:LANE]).astype(BF16)
        mk_ref[0, :, s + LANE:s + 2 * LANE] = kr
    store_values_t(mvt_ref, kvu[:, MLA_HEADS * MLA_NOPE:])


def _odd_in(x, ctx, mod2, gain, w_in, q_a_gain, w_uq, kv_a_gain, w_ukv, dq_gain, dk_gain, mq_gain, mk_gain,
            cos_t, sin_t):
    b, l, d = x.shape
    lc = ctx.shape[1]
    tm = min(ODD_TILE, l, lc)
    n_lat, n_ctx = l // tm, lc // tm
    lk = l + lc
    tiles = tm // ATT_TK
    row = lambda c: pl.BlockSpec((1, tm, c), lambda bi, i: (bi, i, 0))
    vt = pl.BlockSpec((1, MLA_HEADS, tiles, V_ROWS, ATT_TK), lambda bi, i: (bi, 0, i, 0, 0))
    widths = (DA_QK, DA_QK, None, MLA_HEADS * MLA_QK_PAD, MLA_HEADS * MLA_QK_PAD, None)
    table = pl.BlockSpec((tm, LANE), lambda bi, i: (i, 0))
    li = jnp.arange(LANE)
    half = (li[:, None] // DA_DIM == li[None, :] // DA_DIM).astype(F32) / DA_DIM
    low = jnp.broadcast_to((li[:, None] < MLA_ROPE).astype(F32) / MLA_ROPE, (LANE, LANE))
    norm_mats = jnp.stack([half, jnp.full((LANE, LANE), 1.0 / LANE, F32), low])
    res = [gain, w_in, q_a_gain, w_uq, kv_a_gain, w_ukv, dq_gain, dk_gain, mq_gain, mk_gain, norm_mats]
    vt_shape = jax.ShapeDtypeStruct((b, MLA_HEADS, lk // ATT_TK, V_ROWS, ATT_TK), BF16)
    return pl.pallas_call(
        functools.partial(_odin_kernel, n_lat=n_lat),
        grid=(b, n_lat + n_ctx),
        in_specs=[pl.BlockSpec((1, tm, d), lambda bi, i: (bi, jnp.minimum(i, n_lat - 1), 0)),
                  pl.BlockSpec((1, tm, d), lambda bi, i: (bi, jnp.maximum(i - n_lat, 0), 0)),
                  pl.BlockSpec((1, 1, 1, mod2.shape[3]), lambda bi, i: (bi, i // n_lat, 0, 0))]
        + [_resident(a.shape) for a in res] + [table, table],
        out_specs=[vt if c is None else row(c) for c in widths],
        out_shape=[vt_shape if c is None else jax.ShapeDtypeStruct((b, lk, c), BF16) for c in widths],
        compiler_params=_params("parallel", "parallel"),
        name="odd_in",
    )(x, ctx, mod2, *res, cos_t, sin_t)


def _flash_scratch(nchain, strip, tk, qk_width):
    half = [pltpu.VMEM((nchain, tk, strip), F32), pltpu.VMEM((nchain, 1, strip), F32)]
    return [pltpu.VMEM((nchain, 1, strip), F32), pltpu.VMEM((nchain, V_ROWS, strip), F32),
            pltpu.VMEM((nchain, qk_width, strip), BF16)] + half + half


def _flash_loop(q_ref, k_ref, vt_ref, scratch, strip, split_q):
    m_ref, acc_ref, qz_ref, s0_ref, mx0_ref, s1_ref, mx1_ref = scratch
    s_ref, mx_ref = (s0_ref, s1_ref), (mx0_ref, mx1_ref)
    tq = q_ref.shape[1]
    tk = vt_ref.shape[4]
    nk = vt_ref.shape[2]
    nchain = m_ref.shape[0]
    m_ref[...] = jnp.full(m_ref.shape, -jnp.inf, F32)
    acc_ref[...] = jnp.zeros(acc_ref.shape, F32)
    row = lax.broadcasted_iota(jnp.int32, (q_ref.shape[2], strip), 0)
    for st in range(tq // strip):
        qt = q_ref[0, pl.ds(st * strip, strip), :].astype(F32).T
        if split_q:
            qz_ref[2 * st] = jnp.where(row < DA_DIM, qt, 0.0).astype(BF16)
            qz_ref[2 * st + 1] = jnp.where(row < DA_DIM, 0.0, qt).astype(BF16)
        else:
            qz_ref[st] = qt.astype(BF16)

    def scores(j, c):
        st = _dot(k_ref[0, j * tk:(j + 1) * tk, :], qz_ref[c])
        s_ref[j % 2][c] = st
        mx_ref[j % 2][c] = jnp.max(st, axis=0, keepdims=True)

    def weigh(j, c):
        m_old = m_ref[c]
        m_new = jnp.maximum(m_old, mx_ref[j % 2][c])
        p = jnp.exp2((s_ref[j % 2][c] - m_new).astype(BF16))
        m_ref[c] = m_new
        return c, jnp.exp2(m_old - m_new), _dot(vt_ref[0, 0, j], p)

    def accumulate(pending):
        c, alpha, pv = pending
        acc_ref[c] = alpha * acc_ref[c] + pv

    for c in range(nchain):
        scores(0, c)
    pending = None
    for j in range(nk):
        for c in range(nchain):
            if j + 1 < nk:
                scores(j + 1, c)
            done, pending = pending, weigh(j, c)
            if done is not None:
                accumulate(done)
    accumulate(pending)


def _softmax_out(acc):
    return acc[0:MLA_V] / acc[MLA_V:MLA_V + 1]


def _da_kernel(q_ref, k_ref, vt_ref, lam_ref, gain_ref, o_ref, *scratch, strip, lambda_init):
    _flash_loop(q_ref, k_ref, vt_ref, scratch, strip, True)
    acc_ref = scratch[1]
    lp = lam_ref[...]
    lam = (jnp.exp(jnp.sum(lp[0:1] * lp[1:2], axis=-1, keepdims=True))
           - jnp.exp(jnp.sum(lp[2:3] * lp[3:4], axis=-1, keepdims=True)) + lambda_init)
    for st in range(q_ref.shape[1] // strip):
        o = _softmax_out(acc_ref[2 * st]) - lam * _softmax_out(acc_ref[2 * st + 1])
        o = o * lax.rsqrt(jnp.mean(o * o, axis=0, keepdims=True) + EPS) * gain_ref[...]
        o_ref[0, pl.ds(st * strip, strip), :] = (o * (1.0 - lambda_init)).T.astype(BF16)


def _diff_attention(q, k, vt, lq, lam_p, gain_col, lambda_init):
    b, lk, _ = k.shape
    tq = min(DA_TQ, lq)
    strip = min(ATT_STRIP, tq)
    qs = pl.BlockSpec((1, tq, LANE), lambda bi, h, i: (bi, i, h))
    ks = pl.BlockSpec((1, lk, LANE), lambda bi, h, i: (bi, 0, h))
    vs = pl.BlockSpec((1, 1) + vt.shape[2:], lambda bi, h, i: (bi, h, 0, 0, 0))
    nchain = 2 * (tq // strip)
    return pl.pallas_call(
        functools.partial(_da_kernel, strip=strip, lambda_init=lambda_init),
        grid=(b, DA_HEADS, lq // tq),
        in_specs=[qs, ks, vs, _resident(lam_p.shape), _resident(gain_col.shape)],
        out_specs=qs,
        out_shape=jax.ShapeDtypeStruct((b, lq, q.shape[2]), BF16),
        scratch_shapes=_flash_scratch(nchain, strip, ATT_TK, LANE),
        compiler_params=_params("parallel", "parallel", "parallel"),
        name="diff_attention",
    )(q, k, vt, lam_p, gain_col)


def _mla_kernel(q_ref, k_ref, vt_ref, o_ref, *scratch, strip):
    _flash_loop(q_ref, k_ref, vt_ref, scratch, strip, False)
    acc_ref = scratch[1]
    for st in range(q_ref.shape[1] // strip):
        o_ref[0, pl.ds(st * strip, strip), :] = _softmax_out(acc_ref[st]).T.astype(BF16)


def _mla_attention(q, k, vt, lq):
    b, lk, _ = k.shape
    tq = min(MLA_TQ, lq)
    strip = min(ATT_STRIP, tq)
    qs = pl.BlockSpec((1, tq, MLA_QK_PAD), lambda bi, h, i: (bi, i, h))
    ks = pl.BlockSpec((1, lk, MLA_QK_PAD), lambda bi, h, i: (bi, 0, h))
    vs = pl.BlockSpec((1, 1) + vt.shape[2:], lambda bi, h, i: (bi, h, 0, 0, 0))
    nchain = tq // strip
    return pl.pallas_call(
        functools.partial(_mla_kernel, strip=strip),
        grid=(b, MLA_HEADS, lq // tq),
        in_specs=[qs, ks, vs],
        out_specs=pl.BlockSpec((1, tq, MLA_V), lambda bi, h, i: (bi, i, h)),
        out_shape=jax.ShapeDtypeStruct((b, lq, MLA_HEADS * MLA_V), BF16),
        scratch_shapes=_flash_scratch(nchain, strip, ATT_TK, MLA_QK_PAD),
        compiler_params=_params("parallel", "parallel", "parallel"),
        name="mla_attention",
    )(q, k, vt)


def _scan_rows(gates_t, index, chunks):
    b, _, l = gates_t.shape
    idx = jnp.asarray(index, jnp.int32)
    rows = gates_t[:, idx, :]
    rows = rows.reshape(b, idx.shape[0], idx.shape[1], chunks, l // chunks)
    return jnp.swapaxes(rows, 2, 3)


def _even_layer(x, ctx, mod_x, mod_c, p):
    b = x.shape[0]
    nh, ns = DN_HEADS, SSD_HEADS
    dn_index = [[d * nh + h for h in range(nh)] + [2 * nh + d * nh + h for h in range(nh)] for d in range(2)]
    dt0, a0 = 4 * nh, 4 * nh + 2 * ns
    ssd_index = [[dt0 + g * SSD_GRP + r for r in range(SSD_GRP)] + [a0 + g * SSD_GRP + r for r in range(SSD_GRP)]
                 + [dt0 + ns + g * SSD_GRP + r for r in range(SSD_GRP)] + [a0 + ns + g * SSD_GRP + r for r in range(SSD_GRP)]
                 for g in range(SSD_GROUPS)]

    def prepare(t, modv):
        q, k, v, xs, bc, z, gates = _even_in(t, modv, p["gain_mix"], p["w_main"], p["w_small"], p["conv_w"], p["conv_b"],
                                             p["gate_bias"], p["gate_scale"])
        gt = jnp.swapaxes(gates, 1, 2)
        nchunk = t.shape[1] // CHUNK
        return dict(q=q, k=k, v=v, xs=xs, bc=bc, z=z, dn_rows=jnp.swapaxes(_scan_rows(gt, dn_index, nchunk), 1, 2),
                    ssd_rows=jnp.swapaxes(_scan_rows(gt, ssd_index, nchunk), 1, 2))

    pc = prepare(ctx, mod_c)
    pL = prepare(x, mod_x)
    dn0 = jnp.zeros((b, 2 * nh, DN_DK, DN_DV), F32)
    ssd0 = jnp.zeros((b, SSD_GROUPS, 2, SSD_STATE, SSD_GRP * SSD_HEAD_DIM), F32)
    ocf, ocb, dn1 = _dn_scan(pc["q"], pc["k"], pc["v"], pc["dn_rows"], dn0)
    olf, olb, _ = _dn_scan(pL["q"], pL["k"], pL["v"], pL["dn_rows"], dn1)
    ycf, ycb, ssd1 = _ssd_scan(pc["xs"], pc["bc"], pc["ssd_rows"], ssd0, p["expand"])
    ylf, ylb, _ = _ssd_scan(pL["xs"], pL["bc"], pL["ssd_rows"], ssd1, p["expand"])

    def finish(t, modv, pp, of, ob, yf, yb):
        a, bm = _even_merge(of, ob, yf, yb, pp["xs"], pp["z"], p["dn_gain"], p["ssd_d"], p["ssd_gain"])
        return _mix_ffn(t, a, bm, modv, p["gain_ffn"], p["w_out"], p["ffn_w_in"], p["ffn_w_out"])

    return finish(x, mod_x, pL, olf, olb, ylf, ylb), finish(ctx, mod_c, pc, ocf, ocb, ycf, ycb)


def _rope_tables(n_tokens):
    lane = jnp.arange(LANE)
    quarter = (lane % 64) // 16
    n_freq = DA_DIM // 4
    inv_freq = ROPE_THETA ** (-(lane % 16).astype(F32) / n_freq)
    tok = jnp.arange(n_tokens)
    pos = jnp.where(quarter[None, :] < 2, (tok // GRID_W)[:, None], (tok % GRID_W)[:, None]).astype(F32)
    ang = pos * inv_freq[None, :]
    sign = jnp.where(quarter % 2 == 0, -1.0, 1.0).astype(F32)
    return jnp.cos(ang), jnp.sin(ang) * sign[None, :]


def _odd_layer(x, ctx, mod_x, mod_c, p, lambda_init):
    lq, lc = x.shape[1], ctx.shape[1]
    cos_t, sin_t = _rope_tables(lq)
    cos_t = jnp.concatenate([cos_t, jnp.ones((lc, LANE), F32)], axis=0)
    sin_t = jnp.concatenate([sin_t, jnp.zeros((lc, LANE), F32)], axis=0)
    mod2 = jnp.stack([mod_x, mod_c], axis=1)
    daq, dak, davt, mq, mk, mvt = _odd_in(x, ctx, mod2, p["gain_mix"], p["w_in"], p["q_a_gain"], p["w_uq"], p["kv_a_gain"],
                                          p["w_ukv"], p["dq_gain"], p["dk_gain"], p["mq_gain"], p["mk_gain"], cos_t, sin_t)
    da = _diff_attention(daq, dak, davt, lq, p["da_lambda"], p["sub_gain"], lambda_init)
    ml = _mla_attention(mq, mk, mvt, lq)
    return _mix_ffn(x, da, ml, mod_x, p["gain_ffn"], p["w_out"], p["ffn_w_in"], p["ffn_w_out"])


def _even_params(i, j, norm_mix, norm_ffn, ffn_w_in, ffn_w_out, ev_w_in, ev_conv_w, ev_conv_b, dn_a_log, dn_dt_bias,
                 dn_norm, ssd_a_log, ssd_dt_bias, ssd_d, ssd_norm, ev_w_out):
    d = norm_mix.shape[1]
    w = ev_w_in[j]
    small = w[:, EV_CONV + EV_GATE:]
    nh, ns = DN_HEADS, SSD_HEADS
    w_small = jnp.concatenate([small, small[:, 4 * nh:], jnp.zeros((d, LANE - 4 * nh - 4 * ns), F32)], axis=1)
    zeros = lambda n: jnp.zeros((n,), F32)
    gate_bias = jnp.concatenate([zeros(2 * nh), dn_dt_bias[j].reshape(-1), ssd_dt_bias[j].reshape(-1),
                                 ssd_dt_bias[j].reshape(-1), zeros(LANE - 4 * nh - 4 * ns)])
    gate_scale = jnp.concatenate([jnp.ones((2 * nh,), F32), -jnp.exp(dn_a_log[j].reshape(-1)), jnp.ones((2 * ns,), F32),
                                  -jnp.exp(ssd_a_log[j].reshape(-1)), zeros(LANE - 4 * nh - 4 * ns)])
    expand = jnp.repeat(jnp.eye(SSD_GRP, dtype=F32), SSD_HEAD_DIM, axis=1)
    return dict(
        gain_mix=norm_mix[i].reshape(1, d), gain_ffn=norm_ffn[i].reshape(1, d),
        w_main=w[:, :EV_CONV + EV_GATE].astype(BF16), w_small=w_small,
        conv_w=ev_conv_w[j], conv_b=ev_conv_b[j].reshape(1, -1),
        gate_bias=gate_bias.reshape(1, LANE), gate_scale=gate_scale.reshape(1, LANE), expand=expand,
        dn_gain=dn_norm[j].reshape(1, DN_DV), ssd_d=jnp.repeat(ssd_d[j], SSD_HEAD_DIM).reshape(1, SSD_INNER),
        ssd_gain=ssd_norm[j].reshape(1, SSD_INNER), w_out=ev_w_out[j].astype(BF16),
        ffn_w_in=ffn_w_in[i].astype(BF16), ffn_w_out=ffn_w_out[i].astype(BF16))


def _odd_params(i, j, norm_mix, norm_ffn, ffn_w_in, ffn_w_out, od_w_in, da_q_norm, da_k_norm, da_lambda, da_sub_norm,
                mla_q_a_norm, mla_w_uq, mla_kv_a_norm, mla_w_ukv, mla_q_norm, mla_k_norm, od_w_out):
    d = norm_mix.shape[1]
    w = od_w_in[j]
    w_in = jnp.concatenate([w, jnp.zeros((d, LANE - MLA_ROPE), F32)], axis=1).astype(BF16)
    hq = mla_w_uq[j].reshape(MLA_Q_RANK, MLA_HEADS, MLA_NOPE + MLA_ROPE)
    hq = jnp.pad(hq, ((0, 0), (0, 0), (0, MLA_QK_PAD - MLA_NOPE - MLA_ROPE)))
    hkv = mla_w_ukv[j].reshape(MLA_KV_RANK, MLA_HEADS, MLA_NOPE + MLA_V)
    w_ukv = jnp.concatenate([hkv[:, :, :MLA_NOPE].reshape(MLA_KV_RANK, -1), hkv[:, :, MLA_NOPE:].reshape(MLA_KV_RANK, -1)], axis=1)
    pad_gain = lambda g: jnp.pad(g, (0, MLA_QK_PAD - MLA_NOPE - MLA_ROPE)).reshape(1, MLA_QK_PAD)
    return dict(
        gain_mix=norm_mix[i].reshape(1, d), gain_ffn=norm_ffn[i].reshape(1, d), w_in=w_in,
        q_a_gain=mla_q_a_norm[j].reshape(1, -1), w_uq=hq.reshape(MLA_Q_RANK, -1).astype(BF16),
        kv_a_gain=mla_kv_a_norm[j].reshape(1, -1), w_ukv=w_ukv.astype(BF16),
        dq_gain=jnp.tile(da_q_norm[j], 2).reshape(1, LANE), dk_gain=jnp.tile(da_k_norm[j], 2).reshape(1, LANE),
        mq_gain=pad_gain(mla_q_norm[j]), mk_gain=pad_gain(mla_k_norm[j]),
        da_lambda=jnp.pad(da_lambda[j], ((0, 4), (0, LANE - DA_DIM))), sub_gain=da_sub_norm[j].reshape(2 * DA_DIM, 1),
        w_out=od_w_out[j].astype(BF16), ffn_w_in=ffn_w_in[i].astype(BF16), ffn_w_out=ffn_w_out[i].astype(BF16))


def kernel(x, c, ctx, c_ctx, ada_w, ada_b, norm_mix, norm_ffn, ffn_w_in, ffn_w_out, ev_w_in, ev_conv_w, ev_conv_b, dn_a_log, dn_dt_bias, dn_norm, ssd_a_log, ssd_dt_bias, ssd_d, ssd_norm, ev_w_out, od_w_in, da_q_norm, da_k_norm, da_lambda, da_sub_norm, mla_q_a_norm, mla_w_uq, mla_kv_a_norm, mla_w_ukv, mla_q_norm, mla_k_norm, od_w_out):
    b, _, d = x.shape
    depth = ada_w.shape[0]
    assert b < 8
    cvec = jnp.concatenate([c, c_ctx[None, :], jnp.zeros((8 - b - 1, d), F32)], axis=0)
    mod = _modulation(cvec, ada_w, ada_b)
    for i in range(depth):
        last = i == depth - 1
        j = i // 2
        mod_x = mod[i, :b].reshape(b, 1, 6 * d)
        mod_c = jnp.broadcast_to(mod[i, b].reshape(1, 1, 6 * d), (b, 1, 6 * d))
        if i % 2 == 0:
            p = _even_params(i, j, norm_mix, norm_ffn, ffn_w_in, ffn_w_out, ev_w_in, ev_conv_w, ev_conv_b, dn_a_log,
                             dn_dt_bias, dn_norm, ssd_a_log, ssd_dt_bias, ssd_d, ssd_norm, ev_w_out)
            x, ctx_new = _even_layer(x, ctx, mod_x, mod_c, p)
        else:
            p = _odd_params(i, j, norm_mix, norm_ffn, ffn_w_in, ffn_w_out, od_w_in, da_q_norm, da_k_norm, da_lambda,
                            da_sub_norm, mla_q_a_norm, mla_w_uq, mla_kv_a_norm, mla_w_ukv, mla_q_norm, mla_k_norm, od_w_out)
            lambda_init = 0.8 - 0.6 * math.exp(-0.3 * i)
            if last:
                x = _odd_layer(x, ctx, mod_x, mod_c, p, lambda_init)
                ctx_new = ctx
            else:
                raise NotImplementedError("context update after an attention layer is not needed for depth 2")
        ctx = ctx_new
    return x
```

```python
import functools
import math

import jax
import jax.numpy as jnp
from jax import lax
from jax.experimental import pallas as pl
from jax.experimental.pallas import tpu as pltpu

F32 = jnp.float32
BF16 = jnp.bfloat16

GRID_W = 64
EPS = 1e-6
ROPE_THETA = 10000.0
CONV_K = 5
DN_HEADS = 8
DN_DK = 128
DN_DV = 128
SSD_HEADS = 16
SSD_HEAD_DIM = 64
SSD_GROUPS = 2
SSD_STATE = 128
SSD_GRP = SSD_HEADS // SSD_GROUPS
SSD_INNER = SSD_HEADS * SSD_HEAD_DIM
DA_HEADS = 8
DA_DIM = 64
MLA_HEADS = 8
MLA_Q_RANK = 512
MLA_KV_RANK = 256
MLA_NOPE = 128
MLA_ROPE = 64
MLA_V = 128
MLA_QK_PAD = 256

DN_QK = DN_HEADS * DN_DK
DN_VW = DN_HEADS * DN_DV
SSD_BC = SSD_GROUPS * SSD_STATE
EV_CONV = 2 * DN_QK + DN_VW + SSD_INNER + 2 * SSD_BC
EV_GATE = DN_VW + SSD_INNER
DA_QK = DA_HEADS * 2 * DA_DIM
DA_VW = DA_HEADS * 2 * DA_DIM

LANE = 128
BF16_ROWS = 16
CHUNK = 128
VMEM_LIMIT = 56 * 1024 * 1024

ROW_TILE = 512
ODD_TILE = 256
SCAN_TILE = 512
DN_TILE = 256
DA_TQ = 2048
MLA_TQ = 2048
ATT_TK = 256
ATT_STRIP = 256
V_ROWS = MLA_V + BF16_ROWS
LOG2E = 1.4426950408889634


def _dot(a, b):
    return jnp.dot(a, b, preferred_element_type=F32)


def _dot_nt(a, b):
    return lax.dot_general(a, b, (((1,), (1,)), ((), ())), preferred_element_type=F32)


def _dot_tn(a, b):
    return lax.dot_general(a, b, (((0,), (0,)), ((), ())), preferred_element_type=F32)


def _sigmoid(x):
    return 1.0 / (1.0 + jnp.exp(-x))


def _silu(x):
    return x * _sigmoid(x)


def _softplus(x):
    return jnp.maximum(x, 0.0) + jnp.log(1.0 + jnp.exp(-jnp.abs(x)))


def _params(*sem):
    return pltpu.CompilerParams(dimension_semantics=sem, vmem_limit_bytes=VMEM_LIMIT)


def _resident(shape):
    nd = len(shape)
    return pl.BlockSpec(shape, lambda *_: (0,) * nd, pipeline_mode=pl.Buffered(1))


def _norm_mod(x, gain, shift, scale):
    y = x * lax.rsqrt(jnp.mean(x * x, axis=-1, keepdims=True) + EPS)
    return (y * gain) * (1.0 + scale) + shift


def _mod_kernel(c_ref, w_ref, b_ref, o_ref):
    o_ref[0] = _dot(_silu(c_ref[...]), w_ref[0]) + b_ref[0]


def _modulation(cvec, ada_w, ada_b):
    depth, d, n = ada_w.shape
    tn = n // 4
    return pl.pallas_call(
        _mod_kernel,
        grid=(depth, n // tn),
        in_specs=[
            pl.BlockSpec((8, d), lambda l, j: (0, 0)),
            pl.BlockSpec((1, d, tn), lambda l, j: (l, 0, j)),
            pl.BlockSpec((1, 1, tn), lambda l, j: (l, 0, j)),
        ],
        out_specs=pl.BlockSpec((1, 8, tn), lambda l, j: (l, 0, j)),
        out_shape=jax.ShapeDtypeStruct((depth, 8, n), F32),
        compiler_params=_params("parallel", "parallel"),
        name="modulation",
    )(cvec, ada_w, ada_b.reshape(depth, 1, n))


def _evin_kernel(xp_ref, xc_ref, xn_ref, mod_ref, gain_ref, w_ref, wsm_ref, cw_ref, cb_ref, gb_ref, gs_ref,
                 q_ref, k_ref, v_ref, xs_ref, bc_ref, z_ref, g_ref, hext_ref, pext_ref):
    i = pl.program_id(1)
    nb = pl.num_programs(1)
    tm = xc_ref.shape[1]
    d = xc_ref.shape[2]
    halo = BF16_ROWS
    gain = gain_ref[...]
    shift = mod_ref[0, :, 0:d]
    scale = mod_ref[0, :, d:2 * d]

    hc = _norm_mod(xc_ref[0], gain, shift, scale)
    hp = _norm_mod(xp_ref[0], gain, shift, scale)
    hn = _norm_mod(xn_ref[0], gain, shift, scale)
    hext_ref[0:halo, :] = jnp.where(i > 0, hp, 0.0).astype(BF16)
    hext_ref[halo:halo + tm, :] = hc.astype(BF16)
    hext_ref[halo + tm:, :] = jnp.where(i < nb - 1, hn, 0.0).astype(BF16)

    gw = pext_ref.shape[1]
    pad = CONV_K // 2
    plan = ((q_ref, 0, DN_QK, DN_DK ** -0.5), (k_ref, DN_QK, DN_QK, 1.0), (v_ref, 2 * DN_QK, DN_VW, None),
            (xs_ref, 2 * DN_QK + DN_VW, SSD_INNER, None), (bc_ref, 2 * DN_QK + DN_VW + SSD_INNER, 2 * SSD_BC, None))
    for out_ref, base, width, l2_scale in plan:
        for off in range(0, width, gw):
            c0 = base + off
            pext_ref[...] = _dot(hext_ref[...], w_ref[:, c0:c0 + gw])
            acc = cb_ref[:, c0:c0 + gw] + cw_ref[0:1, c0:c0 + gw] * pext_ref[halo - pad:halo - pad + tm, :]
            for t in range(1, CONV_K):
                acc = acc + cw_ref[t:t + 1, c0:c0 + gw] * pext_ref[halo - pad + t:halo - pad + t + tm, :]
            u = _silu(acc)
            if l2_scale is None:
                out_ref[0, :, off:off + gw] = u
            else:
                for s in range(0, gw, LANE):
                    uh = u[:, s:s + LANE]
                    inv = lax.rsqrt(jnp.sum(uh * uh, axis=-1, keepdims=True) + EPS)
                    out_ref[0, :, off + s:off + s + LANE] = uh * (inv * l2_scale)

    hcb = hext_ref[halo:halo + tm, :]
    for off in range(0, EV_GATE, gw):
        z_ref[0, :, off:off + gw] = _dot(hcb, w_ref[:, EV_CONV + off:EV_CONV + off + gw]).astype(z_ref.dtype)

    p = _dot(hc, wsm_ref[...]) + gb_ref[...]
    lane = lax.broadcasted_iota(jnp.int32, p.shape, 1)
    g_ref[0] = jnp.where(lane < 2 * DN_HEADS, _sigmoid(p), _softplus(p)) * gs_ref[...]


def _even_in(x, modv, gain, w_main, w_small, conv_w, conv_b, gate_bias, gate_scale):
    b, l, d = x.shape
    tm = min(ROW_TILE, l)
    nb = l // tm
    hb = tm // BF16_ROWS
    gw = 512
    row = lambda c: pl.BlockSpec((1, tm, c), lambda bi, i: (bi, i, 0))
    outs = (DN_QK, DN_QK, DN_VW, SSD_INNER, 2 * SSD_BC, EV_GATE, LANE)
    return pl.pallas_call(
        _evin_kernel,
        grid=(b, nb),
        in_specs=[
            pl.BlockSpec((1, BF16_ROWS, d), lambda bi, i: (bi, jnp.maximum(i * hb - 1, 0), 0)),
            row(d),
            pl.BlockSpec((1, BF16_ROWS, d), lambda bi, i: (bi, jnp.minimum((i + 1) * hb, l // BF16_ROWS - 1), 0)),
            pl.BlockSpec((1, 1, modv.shape[2]), lambda bi, i: (bi, 0, 0)),
            _resident(gain.shape), _resident(w_main.shape), _resident(w_small.shape), _resident(conv_w.shape),
            _resident(conv_b.shape), _resident(gate_bias.shape), _resident(gate_scale.shape),
        ],
        out_specs=[row(c) for c in outs],
        out_shape=[jax.ShapeDtypeStruct((b, l, c), BF16 if c == EV_GATE else F32) for c in outs],
        scratch_shapes=[pltpu.VMEM((tm + 2 * BF16_ROWS, d), BF16), pltpu.VMEM((tm + 2 * BF16_ROWS, gw), F32)],
        compiler_params=_params("parallel", "parallel"),
        name="even_in",
    )(x, x, x, modv, gain, w_main, w_small, conv_w, conv_b, gate_bias, gate_scale)


def _tri_masks(c):
    r = lax.broadcasted_iota(jnp.int32, (c, c), 0)
    s = lax.broadcasted_iota(jnp.int32, (c, c), 1)
    return r, s


TRI_BASE = 16


def _block_masks(r, s):
    same = lambda n: (r // n) == (s // n)
    masks = [same(TRI_BASE).astype(F32)]
    n = TRI_BASE
    while n < CHUNK:
        n *= 2
        masks.append(jnp.where(same(n), 1.0, 0.0) - jnp.where(same(n // 2), 1.0, 0.0))
    return masks


def _unit_tri_inverses(a, eye, blocks):
    p = [-(a_n * blocks[0]) for a_n in a]
    x = [eye + p_n for p_n in p]
    for _ in range(int(math.log2(TRI_BASE)) - 1):
        p = [_dot(p_n, p_n) for p_n in p]
        x = [x_n + _dot(x_n, p_n) for x_n, p_n in zip(x, p)]
    for m in blocks[1:]:
        e = [_dot(a_n * m, x_n) for a_n, x_n in zip(a, x)]
        x = [x_n - _dot(x_n, e_n) for x_n, e_n in zip(x, e)]
    return x


def _dn_chunks(q, k, v, gc, beta, crow, total, state, incl, strict, eye, blocks):
    n = range(len(q))
    c = q[0].shape[0]
    decay = [jnp.where(incl[i], jnp.exp(gc[i] - crow[i]), 0.0) for i in n]
    kb = [k[i] * beta[i] for i in n]
    a = [jnp.where(strict[i], _dot_nt(kb[i], k[i]) * decay[i], 0.0) for i in n]
    t = _unit_tri_inverses(a, eye, blocks)
    egc = [jnp.exp(gc[i]) for i in n]
    uw = [_dot(t[i], jnp.concatenate([v[i] * beta[i], kb[i] * egc[i]], axis=1)) for i in n]
    ws_qs = [_dot(jnp.concatenate([uw[i][:, DN_DV:], q[i] * egc[i]], axis=0), state[i]) for i in n]
    v_new = [uw[i][:, 0:DN_DV] - ws_qs[i][0:c] for i in n]
    attn = [_dot_nt(q[i], k[i]) * decay[i] for i in n]
    o = [ws_qs[i][c:2 * c] + _dot(attn[i], v_new[i]) for i in n]
    state = [state[i] * jnp.exp(total[i]) + _dot_tn(k[i] * jnp.exp(total[i] - gc[i]), v_new[i]) for i in n]
    return o, state


def _dn_kernel(qf_ref, kf_ref, vf_ref, gf_ref, qb_ref, kb_ref, vb_ref, gb_ref, s0_ref,
               of_ref, ob_ref, sfin_ref, s_ref):
    i = pl.program_id(1)
    nck = gf_ref.shape[1]
    c = CHUNK
    nh = DN_HEADS

    @pl.when(i == 0)
    def _():
        s_ref[...] = s0_ref[0]

    r, s = _tri_masks(c)
    eye = (r == s).astype(F32)
    blocks = _block_masks(r, s)
    dirs = ((qf_ref, kf_ref, vf_ref, gf_ref, of_ref, r >= s, r > s, c - 1),
            (qb_ref, kb_ref, vb_ref, gb_ref, ob_ref, r <= s, r < s, 0))

    def body(ci, carry):
        q, k, v, gc, beta, crow, total, state, incl, strict, dest = ([] for _ in range(11))
        for d, (q_ref, k_ref, v_ref, g_ref, o_ref, inc, stri, tot_row) in enumerate(dirs):
            cc = ci if d == 0 else nck - 1 - ci
            rows = pl.ds(pl.multiple_of(cc * c, c), c)
            inclf = inc.astype(F32)
            grow = g_ref[0, cc, d]
            cum_rows = _dot_nt(grow, inclf)
            cols = _dot_nt(jnp.concatenate([inclf, eye], axis=0), grow)
            for h in range(nh):
                lanes = slice(h * LANE, (h + 1) * LANE)
                q.append(q_ref[0, rows, lanes])
                k.append(k_ref[0, rows, lanes])
                v.append(v_ref[0, rows, lanes])
                gc.append(cols[0:c, nh + h:nh + h + 1])
                beta.append(cols[c:2 * c, h:h + 1])
                crow.append(cum_rows[nh + h:nh + h + 1, :])
                total.append(cols[tot_row:tot_row + 1, nh + h:nh + h + 1])
                state.append(s_ref[d * nh + h])
                incl.append(inc)
                strict.append(stri)
                dest.append((o_ref, rows, lanes))
        o, state = _dn_chunks(q, k, v, gc, beta, crow, total, state, incl, strict, eye, blocks)
        for n, (o_ref, rows, lanes) in enumerate(dest):
            o_ref[0, rows, lanes] = o[n].astype(o_ref.dtype)
            s_ref[n] = state[n]
        return carry

    lax.fori_loop(0, nck, body, 0)

    @pl.when(i == pl.num_programs(1) - 1)
    def _():
        sfin_ref[0] = s_ref[...]


def _dn_scan(q, k, v, grows, state0):
    b, l, width = q.shape
    tl = min(DN_TILE, l)
    nb = l // tl
    nck = tl // CHUNK
    fwd = pl.BlockSpec((1, tl, width), lambda bi, i: (bi, i, 0))
    bwd = pl.BlockSpec((1, tl, width), lambda bi, i: (bi, nb - 1 - i, 0))
    gfwd = pl.BlockSpec((1, nck) + grows.shape[2:], lambda bi, i: (bi, i, 0, 0, 0))
    gbwd = pl.BlockSpec((1, nck) + grows.shape[2:], lambda bi, i: (bi, nb - 1 - i, 0, 0, 0))
    st = pl.BlockSpec((1,) + state0.shape[1:], lambda bi, i: (bi, 0, 0, 0))
    return pl.pallas_call(
        _dn_kernel,
        grid=(b, nb),
        in_specs=[fwd, fwd, fwd, gfwd, bwd, bwd, bwd, gbwd, st],
        out_specs=[fwd, bwd, st],
        out_shape=[jax.ShapeDtypeStruct(q.shape, BF16), jax.ShapeDtypeStruct(q.shape, BF16),
                   jax.ShapeDtypeStruct(state0.shape, F32)],
        scratch_shapes=[pltpu.VMEM(state0.shape[1:], F32)],
        compiler_params=_params("parallel", "arbitrary"),
        name="deltanet_scan",
    )(q, k, v, grows, q, k, v, grows, state0)


def _ssd_chunks(x, bm, cm, rows, state, expand, incl, eye, first_row):
    n = range(len(x))
    c = x[0].shape[0]
    nh = SSD_GRP
    p = SSD_HEAD_DIM
    inclf = [incl[i].astype(F32) for i in n]
    cum_row = [_dot_nt(rows[i][nh:2 * nh], inclf[i]) for i in n]
    cols = [_dot_nt(jnp.concatenate([inclf[i], eye], axis=0), rows[i]) for i in n]
    cum = [cols[i][0:c, nh:2 * nh] for i in n]
    total = [cum[i][first_row[i]:first_row[i] + 1, :] for i in n]
    xdt = [x[i] * _dot(cols[i][c:2 * c, 0:nh], expand) for i in n]
    cb = [_dot_nt(cm[i], bm[i]) for i in n]
    from_state = [_dot(cm[i], state[i]) * _dot(jnp.exp(cum[i]), expand) for i in n]
    parts = [[] for _ in n]
    for h in range(nh):
        seg = [jnp.where(incl[i], jnp.exp(cum[i][:, h:h + 1] - cum_row[i][h:h + 1, :]), 0.0) for i in n]
        for i in n:
            parts[i].append(_dot(cb[i] * seg[i], xdt[i][:, h * p:(h + 1) * p]))
    y = [jnp.concatenate(parts[i], axis=1) + from_state[i] for i in n]
    carry_in = [xdt[i] * _dot(jnp.exp(total[i] - cum[i]), expand) for i in n]
    state = [state[i] * _dot(jnp.exp(total[i]), expand) + _dot_tn(bm[i], carry_in[i]) for i in n]
    return y, state


def _ssd_kernel(xf_ref, bcf_ref, gf_ref, xb_ref, bcb_ref, gb_ref, s0_ref, e_ref, yf_ref, yb_ref, sfin_ref, s_ref):
    i = pl.program_id(1)
    nck = gf_ref.shape[1]
    c = CHUNK
    nh = SSD_GRP
    ng = SSD_GROUPS
    gwid = nh * SSD_HEAD_DIM

    @pl.when(i == 0)
    def _():
        s_ref[...] = s0_ref[0]

    r, s = _tri_masks(c)
    eye = (r == s).astype(F32)
    expand = e_ref[...]
    dirs = ((xf_ref, bcf_ref, gf_ref, yf_ref, r >= s, c - 1), (xb_ref, bcb_ref, gb_ref, yb_ref, r <= s, 0))

    def body(ci, carry):
        x, bm, cm, rows, state, incl, first, dest = ([] for _ in range(8))
        for d, (x_ref, bc_ref, g_ref, y_ref, inc, tot_row) in enumerate(dirs):
            cc = ci if d == 0 else nck - 1 - ci
            tok = pl.ds(pl.multiple_of(cc * c, c), c)
            for g in range(ng):
                lanes = slice(g * gwid, (g + 1) * gwid)
                x.append(x_ref[0, tok, lanes])
                bm.append(bc_ref[0, tok, g * SSD_STATE:(g + 1) * SSD_STATE])
                cm.append(bc_ref[0, tok, (ng + g) * SSD_STATE:(ng + g + 1) * SSD_STATE])
                rows.append(g_ref[0, cc, g][2 * nh * d:2 * nh * (d + 1)])
                state.append(s_ref[g, d])
                incl.append(inc)
                first.append(tot_row)
                dest.append((y_ref, tok, lanes, g, d))
        y, state = _ssd_chunks(x, bm, cm, rows, state, expand, incl, eye, first)
        for n, (y_ref, tok, lanes, g, d) in enumerate(dest):
            y_ref[0, tok, lanes] = y[n].astype(y_ref.dtype)
            s_ref[g, d] = state[n]
        return carry

    lax.fori_loop(0, nck, body, 0)

    @pl.when(i == pl.num_programs(1) - 1)
    def _():
        sfin_ref[0] = s_ref[...]


def _ssd_scan(xs, bc, grows, state0, expand):
    b, l, width = xs.shape
    tl = min(SCAN_TILE, l)
    nb = l // tl
    nck = tl // CHUNK
    fwd = lambda w: pl.BlockSpec((1, tl, w), lambda bi, i: (bi, i, 0))
    bwd = lambda w: pl.BlockSpec((1, tl, w), lambda bi, i: (bi, nb - 1 - i, 0))
    gf = pl.BlockSpec((1, nck) + grows.shape[2:], lambda bi, i: (bi, i, 0, 0, 0))
    gb = pl.BlockSpec((1, nck) + grows.shape[2:], lambda bi, i: (bi, nb - 1 - i, 0, 0, 0))
    st = pl.BlockSpec((1,) + state0.shape[1:], lambda bi, i: (bi, 0, 0, 0, 0))
    return pl.pallas_call(
        _ssd_kernel,
        grid=(b, nb),
        in_specs=[fwd(width), fwd(bc.shape[2]), gf, bwd(width), bwd(bc.shape[2]), gb, st, _resident(expand.shape)],
        out_specs=[fwd(width), bwd(width), st],
        out_shape=[jax.ShapeDtypeStruct(xs.shape, BF16), jax.ShapeDtypeStruct(xs.shape, BF16),
                   jax.ShapeDtypeStruct(state0.shape, F32)],
        scratch_shapes=[pltpu.VMEM(state0.shape[1:], F32)],
        compiler_params=_params("parallel", "arbitrary"),
        name="ssd_scan",
    )(xs, bc, grows, xs, bc, grows, state0, expand)


def _evmerge_kernel(of_ref, ob_ref, yf_ref, yb_ref, xs_ref, z_ref, dng_ref, sd_ref, sg_ref, a_ref, b_ref):
    f32 = lambda ref, lo, width: ref[0, :, lo:lo + width].astype(F32)
    for s in range(0, DN_VW, LANE):
        o = f32(of_ref, s, LANE) + f32(ob_ref, s, LANE)
        o = o * lax.rsqrt(jnp.mean(o * o, axis=-1, keepdims=True) + EPS) * dng_ref[...]
        a_ref[0, :, s:s + LANE] = (o * _silu(f32(z_ref, s, LANE))).astype(BF16)
    gwid = SSD_INNER // SSD_GROUPS
    for s in range(0, SSD_INNER, gwid):
        y = f32(yf_ref, s, gwid) + f32(yb_ref, s, gwid) + sd_ref[:, s:s + gwid] * xs_ref[0, :, s:s + gwid]
        y = y * _silu(f32(z_ref, DN_VW + s, gwid))
        y = y * lax.rsqrt(jnp.mean(y * y, axis=-1, keepdims=True) + EPS) * sg_ref[:, s:s + gwid]
        b_ref[0, :, s:s + gwid] = y.astype(BF16)


def _even_merge(o_f, o_b, y_f, y_b, xs, z, dn_gain, ssd_d, ssd_gain):
    b, l, _ = o_f.shape
    tm = min(ROW_TILE, l)
    row = lambda c: pl.BlockSpec((1, tm, c), lambda bi, i: (bi, i, 0))
    return pl.pallas_call(
        _evmerge_kernel,
        grid=(b, l // tm),
        in_specs=[row(DN_VW), row(DN_VW), row(SSD_INNER), row(SSD_INNER), row(SSD_INNER), row(EV_GATE),
                  _resident(dn_gain.shape), _resident(ssd_d.shape), _resident(ssd_gain.shape)],
        out_specs=[row(DN_VW), row(SSD_INNER)],
        out_shape=[jax.ShapeDtypeStruct((b, l, DN_VW), BF16), jax.ShapeDtypeStruct((b, l, SSD_INNER), BF16)],
        compiler_params=_params("parallel", "parallel"),
        name="even_merge",
    )(o_f, o_b, y_f, y_b, xs, z, dn_gain, ssd_d, ssd_gain)


def _mixffn_kernel(x_ref, a_ref, b_ref, mod_ref, gain_ref, wo_ref, wi_ref, wf_ref, o_ref, act_ref):
    d = x_ref.shape[2]
    na = a_ref.shape[2]
    hid = wf_ref.shape[0]
    g1 = mod_ref[0, :, 2 * d:3 * d]
    shift = mod_ref[0, :, 3 * d:4 * d]
    scale = mod_ref[0, :, 4 * d:5 * d]
    g2 = mod_ref[0, :, 5 * d:6 * d]
    mixed = _dot(a_ref[0], wo_ref[0:na, :]) + _dot(b_ref[0], wo_ref[na:, :])
    x1 = x_ref[0] + g1 * mixed
    h = _norm_mod(x1, gain_ref[...], shift, scale).astype(BF16)
    half = hid // 2
    for s in range(0, hid, half):
        gate = _dot(h, wi_ref[:, s:s + half])
        up = _dot(h, wi_ref[:, hid + s:hid + s + half])
        act_ref[:, s:s + half] = (_silu(gate) * up).astype(BF16)
    o_ref[0] = x1 + g2 * _dot(act_ref[...], wf_ref[...])


def _mix_ffn(x, a, bmix, modv, gain, w_out, w_in, w_ffn_out):
    b, l, d = x.shape
    tm = min(ROW_TILE, l)
    hid = w_ffn_out.shape[0]
    row = lambda c: pl.BlockSpec((1, tm, c), lambda bi, i: (bi, i, 0))
    return pl.pallas_call(
        _mixffn_kernel,
        grid=(b, l // tm),
        in_specs=[row(d), row(a.shape[2]), row(bmix.shape[2]),
                  pl.BlockSpec((1, 1, modv.shape[2]), lambda bi, i: (bi, 0, 0)),
                  _resident(gain.shape), _resident(w_out.shape), _resident(w_in.shape), _resident(w_ffn_out.shape)],
        out_specs=row(d),
        out_shape=jax.ShapeDtypeStruct(x.shape, F32),
        scratch_shapes=[pltpu.VMEM((tm, hid), BF16)],
        compiler_params=_params("parallel", "parallel"),
        name="mix_ffn",
    )(x, a, bmix, modv, gain, w_out, w_in, w_ffn_out)


def _rope(x, cos, sin):
    lane = lax.broadcasted_iota(jnp.int32, x.shape, 1)
    partner = jnp.where(lane % 32 < 16, pltpu.roll(x, LANE - 16, 1), pltpu.roll(x, 16, 1))
    return x * cos + partner * sin


def _odin_kernel(x_ref, c_ref, mod_ref, gain_ref, w_ref, qan_ref, wuq_ref, kvan_ref, wukv_ref, dqn_ref, dkn_ref,
                 mqn_ref, mkn_ref, nm_ref, cos_ref, sin_ref, daq_ref, dak_ref, davt_ref, mq_ref, mk_ref, mvt_ref,
                 *, n_lat):
    d = x_ref.shape[2]
    xin = jnp.where(pl.program_id(1) < n_lat, x_ref[0], c_ref[0])
    h = _norm_mod(xin, gain_ref[...], mod_ref[0, 0, :, 0:d], mod_ref[0, 0, :, d:2 * d]).astype(BF16)
    cos = cos_ref[...]
    sin = sin_ref[...]
    ones = jnp.ones((V_ROWS - MLA_V, ATT_TK), BF16)

    def store_values_t(out_ref, vals):
        for hd in range(vals.shape[1] // LANE):
            for t in range(vals.shape[0] // ATT_TK):
                tile = vals[t * ATT_TK:(t + 1) * ATT_TK, hd * LANE:(hd + 1) * LANE]
                out_ref[0, hd, t, 0:MLA_V, :] = tile.T.astype(BF16)
                out_ref[0, hd, t, MLA_V:, :] = ones

    def slab_norm(u, g, kind):
        return u * lax.rsqrt(_dot(u * u, nm_ref[kind]) + EPS) * g

    pair_norm = functools.partial(slab_norm, kind=0)
    full_norm = functools.partial(slab_norm, kind=1)
    low_norm = functools.partial(slab_norm, kind=2)

    def da_part(out_ref, base, g, out_scale):
        p = _dot(h, w_ref[:, base:base + DA_QK])
        for s in range(0, DA_QK, LANE):
            u = _rope(pair_norm(p[:, s:s + LANE], g), cos, sin)
            out_ref[0, :, s:s + LANE] = (u * out_scale).astype(BF16)

    da_part(daq_ref, 0, dqn_ref[...], DA_DIM ** -0.5 * LOG2E)
    da_part(dak_ref, DA_QK, dkn_ref[...], 1.0)
    store_values_t(davt_ref, _dot(h, w_ref[:, 2 * DA_QK:2 * DA_QK + DA_VW]))

    def wide_norm(u, g):
        return u * lax.rsqrt(jnp.mean(u * u, axis=-1, keepdims=True) + EPS) * g

    c0 = 2 * DA_QK + DA_VW
    cq = wide_norm(_dot(h, w_ref[:, c0:c0 + MLA_Q_RANK]), qan_ref[...]).astype(BF16)
    qf = _dot(cq, wuq_ref[...])
    scale = (MLA_NOPE + MLA_ROPE) ** -0.5 * LOG2E
    for hd in range(MLA_HEADS):
        s = hd * MLA_QK_PAD
        mq_ref[0, :, s:s + LANE] = (full_norm(qf[:, s:s + LANE], mqn_ref[:, 0:LANE]) * scale).astype(BF16)
        u = _rope(low_norm(qf[:, s + LANE:s + 2 * LANE], mqn_ref[:, LANE:2 * LANE]), cos, sin)
        mq_ref[0, :, s + LANE:s + 2 * LANE] = (u * scale).astype(BF16)
    c0 += MLA_Q_RANK
    ckv = wide_norm(_dot(h, w_ref[:, c0:c0 + MLA_KV_RANK]), kvan_ref[...]).astype(BF16)
    kvu = _dot(ckv, wukv_ref[...])
    c0 += MLA_KV_RANK
    kr = _rope(low_norm(_dot(h, w_ref[:, c0:c0 + LANE]), mkn_ref[:, LANE:2 * LANE]), cos, sin).astype(BF16)
    for hd in range(MLA_HEADS):
        s = hd * MLA_QK_PAD
        mk_ref[0, :, s:s + LANE] = full_norm(kvu[:, hd * LANE:(hd + 1) * LANE], mkn_ref[:, 0:LANE]).astype(BF16)
        mk_ref[0, :, s + LANE:s + 2 * LANE] = kr
    store_values_t(mvt_ref, kvu[:, MLA_HEADS * MLA_NOPE:])


def _odd_in(x, ctx, mod2, gain, w_in, q_a_gain, w_uq, kv_a_gain, w_ukv, dq_gain, dk_gain, mq_gain, mk_gain,
            cos_t, sin_t):
    b, l, d = x.shape
    lc = ctx.shape[1]
    tm = min(ODD_TILE, l, lc)
    n_lat, n_ctx = l // tm, lc // tm
    lk = l + lc
    tiles = tm // ATT_TK
    row = lambda c: pl.BlockSpec((1, tm, c), lambda bi, i: (bi, i, 0))
    vt = pl.BlockSpec((1, MLA_HEADS, tiles, V_ROWS, ATT_TK), lambda bi, i: (bi, 0, i, 0, 0))
    widths = (DA_QK, DA_QK, None, MLA_HEADS * MLA_QK_PAD, MLA_HEADS * MLA_QK_PAD, None)
    table = pl.BlockSpec((tm, LANE), lambda bi, i: (i, 0))
    li = jnp.arange(LANE)
    half = (li[:, None] // DA_DIM == li[None, :] // DA_DIM).astype(F32) / DA_DIM
    low = jnp.broadcast_to((li[:, None] < MLA_ROPE).astype(F32) / MLA_ROPE, (LANE, LANE))
    norm_mats = jnp.stack([half, jnp.full((LANE, LANE), 1.0 / LANE, F32), low])
    res = [gain, w_in, q_a_gain, w_uq, kv_a_gain, w_ukv, dq_gain, dk_gain, mq_gain, mk_gain, norm_mats]
    vt_shape = jax.ShapeDtypeStruct((b, MLA_HEADS, lk // ATT_TK, V_ROWS, ATT_TK), BF16)
    return pl.pallas_call(
        functools.partial(_odin_kernel, n_lat=n_lat),
        grid=(b, n_lat + n_ctx),
        in_specs=[pl.BlockSpec((1, tm, d), lambda bi, i: (bi, jnp.minimum(i, n_lat - 1), 0)),
                  pl.BlockSpec((1, tm, d), lambda bi, i: (bi, jnp.maximum(i - n_lat, 0), 0)),
                  pl.BlockSpec((1, 1, 1, mod2.shape[3]), lambda bi, i: (bi, i // n_lat, 0, 0))]
        + [_resident(a.shape) for a in res] + [table, table],
        out_specs=[vt if c is None else row(c) for c in widths],
        out_shape=[vt_shape if c is None else jax.ShapeDtypeStruct((b, lk, c), BF16) for c in widths],
        compiler_params=_params("parallel", "parallel"),
        name="odd_in",
    )(x, ctx, mod2, *res, cos_t, sin_t)


def _flash_scratch(nchain, strip, tk, qk_width):
    half = [pltpu.VMEM((nchain, tk, strip), F32), pltpu.VMEM((nchain, 1, strip), F32)]
    return [pltpu.VMEM((nchain, 1, strip), F32), pltpu.VMEM((nchain, V_ROWS, strip), F32),
            pltpu.VMEM((nchain, qk_width, strip), BF16)] + half + half


def _flash_loop(q_ref, k_ref, vt_ref, scratch, strip, split_q):
    m_ref, acc_ref, qz_ref, s0_ref, mx0_ref, s1_ref, mx1_ref = scratch
    s_ref, mx_ref = (s0_ref, s1_ref), (mx0_ref, mx1_ref)
    tq = q_ref.shape[1]
    tk = vt_ref.shape[4]
    nk = vt_ref.shape[2]
    nchain = m_ref.shape[0]
    m_ref[...] = jnp.full(m_ref.shape, -jnp.inf, F32)
    acc_ref[...] = jnp.zeros(acc_ref.shape, F32)
    row = lax.broadcasted_iota(jnp.int32, (q_ref.shape[2], strip), 0)
    for st in range(tq // strip):
        qt = q_ref[0, pl.ds(st * strip, strip), :].astype(F32).T
        if split_q:
            qz_ref[2 * st] = jnp.where(row < DA_DIM, qt, 0.0).astype(BF16)
            qz_ref[2 * st + 1] = jnp.where(row < DA_DIM, 0.0, qt).astype(BF16)
        else:
            qz_ref[st] = qt.astype(BF16)

    def scores(j, c):
        st = _dot(k_ref[0, j * tk:(j + 1) * tk, :], qz_ref[c])
        s_ref[j % 2][c] = st
        mx_ref[j % 2][c] = jnp.max(st, axis=0, keepdims=True)

    def weigh(j, c):
        m_old = m_ref[c]
        m_new = jnp.maximum(m_old, mx_ref[j % 2][c])
        p = jnp.exp2((s_ref[j % 2][c] - m_new).astype(BF16))
        m_ref[c] = m_new
        return c, jnp.exp2(m_old - m_new), _dot(vt_ref[0, 0, j], p)

    def accumulate(pending):
        c, alpha, pv = pending
        acc_ref[c] = alpha * acc_ref[c] + pv

    for c in range(nchain):
        scores(0, c)
    pending = None
    for j in range(nk):
        for c in range(nchain):
            if j + 1 < nk:
                scores(j + 1, c)
            done, pending = pending, weigh(j, c)
            if done is not None:
                accumulate(done)
    accumulate(pending)


def _softmax_out(acc):
    return acc[0:MLA_V] / acc[MLA_V:MLA_V + 1]


def _da_kernel(q_ref, k_ref, vt_ref, lam_ref, gain_ref, o_ref, *scratch, strip, lambda_init):
    _flash_loop(q_ref, k_ref, vt_ref, scratch, strip, True)
    acc_ref = scratch[1]
    lp = lam_ref[...]
    lam = (jnp.exp(jnp.sum(lp[0:1] * lp[1:2], axis=-1, keepdims=True))
           - jnp.exp(jnp.sum(lp[2:3] * lp[3:4], axis=-1, keepdims=True)) + lambda_init)
    for st in range(q_ref.shape[1] // strip):
        o = _softmax_out(acc_ref[2 * st]) - lam * _softmax_out(acc_ref[2 * st + 1])
        o = o * lax.rsqrt(jnp.mean(o * o, axis=0, keepdims=True) + EPS) * gain_ref[...]
        o_ref[0, pl.ds(st * strip, strip), :] = (o * (1.0 - lambda_init)).T.astype(BF16)


def _diff_attention(q, k, vt, lq, lam_p, gain_col, lambda_init):
    b, lk, _ = k.shape
    tq = min(DA_TQ, lq)
    strip = min(ATT_STRIP, tq)
    qs = pl.BlockSpec((1, tq, LANE), lambda bi, h, i: (bi, i, h))
    ks = pl.BlockSpec((1, lk, LANE), lambda bi, h, i: (bi, 0, h))
    vs = pl.BlockSpec((1, 1) + vt.shape[2:], lambda bi, h, i: (bi, h, 0, 0, 0))
    nchain = 2 * (tq // strip)
    return pl.pallas_call(
        functools.partial(_da_kernel, strip=strip, lambda_init=lambda_init),
        grid=(b, DA_HEADS, lq // tq),
        in_specs=[qs, ks, vs, _resident(lam_p.shape), _resident(gain_col.shape)],
        out_specs=qs,
        out_shape=jax.ShapeDtypeStruct((b, lq, q.shape[2]), BF16),
        scratch_shapes=_flash_scratch(nchain, strip, ATT_TK, LANE),
        compiler_params=_params("parallel", "parallel", "parallel"),
        name="diff_attention",
    )(q, k, vt, lam_p, gain_col)


def _mla_kernel(q_ref, k_ref, vt_ref, o_ref, *scratch, strip):
    _flash_loop(q_ref, k_ref, vt_ref, scratch, strip, False)
    acc_ref = scratch[1]
    for st in range(q_ref.shape[1] // strip):
        o_ref[0, pl.ds(st * strip, strip), :] = _softmax_out(acc_ref[st]).T.astype(BF16)


def _mla_attention(q, k, vt, lq):
    b, lk, _ = k.shape
    tq = min(MLA_TQ, lq)
    strip = min(ATT_STRIP, tq)
    qs = pl.BlockSpec((1, tq, MLA_QK_PAD), lambda bi, h, i: (bi, i, h))
    ks = pl.BlockSpec((1, lk, MLA_QK_PAD), lambda bi, h, i: (bi, 0, h))
    vs = pl.BlockSpec((1, 1) + vt.shape[2:], lambda bi, h, i: (bi, h, 0, 0, 0))
    nchain = tq // strip
    return pl.pallas_call(
        functools.partial(_mla_kernel, strip=strip),
        grid=(b, MLA_HEADS, lq // tq),
        in_specs=[qs, ks, vs],
        out_specs=pl.BlockSpec((1, tq, MLA_V), lambda bi, h, i: (bi, i, h)),
        out_shape=jax.ShapeDtypeStruct((b, lq, MLA_HEADS * MLA_V), BF16),
        scratch_shapes=_flash_scratch(nchain, strip, ATT_TK, MLA_QK_PAD),
        compiler_params=_params("parallel", "parallel", "parallel"),
        name="mla_attention",
    )(q, k, vt)


def _scan_rows(gates_t, index, chunks):
    b, _, l = gates_t.shape
    idx = jnp.asarray(index, jnp.int32)
    rows = gates_t[:, idx, :]
    rows = rows.reshape(b, idx.shape[0], idx.shape[1], chunks, l // chunks)
    return jnp.swapaxes(rows, 2, 3)


def _even_layer(x, ctx, mod_x, mod_c, p):
    b = x.shape[0]
    nh, ns = DN_HEADS, SSD_HEADS
    dn_index = [[d * nh + h for h in range(nh)] + [2 * nh + d * nh + h for h in range(nh)] for d in range(2)]
    dt0, a0 = 4 * nh, 4 * nh + 2 * ns
    ssd_index = [[dt0 + g * SSD_GRP + r for r in range(SSD_GRP)] + [a0 + g * SSD_GRP + r for r in range(SSD_GRP)]
                 + [dt0 + ns + g * SSD_GRP + r for r in range(SSD_GRP)] + [a0 + ns + g * SSD_GRP + r for r in range(SSD_GRP)]
                 for g in range(SSD_GROUPS)]

    def prepare(t, modv):
        q, k, v, xs, bc, z, gates = _even_in(t, modv, p["gain_mix"], p["w_main"], p["w_small"], p["conv_w"], p["conv_b"],
                                             p["gate_bias"], p["gate_scale"])
        gt = jnp.swapaxes(gates, 1, 2)
        nchunk = t.shape[1] // CHUNK
        return dict(q=q, k=k, v=v, xs=xs, bc=bc, z=z, dn_rows=jnp.swapaxes(_scan_rows(gt, dn_index, nchunk), 1, 2),
                    ssd_rows=jnp.swapaxes(_scan_rows(gt, ssd_index, nchunk), 1, 2))

    pc = prepare(ctx, mod_c)
    pL = prepare(x, mod_x)
    dn0 = jnp.zeros((b, 2 * nh, DN_DK, DN_DV), F32)
    ssd0 = jnp.zeros((b, SSD_GROUPS, 2, SSD_STATE, SSD_GRP * SSD_HEAD_DIM), F32)
    ocf, ocb, dn1 = _dn_scan(pc["q"], pc["k"], pc["v"], pc["dn_rows"], dn0)
    olf, olb, _ = _dn_scan(pL["q"], pL["k"], pL["v"], pL["dn_rows"], dn1)
    ycf, ycb, ssd1 = _ssd_scan(pc["xs"], pc["bc"], pc["ssd_rows"], ssd0, p["expand"])
    ylf, ylb, _ = _ssd_scan(pL["xs"], pL["bc"], pL["ssd_rows"], ssd1, p["expand"])

    def finish(t, modv, pp, of, ob, yf, yb):
        a, bm = _even_merge(of, ob, yf, yb, pp["xs"], pp["z"], p["dn_gain"], p["ssd_d"], p["ssd_gain"])
        return _mix_ffn(t, a, bm, modv, p["gain_ffn"], p["w_out"], p["ffn_w_in"], p["ffn_w_out"])

    return finish(x, mod_x, pL, olf, olb, ylf, ylb), finish(ctx, mod_c, pc, ocf, ocb, ycf, ycb)


def _rope_tables(n_tokens):
    lane = jnp.arange(LANE)
    quarter = (lane % 64) // 16
    n_freq = DA_DIM // 4
    inv_freq = ROPE_THETA ** (-(lane % 16).astype(F32) / n_freq)
    tok = jnp.arange(n_tokens)
    pos = jnp.where(quarter[None, :] < 2, (tok // GRID_W)[:, None], (tok % GRID_W)[:, None]).astype(F32)
    ang = pos * inv_freq[None, :]
    sign = jnp.where(quarter % 2 == 0, -1.0, 1.0).astype(F32)
    return jnp.cos(ang), jnp.sin(ang) * sign[None, :]


def _odd_layer(x, ctx, mod_x, mod_c, p, lambda_init):
    lq, lc = x.shape[1], ctx.shape[1]
    cos_t, sin_t = _rope_tables(lq)
    cos_t = jnp.concatenate([cos_t, jnp.ones((lc, LANE), F32)], axis=0)
    sin_t = jnp.concatenate([sin_t, jnp.zeros((lc, LANE), F32)], axis=0)
    mod2 = jnp.stack([mod_x, mod_c], axis=1)
    daq, dak, davt, mq, mk, mvt = _odd_in(x, ctx, mod2, p["gain_mix"], p["w_in"], p["q_a_gain"], p["w_uq"], p["kv_a_gain"],
                                          p["w_ukv"], p["dq_gain"], p["dk_gain"], p["mq_gain"], p["mk_gain"], cos_t, sin_t)
    da = _diff_attention(daq, dak, davt, lq, p["da_lambda"], p["sub_gain"], lambda_init)
    ml = _mla_attention(mq, mk, mvt, lq)
    return _mix_ffn(x, da, ml, mod_x, p["gain_ffn"], p["w_out"], p["ffn_w_in"], p["ffn_w_out"])


def _even_params(i, j, norm_mix, norm_ffn, ffn_w_in, ffn_w_out, ev_w_in, ev_conv_w, ev_conv_b, dn_a_log, dn_dt_bias,
                 dn_norm, ssd_a_log, ssd_dt_bias, ssd_d, ssd_norm, ev_w_out):
    d = norm_mix.shape[1]
    w = ev_w_in[j]
    small = w[:, EV_CONV + EV_GATE:]
    nh, ns = DN_HEADS, SSD_HEADS
    w_small = jnp.concatenate([small, small[:, 4 * nh:], jnp.zeros((d, LANE - 4 * nh - 4 * ns), F32)], axis=1)
    zeros = lambda n: jnp.zeros((n,), F32)
    gate_bias = jnp.concatenate([zeros(2 * nh), dn_dt_bias[j].reshape(-1), ssd_dt_bias[j].reshape(-1),
                                 ssd_dt_bias[j].reshape(-1), zeros(LANE - 4 * nh - 4 * ns)])
    gate_scale = jnp.concatenate([jnp.ones((2 * nh,), F32), -jnp.exp(dn_a_log[j].reshape(-1)), jnp.ones((2 * ns,), F32),
                                  -jnp.exp(ssd_a_log[j].reshape(-1)), zeros(LANE - 4 * nh - 4 * ns)])
    expand = jnp.repeat(jnp.eye(SSD_GRP, dtype=F32), SSD_HEAD_DIM, axis=1)
    return dict(
        gain_mix=norm_mix[i].reshape(1, d), gain_ffn=norm_ffn[i].reshape(1, d),
        w_main=w[:, :EV_CONV + EV_GATE].astype(BF16), w_small=w_small,
        conv_w=ev_conv_w[j], conv_b=ev_conv_b[j].reshape(1, -1),
        gate_bias=gate_bias.reshape(1, LANE), gate_scale=gate_scale.reshape(1, LANE), expand=expand,
        dn_gain=dn_norm[j].reshape(1, DN_DV), ssd_d=jnp.repeat(ssd_d[j], SSD_HEAD_DIM).reshape(1, SSD_INNER),
        ssd_gain=ssd_norm[j].reshape(1, SSD_INNER), w_out=ev_w_out[j].astype(BF16),
        ffn_w_in=ffn_w_in[i].astype(BF16), ffn_w_out=ffn_w_out[i].astype(BF16))


def _odd_params(i, j, norm_mix, norm_ffn, ffn_w_in, ffn_w_out, od_w_in, da_q_norm, da_k_norm, da_lambda, da_sub_norm,
                mla_q_a_norm, mla_w_uq, mla_kv_a_norm, mla_w_ukv, mla_q_norm, mla_k_norm, od_w_out):
    d = norm_mix.shape[1]
    w = od_w_in[j]
    w_in = jnp.concatenate([w, jnp.zeros((d, LANE - MLA_ROPE), F32)], axis=1).astype(BF16)
    hq = mla_w_uq[j].reshape(MLA_Q_RANK, MLA_HEADS, MLA_NOPE + MLA_ROPE)
    hq = jnp.pad(hq, ((0, 0), (0, 0), (0, MLA_QK_PAD - MLA_NOPE - MLA_ROPE)))
    hkv = mla_w_ukv[j].reshape(MLA_KV_RANK, MLA_HEADS, MLA_NOPE + MLA_V)
    w_ukv = jnp.concatenate([hkv[:, :, :MLA_NOPE].reshape(MLA_KV_RANK, -1), hkv[:, :, MLA_NOPE:].reshape(MLA_KV_RANK, -1)], axis=1)
    pad_gain = lambda g: jnp.pad(g, (0, MLA_QK_PAD - MLA_NOPE - MLA_ROPE)).reshape(1, MLA_QK_PAD)
    return dict(
        gain_mix=norm_mix[i].reshape(1, d), gain_ffn=norm_ffn[i].reshape(1, d), w_in=w_in,
        q_a_gain=mla_q_a_norm[j].reshape(1, -1), w_uq=hq.reshape(MLA_Q_RANK, -1).astype(BF16),
        kv_a_gain=mla_kv_a_norm[j].reshape(1, -1), w_ukv=w_ukv.astype(BF16),
        dq_gain=jnp.tile(da_q_norm[j], 2).reshape(1, LANE), dk_gain=jnp.tile(da_k_norm[j], 2).reshape(1, LANE),
        mq_gain=pad_gain(mla_q_norm[j]), mk_gain=pad_gain(mla_k_norm[j]),
        da_lambda=jnp.pad(da_lambda[j], ((0, 4), (0, LANE - DA_DIM))), sub_gain=da_sub_norm[j].reshape(2 * DA_DIM, 1),
        w_out=od_w_out[j].astype(BF16), ffn_w_in=ffn_w_in[i].astype(BF16), ffn_w_out=ffn_w_out[i].astype(BF16))


def kernel(x, c, ctx, c_ctx, ada_w, ada_b, norm_mix, norm_ffn, ffn_w_in, ffn_w_out, ev_w_in, ev_conv_w, ev_conv_b, dn_a_log, dn_dt_bias, dn_norm, ssd_a_log, ssd_dt_bias, ssd_d, ssd_norm, ev_w_out, od_w_in, da_q_norm, da_k_norm, da_lambda, da_sub_norm, mla_q_a_norm, mla_w_uq, mla_kv_a_norm, mla_w_ukv, mla_q_norm, mla_k_norm, od_w_out):
    b, _, d = x.shape
    depth = ada_w.shape[0]
    assert b < 8
    cvec = jnp.concatenate([c, c_ctx[None, :], jnp.zeros((8 - b - 1, d), F32)], axis=0)
    mod = _modulation(cvec, ada_w, ada_b)
    for i in range(depth):
        last = i == depth - 1
        j = i // 2
        mod_x = mod[i, :b].reshape(b, 1, 6 * d)
        mod_c = jnp.broadcast_to(mod[i, b].reshape(1, 1, 6 * d), (b, 1, 6 * d))
        if i % 2 == 0:
            p = _even_params(i, j, norm_mix, norm_ffn, ffn_w_in, ffn_w_out, ev_w_in, ev_conv_w, ev_conv_b, dn_a_log,
                             dn_dt_bias, dn_norm, ssd_a_log, ssd_dt_bias, ssd_d, ssd_norm, ev_w_out)
            x, ctx_new = _even_layer(x, ctx, mod_x, mod_c, p)
        else:
            p = _odd_params(i, j, norm_mix, norm_ffn, ffn_w_in, ffn_w_out, od_w_in, da_q_norm, da_k_norm, da_lambda,
                            da_sub_norm, mla_q_a_norm, mla_w_uq, mla_kv_a_norm, mla_w_ukv, mla_q_norm, mla_k_norm, od_w_out)
            lambda_init = 0.8 - 0.6 * math.exp(-0.3 * i)
            if last:
                x = _odd_layer(x, ctx, mod_x, mod_c, p, lambda_init)
                ctx_new = ctx
            else:
                raise NotImplementedError("context update after an attention layer is not needed for depth 2")
        ctx = ctx_new
    return x
```

```python
import functools
import math

import jax
import jax.numpy as jnp
from jax import lax
from jax.experimental import pallas as pl
from jax.experimental.pallas import tpu as pltpu

F32 = jnp.float32
BF16 = jnp.bfloat16

GRID_W = 64
EPS = 1e-6
ROPE_THETA = 10000.0
CONV_K = 5
DN_HEADS = 8
DN_DK = 128
DN_DV = 128
SSD_HEADS = 16
SSD_HEAD_DIM = 64
SSD_GROUPS = 2
SSD_STATE = 128
SSD_GRP = SSD_HEADS // SSD_GROUPS
SSD_INNER = SSD_HEADS * SSD_HEAD_DIM
DA_HEADS = 8
DA_DIM = 64
MLA_HEADS = 8
MLA_Q_RANK = 512
MLA_KV_RANK = 256
MLA_NOPE = 128
MLA_ROPE = 64
MLA_V = 128
MLA_QK_PAD = 256

DN_QK = DN_HEADS * DN_DK
DN_VW = DN_HEADS * DN_DV
SSD_BC = SSD_GROUPS * SSD_STATE
EV_CONV = 2 * DN_QK + DN_VW + SSD_INNER + 2 * SSD_BC
EV_GATE = DN_VW + SSD_INNER
DA_QK = DA_HEADS * 2 * DA_DIM
DA_VW = DA_HEADS * 2 * DA_DIM

LANE = 128
BF16_ROWS = 16
CHUNK = 128
VMEM_LIMIT = 56 * 1024 * 1024

ROW_TILE = 512
ODD_TILE = 256
SCAN_TILE = 512
DN_TILE = 256
DA_TQ = 1024
MLA_TQ = 1024
ATT_TK = 256
ATT_STRIP = 256
ACC_LAG = 2
V_ROWS = MLA_V + BF16_ROWS
LOG2E = 1.4426950408889634


def _dot(a, b):
    return jnp.dot(a, b, preferred_element_type=F32)


def _dot_nt(a, b):
    return lax.dot_general(a, b, (((1,), (1,)), ((), ())), preferred_element_type=F32)


def _dot_tn(a, b):
    return lax.dot_general(a, b, (((0,), (0,)), ((), ())), preferred_element_type=F32)


def _sigmoid(x):
    return 1.0 / (1.0 + jnp.exp(-x))


def _silu(x):
    return x * _sigmoid(x)


def _softplus(x):
    return jnp.maximum(x, 0.0) + jnp.log(1.0 + jnp.exp(-jnp.abs(x)))


def _params(*sem):
    return pltpu.CompilerParams(dimension_semantics=sem, vmem_limit_bytes=VMEM_LIMIT)


def _resident(shape):
    nd = len(shape)
    return pl.BlockSpec(shape, lambda *_: (0,) * nd, pipeline_mode=pl.Buffered(1))


def _norm_mod(x, gain, shift, scale):
    y = x * lax.rsqrt(jnp.mean(x * x, axis=-1, keepdims=True) + EPS)
    return (y * gain) * (1.0 + scale) + shift


def _mod_kernel(c_ref, w_ref, b_ref, o_ref):
    o_ref[0] = _dot(_silu(c_ref[...]), w_ref[0]) + b_ref[0]


def _modulation(cvec, ada_w, ada_b):
    depth, d, n = ada_w.shape
    tn = n // 4
    return pl.pallas_call(
        _mod_kernel,
        grid=(depth, n // tn),
        in_specs=[
            pl.BlockSpec((8, d), lambda l, j: (0, 0)),
            pl.BlockSpec((1, d, tn), lambda l, j: (l, 0, j)),
            pl.BlockSpec((1, 1, tn), lambda l, j: (l, 0, j)),
        ],
        out_specs=pl.BlockSpec((1, 8, tn), lambda l, j: (l, 0, j)),
        out_shape=jax.ShapeDtypeStruct((depth, 8, n), F32),
        compiler_params=_params("parallel", "parallel"),
        name="modulation",
    )(cvec, ada_w, ada_b.reshape(depth, 1, n))


def _evin_kernel(xp_ref, xc_ref, xn_ref, mod_ref, gain_ref, w_ref, wsm_ref, cw_ref, cb_ref, gb_ref, gs_ref,
                 q_ref, k_ref, v_ref, xs_ref, bc_ref, z_ref, g_ref, hext_ref, pext_ref):
    i = pl.program_id(1)
    nb = pl.num_programs(1)
    tm = xc_ref.shape[1]
    d = xc_ref.shape[2]
    halo = BF16_ROWS
    gain = gain_ref[...]
    shift = mod_ref[0, :, 0:d]
    scale = mod_ref[0, :, d:2 * d]

    hc = _norm_mod(xc_ref[0], gain, shift, scale)
    hp = _norm_mod(xp_ref[0], gain, shift, scale)
    hn = _norm_mod(xn_ref[0], gain, shift, scale)
    hext_ref[0:halo, :] = jnp.where(i > 0, hp, 0.0).astype(BF16)
    hext_ref[halo:halo + tm, :] = hc.astype(BF16)
    hext_ref[halo + tm:, :] = jnp.where(i < nb - 1, hn, 0.0).astype(BF16)

    gw = pext_ref.shape[1]
    pad = CONV_K // 2
    plan = ((q_ref, 0, DN_QK, DN_DK ** -0.5), (k_ref, DN_QK, DN_QK, 1.0), (v_ref, 2 * DN_QK, DN_VW, None),
            (xs_ref, 2 * DN_QK + DN_VW, SSD_INNER, None), (bc_ref, 2 * DN_QK + DN_VW + SSD_INNER, 2 * SSD_BC, None))
    for out_ref, base, width, l2_scale in plan:
        for off in range(0, width, gw):
            c0 = base + off
            pext_ref[...] = _dot(hext_ref[...], w_ref[:, c0:c0 + gw])
            acc = cb_ref[:, c0:c0 + gw] + cw_ref[0:1, c0:c0 + gw] * pext_ref[halo - pad:halo - pad + tm, :]
            for t in range(1, CONV_K):
                acc = acc + cw_ref[t:t + 1, c0:c0 + gw] * pext_ref[halo - pad + t:halo - pad + t + tm, :]
            u = _silu(acc)
            if l2_scale is None:
                out_ref[0, :, off:off + gw] = u
            else:
                for s in range(0, gw, LANE):
                    uh = u[:, s:s + LANE]
                    inv = lax.rsqrt(jnp.sum(uh * uh, axis=-1, keepdims=True) + EPS)
                    out_ref[0, :, off + s:off + s + LANE] = uh * (inv * l2_scale)

    hcb = hext_ref[halo:halo + tm, :]
    for off in range(0, EV_GATE, gw):
        z_ref[0, :, off:off + gw] = _dot(hcb, w_ref[:, EV_CONV + off:EV_CONV + off + gw]).astype(z_ref.dtype)

    p = _dot(hc, wsm_ref[...]) + gb_ref[...]
    lane = lax.broadcasted_iota(jnp.int32, p.shape, 1)
    g_ref[0] = jnp.where(lane < 2 * DN_HEADS, _sigmoid(p), _softplus(p)) * gs_ref[...]


def _even_in(x, modv, gain, w_main, w_small, conv_w, conv_b, gate_bias, gate_scale):
    b, l, d = x.shape
    tm = min(ROW_TILE, l)
    nb = l // tm
    hb = tm // BF16_ROWS
    gw = 512
    row = lambda c: pl.BlockSpec((1, tm, c), lambda bi, i: (bi, i, 0))
    outs = (DN_QK, DN_QK, DN_VW, SSD_INNER, 2 * SSD_BC, EV_GATE, LANE)
    return pl.pallas_call(
        _evin_kernel,
        grid=(b, nb),
        in_specs=[
            pl.BlockSpec((1, BF16_ROWS, d), lambda bi, i: (bi, jnp.maximum(i * hb - 1, 0), 0)),
            row(d),
            pl.BlockSpec((1, BF16_ROWS, d), lambda bi, i: (bi, jnp.minimum((i + 1) * hb, l // BF16_ROWS - 1), 0)),
            pl.BlockSpec((1, 1, modv.shape[2]), lambda bi, i: (bi, 0, 0)),
            _resident(gain.shape), _resident(w_main.shape), _resident(w_small.shape), _resident(conv_w.shape),
            _resident(conv_b.shape), _resident(gate_bias.shape), _resident(gate_scale.shape),
        ],
        out_specs=[row(c) for c in outs],
        out_shape=[jax.ShapeDtypeStruct((b, l, c), BF16 if c == EV_GATE else F32) for c in outs],
        scratch_shapes=[pltpu.VMEM((tm + 2 * BF16_ROWS, d), BF16), pltpu.VMEM((tm + 2 * BF16_ROWS, gw), F32)],
        compiler_params=_params("parallel", "parallel"),
        name="even_in",
    )(x, x, x, modv, gain, w_main, w_small, conv_w, conv_b, gate_bias, gate_scale)


def _tri_masks(c):
    r = lax.broadcasted_iota(jnp.int32, (c, c), 0)
    s = lax.broadcasted_iota(jnp.int32, (c, c), 1)
    return r, s


TRI_BASE = 16


def _block_masks(r, s):
    same = lambda n: (r // n) == (s // n)
    masks = [same(TRI_BASE).astype(F32)]
    n = TRI_BASE
    while n < CHUNK:
        n *= 2
        masks.append(jnp.where(same(n), 1.0, 0.0) - jnp.where(same(n // 2), 1.0, 0.0))
    return masks


def _unit_tri_inverses(a, eye, blocks):
    p = [-(a_n * blocks[0]) for a_n in a]
    x = [eye + p_n for p_n in p]
    for _ in range(int(math.log2(TRI_BASE)) - 1):
        p = [_dot(p_n, p_n) for p_n in p]
        x = [x_n + _dot(x_n, p_n) for x_n, p_n in zip(x, p)]
    for m in blocks[1:]:
        e = [_dot(a_n * m, x_n) for a_n, x_n in zip(a, x)]
        x = [x_n - _dot(x_n, e_n) for x_n, e_n in zip(x, e)]
    return x


def _dn_chunks(q, k, v, gc, beta, crow, total, state, incl, strict, eye, blocks):
    n = range(len(q))
    c = q[0].shape[0]
    decay = [jnp.where(incl[i], jnp.exp(gc[i] - crow[i]), 0.0) for i in n]
    kb = [k[i] * beta[i] for i in n]
    a = [jnp.where(strict[i], _dot_nt(kb[i], k[i]) * decay[i], 0.0) for i in n]
    t = _unit_tri_inverses(a, eye, blocks)
    egc = [jnp.exp(gc[i]) for i in n]
    uw = [_dot(t[i], jnp.concatenate([v[i] * beta[i], kb[i] * egc[i]], axis=1)) for i in n]
    ws_qs = [_dot(jnp.concatenate([uw[i][:, DN_DV:], q[i] * egc[i]], axis=0), state[i]) for i in n]
    v_new = [uw[i][:, 0:DN_DV] - ws_qs[i][0:c] for i in n]
    attn = [_dot_nt(q[i], k[i]) * decay[i] for i in n]
    o = [ws_qs[i][c:2 * c] + _dot(attn[i], v_new[i]) for i in n]
    state = [state[i] * jnp.exp(total[i]) + _dot_tn(k[i] * jnp.exp(total[i] - gc[i]), v_new[i]) for i in n]
    return o, state


def _dn_kernel(qf_ref, kf_ref, vf_ref, gf_ref, qb_ref, kb_ref, vb_ref, gb_ref, s0_ref,
               of_ref, ob_ref, sfin_ref, s_ref):
    i = pl.program_id(1)
    nck = gf_ref.shape[1]
    c = CHUNK
    nh = DN_HEADS

    @pl.when(i == 0)
    def _():
        s_ref[...] = s0_ref[0]

    r, s = _tri_masks(c)
    eye = (r == s).astype(F32)
    blocks = _block_masks(r, s)
    dirs = ((qf_ref, kf_ref, vf_ref, gf_ref, of_ref, r >= s, r > s, c - 1),
            (qb_ref, kb_ref, vb_ref, gb_ref, ob_ref, r <= s, r < s, 0))

    def body(ci, carry):
        q, k, v, gc, beta, crow, total, state, incl, strict, dest = ([] for _ in range(11))
        for d, (q_ref, k_ref, v_ref, g_ref, o_ref, inc, stri, tot_row) in enumerate(dirs):
            cc = ci if d == 0 else nck - 1 - ci
            rows = pl.ds(pl.multiple_of(cc * c, c), c)
            inclf = inc.astype(F32)
            grow = g_ref[0, cc, d]
            cum_rows = _dot_nt(grow, inclf)
            cols = _dot_nt(jnp.concatenate([inclf, eye], axis=0), grow)
            for h in range(nh):
                lanes = slice(h * LANE, (h + 1) * LANE)
                q.append(q_ref[0, rows, lanes])
                k.append(k_ref[0, rows, lanes])
                v.append(v_ref[0, rows, lanes])
                gc.append(cols[0:c, nh + h:nh + h + 1])
                beta.append(cols[c:2 * c, h:h + 1])
                crow.append(cum_rows[nh + h:nh + h + 1, :])
                total.append(cols[tot_row:tot_row + 1, nh + h:nh + h + 1])
                state.append(s_ref[d * nh + h])
                incl.append(inc)
                strict.append(stri)
                dest.append((o_ref, rows, lanes))
        o, state = _dn_chunks(q, k, v, gc, beta, crow, total, state, incl, strict, eye, blocks)
        for n, (o_ref, rows, lanes) in enumerate(dest):
            o_ref[0, rows, lanes] = o[n].astype(o_ref.dtype)
            s_ref[n] = state[n]
        return carry

    lax.fori_loop(0, nck, body, 0)

    @pl.when(i == pl.num_programs(1) - 1)
    def _():
        sfin_ref[0] = s_ref[...]


def _dn_scan(q, k, v, grows, state0):
    b, l, width = q.shape
    tl = min(DN_TILE, l)
    nb = l // tl
    nck = tl // CHUNK
    fwd = pl.BlockSpec((1, tl, width), lambda bi, i: (bi, i, 0))
    bwd = pl.BlockSpec((1, tl, width), lambda bi, i: (bi, nb - 1 - i, 0))
    gfwd = pl.BlockSpec((1, nck) + grows.shape[2:], lambda bi, i: (bi, i, 0, 0, 0))
    gbwd = pl.BlockSpec((1, nck) + grows.shape[2:], lambda bi, i: (bi, nb - 1 - i, 0, 0, 0))
    st = pl.BlockSpec((1,) + state0.shape[1:], lambda bi, i: (bi, 0, 0, 0))
    return pl.pallas_call(
        _dn_kernel,
        grid=(b, nb),
        in_specs=[fwd, fwd, fwd, gfwd, bwd, bwd, bwd, gbwd, st],
        out_specs=[fwd, bwd, st],
        out_shape=[jax.ShapeDtypeStruct(q.shape, BF16), jax.ShapeDtypeStruct(q.shape, BF16),
                   jax.ShapeDtypeStruct(state0.shape, F32)],
        scratch_shapes=[pltpu.VMEM(state0.shape[1:], F32)],
        compiler_params=_params("parallel", "arbitrary"),
        name="deltanet_scan",
    )(q, k, v, grows, q, k, v, grows, state0)


def _ssd_chunks(x, bm, cm, rows, state, expand, incl, eye, first_row):
    n = range(len(x))
    c = x[0].shape[0]
    nh = SSD_GRP
    p = SSD_HEAD_DIM
    inclf = [incl[i].astype(F32) for i in n]
    cum_row = [_dot_nt(rows[i][nh:2 * nh], inclf[i]) for i in n]
    cols = [_dot_nt(jnp.concatenate([inclf[i], eye], axis=0), rows[i]) for i in n]
    cum = [cols[i][0:c, nh:2 * nh] for i in n]
    total = [cum[i][first_row[i]:first_row[i] + 1, :] for i in n]
    xdt = [x[i] * _dot(cols[i][c:2 * c, 0:nh], expand) for i in n]
    cb = [_dot_nt(cm[i], bm[i]) for i in n]
    from_state = [_dot(cm[i], state[i]) * _dot(jnp.exp(cum[i]), expand) for i in n]
    parts = [[] for _ in n]
    for h in range(nh):
        seg = [jnp.where(incl[i], jnp.exp(cum[i][:, h:h + 1] - cum_row[i][h:h + 1, :]), 0.0) for i in n]
        for i in n:
            parts[i].append(_dot(cb[i] * seg[i], xdt[i][:, h * p:(h + 1) * p]))
    y = [jnp.concatenate(parts[i], axis=1) + from_state[i] for i in n]
    carry_in = [xdt[i] * _dot(jnp.exp(total[i] - cum[i]), expand) for i in n]
    state = [state[i] * _dot(jnp.exp(total[i]), expand) + _dot_tn(bm[i], carry_in[i]) for i in n]
    return y, state


def _ssd_kernel(xf_ref, bcf_ref, gf_ref, xb_ref, bcb_ref, gb_ref, s0_ref, e_ref, yf_ref, yb_ref, sfin_ref, s_ref):
    i = pl.program_id(1)
    nck = gf_ref.shape[1]
    c = CHUNK
    nh = SSD_GRP
    ng = SSD_GROUPS
    gwid = nh * SSD_HEAD_DIM

    @pl.when(i == 0)
    def _():
        s_ref[...] = s0_ref[0]

    r, s = _tri_masks(c)
    eye = (r == s).astype(F32)
    expand = e_ref[...]
    dirs = ((xf_ref, bcf_ref, gf_ref, yf_ref, r >= s, c - 1), (xb_ref, bcb_ref, gb_ref, yb_ref, r <= s, 0))

    def body(ci, carry):
        x, bm, cm, rows, state, incl, first, dest = ([] for _ in range(8))
        for d, (x_ref, bc_ref, g_ref, y_ref, inc, tot_row) in enumerate(dirs):
            cc = ci if d == 0 else nck - 1 - ci
            tok = pl.ds(pl.multiple_of(cc * c, c), c)
            for g in range(ng):
                lanes = slice(g * gwid, (g + 1) * gwid)
                x.append(x_ref[0, tok, lanes])
                bm.append(bc_ref[0, tok, g * SSD_STATE:(g + 1) * SSD_STATE])
                cm.append(bc_ref[0, tok, (ng + g) * SSD_STATE:(ng + g + 1) * SSD_STATE])
                rows.append(g_ref[0, cc, g][2 * nh * d:2 * nh * (d + 1)])
                state.append(s_ref[g, d])
                incl.append(inc)
                first.append(tot_row)
                dest.append((y_ref, tok, lanes, g, d))
        y, state = _ssd_chunks(x, bm, cm, rows, state, expand, incl, eye, first)
        for n, (y_ref, tok, lanes, g, d) in enumerate(dest):
            y_ref[0, tok, lanes] = y[n].astype(y_ref.dtype)
            s_ref[g, d] = state[n]
        return carry

    lax.fori_loop(0, nck, body, 0)

    @pl.when(i == pl.num_programs(1) - 1)
    def _():
        sfin_ref[0] = s_ref[...]


def _ssd_scan(xs, bc, grows, state0, expand):
    b, l, width = xs.shape
    tl = min(SCAN_TILE, l)
    nb = l // tl
    nck = tl // CHUNK
    fwd = lambda w: pl.BlockSpec((1, tl, w), lambda bi, i: (bi, i, 0))
    bwd = lambda w: pl.BlockSpec((1, tl, w), lambda bi, i: (bi, nb - 1 - i, 0))
    gf = pl.BlockSpec((1, nck) + grows.shape[2:], lambda bi, i: (bi, i, 0, 0, 0))
    gb = pl.BlockSpec((1, nck) + grows.shape[2:], lambda bi, i: (bi, nb - 1 - i, 0, 0, 0))
    st = pl.BlockSpec((1,) + state0.shape[1:], lambda bi, i: (bi, 0, 0, 0, 0))
    return pl.pallas_call(
        _ssd_kernel,
        grid=(b, nb),
        in_specs=[fwd(width), fwd(bc.shape[2]), gf, bwd(width), bwd(bc.shape[2]), gb, st, _resident(expand.shape)],
        out_specs=[fwd(width), bwd(width), st],
        out_shape=[jax.ShapeDtypeStruct(xs.shape, BF16), jax.ShapeDtypeStruct(xs.shape, BF16),
                   jax.ShapeDtypeStruct(state0.shape, F32)],
        scratch_shapes=[pltpu.VMEM(state0.shape[1:], F32)],
        compiler_params=_params("parallel", "arbitrary"),
        name="ssd_scan",
    )(xs, bc, grows, xs, bc, grows, state0, expand)


def _evmerge_kernel(of_ref, ob_ref, yf_ref, yb_ref, xs_ref, z_ref, dng_ref, sd_ref, sg_ref, a_ref, b_ref):
    f32 = lambda ref, lo, width: ref[0, :, lo:lo + width].astype(F32)
    for s in range(0, DN_VW, LANE):
        o = f32(of_ref, s, LANE) + f32(ob_ref, s, LANE)
        o = o * lax.rsqrt(jnp.mean(o * o, axis=-1, keepdims=True) + EPS) * dng_ref[...]
        a_ref[0, :, s:s + LANE] = (o * _silu(f32(z_ref, s, LANE))).astype(BF16)
    gwid = SSD_INNER // SSD_GROUPS
    for s in range(0, SSD_INNER, gwid):
        y = f32(yf_ref, s, gwid) + f32(yb_ref, s, gwid) + sd_ref[:, s:s + gwid] * xs_ref[0, :, s:s + gwid]
        y = y * _silu(f32(z_ref, DN_VW + s, gwid))
        y = y * lax.rsqrt(jnp.mean(y * y, axis=-1, keepdims=True) + EPS) * sg_ref[:, s:s + gwid]
        b_ref[0, :, s:s + gwid] = y.astype(BF16)


def _even_merge(o_f, o_b, y_f, y_b, xs, z, dn_gain, ssd_d, ssd_gain):
    b, l, _ = o_f.shape
    tm = min(ROW_TILE, l)
    row = lambda c: pl.BlockSpec((1, tm, c), lambda bi, i: (bi, i, 0))
    return pl.pallas_call(
        _evmerge_kernel,
        grid=(b, l // tm),
        in_specs=[row(DN_VW), row(DN_VW), row(SSD_INNER), row(SSD_INNER), row(SSD_INNER), row(EV_GATE),
                  _resident(dn_gain.shape), _resident(ssd_d.shape), _resident(ssd_gain.shape)],
        out_specs=[row(DN_VW), row(SSD_INNER)],
        out_shape=[jax.ShapeDtypeStruct((b, l, DN_VW), BF16), jax.ShapeDtypeStruct((b, l, SSD_INNER), BF16)],
        compiler_params=_params("parallel", "parallel"),
        name="even_merge",
    )(o_f, o_b, y_f, y_b, xs, z, dn_gain, ssd_d, ssd_gain)


def _mixffn_kernel(x_ref, a_ref, b_ref, mod_ref, gain_ref, wo_ref, wi_ref, wf_ref, o_ref, act_ref):
    d = x_ref.shape[2]
    na = a_ref.shape[2]
    hid = wf_ref.shape[0]
    g1 = mod_ref[0, :, 2 * d:3 * d]
    shift = mod_ref[0, :, 3 * d:4 * d]
    scale = mod_ref[0, :, 4 * d:5 * d]
    g2 = mod_ref[0, :, 5 * d:6 * d]
    mixed = _dot(a_ref[0], wo_ref[0:na, :]) + _dot(b_ref[0], wo_ref[na:, :])
    x1 = x_ref[0] + g1 * mixed
    h = _norm_mod(x1, gain_ref[...], shift, scale).astype(BF16)
    half = hid // 2
    for s in range(0, hid, half):
        gate = _dot(h, wi_ref[:, s:s + half])
        up = _dot(h, wi_ref[:, hid + s:hid + s + half])
        act_ref[:, s:s + half] = (_silu(gate) * up).astype(BF16)
    o_ref[0] = x1 + g2 * _dot(act_ref[...], wf_ref[...])


def _mix_ffn(x, a, bmix, modv, gain, w_out, w_in, w_ffn_out):
    b, l, d = x.shape
    tm = min(ROW_TILE, l)
    hid = w_ffn_out.shape[0]
    row = lambda c: pl.BlockSpec((1, tm, c), lambda bi, i: (bi, i, 0))
    return pl.pallas_call(
        _mixffn_kernel,
        grid=(b, l // tm),
        in_specs=[row(d), row(a.shape[2]), row(bmix.shape[2]),
                  pl.BlockSpec((1, 1, modv.shape[2]), lambda bi, i: (bi, 0, 0)),
                  _resident(gain.shape), _resident(w_out.shape), _resident(w_in.shape), _resident(w_ffn_out.shape)],
        out_specs=row(d),
        out_shape=jax.ShapeDtypeStruct(x.shape, F32),
        scratch_shapes=[pltpu.VMEM((tm, hid), BF16)],
        compiler_params=_params("parallel", "parallel"),
        name="mix_ffn",
    )(x, a, bmix, modv, gain, w_out, w_in, w_ffn_out)


def _rope(x, cos, sin):
    lane = lax.broadcasted_iota(jnp.int32, x.shape, 1)
    partner = jnp.where(lane % 32 < 16, pltpu.roll(x, LANE - 16, 1), pltpu.roll(x, 16, 1))
    return x * cos + partner * sin


def _odin_kernel(x_ref, c_ref, mod_ref, gain_ref, w_ref, qan_ref, wuq_ref, kvan_ref, wukv_ref, dqn_ref, dkn_ref,
                 mqn_ref, mkn_ref, nm_ref, cos_ref, sin_ref, daq_ref, dak_ref, davt_ref, mq_ref, mk_ref, mvt_ref,
                 *, n_lat):
    d = x_ref.shape[2]
    xin = jnp.where(pl.program_id(1) < n_lat, x_ref[0], c_ref[0])
    h = _norm_mod(xin, gain_ref[...], mod_ref[0, 0, :, 0:d], mod_ref[0, 0, :, d:2 * d]).astype(BF16)
    cos = cos_ref[...]
    sin = sin_ref[...]
    ones = jnp.ones((V_ROWS - MLA_V, ATT_TK), BF16)

    def store_values_t(out_ref, vals):
        for hd in range(vals.shape[1] // LANE):
            for t in range(vals.shape[0] // ATT_TK):
                tile = vals[t * ATT_TK:(t + 1) * ATT_TK, hd * LANE:(hd + 1) * LANE]
                out_ref[0, hd, t, 0:MLA_V, :] = tile.T.astype(BF16)
                out_ref[0, hd, t, MLA_V:, :] = ones

    def slab_norm(u, g, kind):
        return u * lax.rsqrt(_dot(u * u, nm_ref[kind]) + EPS) * g

    pair_norm = functools.partial(slab_norm, kind=0)
    full_norm = functools.partial(slab_norm, kind=1)
    low_norm = functools.partial(slab_norm, kind=2)

    def da_part(out_ref, base, g, out_scale):
        p = _dot(h, w_ref[:, base:base + DA_QK])
        for s in range(0, DA_QK, LANE):
            u = _rope(pair_norm(p[:, s:s + LANE], g), cos, sin)
            out_ref[0, :, s:s + LANE] = (u * out_scale).astype(BF16)

    da_part(daq_ref, 0, dqn_ref[...], DA_DIM ** -0.5 * LOG2E)
    da_part(dak_ref, DA_QK, dkn_ref[...], 1.0)
    store_values_t(davt_ref, _dot(h, w_ref[:, 2 * DA_QK:2 * DA_QK + DA_VW]))

    def wide_norm(u, g):
        return u * lax.rsqrt(jnp.mean(u * u, axis=-1, keepdims=True) + EPS) * g

    c0 = 2 * DA_QK + DA_VW
    cq = wide_norm(_dot(h, w_ref[:, c0:c0 + MLA_Q_RANK]), qan_ref[...]).astype(BF16)
    qf = _dot(cq, wuq_ref[...])
    scale = (MLA_NOPE + MLA_ROPE) ** -0.5 * LOG2E
    for hd in range(MLA_HEADS):
        s = hd * MLA_QK_PAD
        mq_ref[0, :, s:s + LANE] = (full_norm(qf[:, s:s + LANE], mqn_ref[:, 0:LANE]) * scale).astype(BF16)
        u = _rope(low_norm(qf[:, s + LANE:s + 2 * LANE], mqn_ref[:, LANE:2 * LANE]), cos, sin)
        mq_ref[0, :, s + LANE:s + 2 * LANE] = (u * scale).astype(BF16)
    c0 += MLA_Q_RANK
    ckv = wide_norm(_dot(h, w_ref[:, c0:c0 + MLA_KV_RANK]), kvan_ref[...]).astype(BF16)
    kvu = _dot(ckv, wukv_ref[...])
    c0 += MLA_KV_RANK
    kr = _rope(low_norm(_dot(h, w_ref[:, c0:c0 + LANE]), mkn_ref[:, LANE:2 * LANE]), cos, sin).astype(BF16)
    for hd in range(MLA_HEADS):
        s = hd * MLA_QK_PAD
        mk_ref[0, :, s:s + LANE] = full_norm(kvu[:, hd * LANE:(hd + 1) * LANE], mkn_ref[:, 0:LANE]).astype(BF16)
        mk_ref[0, :, s + LANE:s + 2 * LANE] = kr
    store_values_t(mvt_ref, kvu[:, MLA_HEADS * MLA_NOPE:])


def _odd_in(x, ctx, mod2, gain, w_in, q_a_gain, w_uq, kv_a_gain, w_ukv, dq_gain, dk_gain, mq_gain, mk_gain,
            cos_t, sin_t):
    b, l, d = x.shape
    lc = ctx.shape[1]
    tm = min(ODD_TILE, l, lc)
    n_lat, n_ctx = l // tm, lc // tm
    lk = l + lc
    tiles = tm // ATT_TK
    row = lambda c: pl.BlockSpec((1, tm, c), lambda bi, i: (bi, i, 0))
    vt = pl.BlockSpec((1, MLA_HEADS, tiles, V_ROWS, ATT_TK), lambda bi, i: (bi, 0, i, 0, 0))
    widths = (DA_QK, DA_QK, None, MLA_HEADS * MLA_QK_PAD, MLA_HEADS * MLA_QK_PAD, None)
    table = pl.BlockSpec((tm, LANE), lambda bi, i: (i, 0))
    li = jnp.arange(LANE)
    half = (li[:, None] // DA_DIM == li[None, :] // DA_DIM).astype(F32) / DA_DIM
    low = jnp.broadcast_to((li[:, None] < MLA_ROPE).astype(F32) / MLA_ROPE, (LANE, LANE))
    norm_mats = jnp.stack([half, jnp.full((LANE, LANE), 1.0 / LANE, F32), low])
    res = [gain, w_in, q_a_gain, w_uq, kv_a_gain, w_ukv, dq_gain, dk_gain, mq_gain, mk_gain, norm_mats]
    vt_shape = jax.ShapeDtypeStruct((b, MLA_HEADS, lk // ATT_TK, V_ROWS, ATT_TK), BF16)
    return pl.pallas_call(
        functools.partial(_odin_kernel, n_lat=n_lat),
        grid=(b, n_lat + n_ctx),
        in_specs=[pl.BlockSpec((1, tm, d), lambda bi, i: (bi, jnp.minimum(i, n_lat - 1), 0)),
                  pl.BlockSpec((1, tm, d), lambda bi, i: (bi, jnp.maximum(i - n_lat, 0), 0)),
                  pl.BlockSpec((1, 1, 1, mod2.shape[3]), lambda bi, i: (bi, i // n_lat, 0, 0))]
        + [_resident(a.shape) for a in res] + [table, table],
        out_specs=[vt if c is None else row(c) for c in widths],
        out_shape=[vt_shape if c is None else jax.ShapeDtypeStruct((b, lk, c), BF16) for c in widths],
        compiler_params=_params("parallel", "parallel"),
        name="odd_in",
    )(x, ctx, mod2, *res, cos_t, sin_t)


def _flash_scratch(nchain, strip, tk, qk_width):
    half = [pltpu.VMEM((nchain, tk, strip), F32), pltpu.VMEM((nchain, 1, strip), F32)]
    return [pltpu.VMEM((nchain, 1, strip), F32), pltpu.VMEM((nchain, V_ROWS, strip), F32),
            pltpu.VMEM((nchain, qk_width, strip), BF16)] + half + half


def _flash_loop(q_ref, k_ref, vt_ref, scratch, strip, split_q):
    m_ref, acc_ref, qz_ref, s0_ref, mx0_ref, s1_ref, mx1_ref = scratch
    s_ref, mx_ref = (s0_ref, s1_ref), (mx0_ref, mx1_ref)
    tq = q_ref.shape[1]
    tk = vt_ref.shape[4]
    nk = vt_ref.shape[2]
    nchain = m_ref.shape[0]
    m_ref[...] = jnp.full(m_ref.shape, -jnp.inf, F32)
    acc_ref[...] = jnp.zeros(acc_ref.shape, F32)
    row = lax.broadcasted_iota(jnp.int32, (q_ref.shape[2], strip), 0)
    for st in range(tq // strip):
        qt = q_ref[0, pl.ds(st * strip, strip), :].astype(F32).T
        if split_q:
            qz_ref[2 * st] = jnp.where(row < DA_DIM, qt, 0.0).astype(BF16)
            qz_ref[2 * st + 1] = jnp.where(row < DA_DIM, 0.0, qt).astype(BF16)
        else:
            qz_ref[st] = qt.astype(BF16)

    def scores(j, c):
        st = _dot(k_ref[0, j * tk:(j + 1) * tk, :], qz_ref[c])
        s_ref[j % 2][c] = st
        mx_ref[j % 2][c] = jnp.max(st, axis=0, keepdims=True)

    def weigh(j, c):
        m_old = m_ref[c]
        m_new = jnp.maximum(m_old, mx_ref[j % 2][c])
        p = jnp.exp2((s_ref[j % 2][c] - m_new).astype(BF16))
        m_ref[c] = m_new
        return c, jnp.exp2(m_old - m_new), _dot(vt_ref[0, 0, j], p)

    def accumulate(pending):
        c, alpha, pv = pending
        acc_ref[c] = alpha * acc_ref[c] + pv

    for c in range(nchain):
        scores(0, c)
    pending = []
    for j in range(nk):
        for c in range(nchain):
            if j + 1 < nk:
                scores(j + 1, c)
            pending.append(weigh(j, c))
            if len(pending) > ACC_LAG:
                accumulate(pending.pop(0))
    for item in pending:
        accumulate(item)


def _softmax_out(acc):
    return acc[0:MLA_V] / acc[MLA_V:MLA_V + 1]


def _da_kernel(q_ref, k_ref, vt_ref, lam_ref, gain_ref, o_ref, *scratch, strip, lambda_init):
    _flash_loop(q_ref, k_ref, vt_ref, scratch, strip, True)
    acc_ref = scratch[1]
    lp = lam_ref[...]
    lam = (jnp.exp(jnp.sum(lp[0:1] * lp[1:2], axis=-1, keepdims=True))
           - jnp.exp(jnp.sum(lp[2:3] * lp[3:4], axis=-1, keepdims=True)) + lambda_init)
    for st in range(q_ref.shape[1] // strip):
        o = _softmax_out(acc_ref[2 * st]) - lam * _softmax_out(acc_ref[2 * st + 1])
        o = o * lax.rsqrt(jnp.mean(o * o, axis=0, keepdims=True) + EPS) * gain_ref[...]
        o_ref[0, pl.ds(st * strip, strip), :] = (o * (1.0 - lambda_init)).T.astype(BF16)


def _diff_attention(q, k, vt, lq, lam_p, gain_col, lambda_init):
    b, lk, _ = k.shape
    tq = min(DA_TQ, lq)
    strip = min(ATT_STRIP, tq)
    qs = pl.BlockSpec((1, tq, LANE), lambda bi, h, i: (bi, i, h))
    ks = pl.BlockSpec((1, lk, LANE), lambda bi, h, i: (bi, 0, h))
    vs = pl.BlockSpec((1, 1) + vt.shape[2:], lambda bi, h, i: (bi, h, 0, 0, 0))
    nchain = 2 * (tq // strip)
    return pl.pallas_call(
        functools.partial(_da_kernel, strip=strip, lambda_init=lambda_init),
        grid=(b, DA_HEADS, lq // tq),
        in_specs=[qs, ks, vs, _resident(lam_p.shape), _resident(gain_col.shape)],
        out_specs=qs,
        out_shape=jax.ShapeDtypeStruct((b, lq, q.shape[2]), BF16),
        scratch_shapes=_flash_scratch(nchain, strip, ATT_TK, LANE),
        compiler_params=_params("parallel", "parallel", "parallel"),
        name="diff_attention",
    )(q, k, vt, lam_p, gain_col)


def _mla_kernel(q_ref, k_ref, vt_ref, o_ref, *scratch, strip):
    _flash_loop(q_ref, k_ref, vt_ref, scratch, strip, False)
    acc_ref = scratch[1]
    for st in range(q_ref.shape[1] // strip):
        o_ref[0, pl.ds(st * strip, strip), :] = _softmax_out(acc_ref[st]).T.astype(BF16)


def _mla_attention(q, k, vt, lq):
    b, lk, _ = k.shape
    tq = min(MLA_TQ, lq)
    strip = min(ATT_STRIP, tq)
    qs = pl.BlockSpec((1, tq, MLA_QK_PAD), lambda bi, h, i: (bi, i, h))
    ks = pl.BlockSpec((1, lk, MLA_QK_PAD), lambda bi, h, i: (bi, 0, h))
    vs = pl.BlockSpec((1, 1) + vt.shape[2:], lambda bi, h, i: (bi, h, 0, 0, 0))
    nchain = tq // strip
    return pl.pallas_call(
        functools.partial(_mla_kernel, strip=strip),
        grid=(b, MLA_HEADS, lq // tq),
        in_specs=[qs, ks, vs],
        out_specs=pl.BlockSpec((1, tq, MLA_V), lambda bi, h, i: (bi, i, h)),
        out_shape=jax.ShapeDtypeStruct((b, lq, MLA_HEADS * MLA_V), BF16),
        scratch_shapes=_flash_scratch(nchain, strip, ATT_TK, MLA_QK_PAD),
        compiler_params=_params("parallel", "parallel", "parallel"),
        name="mla_attention",
    )(q, k, vt)


def _scan_rows(gates_t, index, chunks):
    b, _, l = gates_t.shape
    idx = jnp.asarray(index, jnp.int32)
    rows = gates_t[:, idx, :]
    rows = rows.reshape(b, idx.shape[0], idx.shape[1], chunks, l // chunks)
    return jnp.swapaxes(rows, 2, 3)


def _even_layer(x, ctx, mod_x, mod_c, p):
    b = x.shape[0]
    nh, ns = DN_HEADS, SSD_HEADS
    dn_index = [[d * nh + h for h in range(nh)] + [2 * nh + d * nh + h for h in range(nh)] for d in range(2)]
    dt0, a0 = 4 * nh, 4 * nh + 2 * ns
    ssd_index = [[dt0 + g * SSD_GRP + r for r in range(SSD_GRP)] + [a0 + g * SSD_GRP + r for r in range(SSD_GRP)]
                 + [dt0 + ns + g * SSD_GRP + r for r in range(SSD_GRP)] + [a0 + ns + g * SSD_GRP + r for r in range(SSD_GRP)]
                 for g in range(SSD_GROUPS)]

    def prepare(t, modv):
        q, k, v, xs, bc, z, gates = _even_in(t, modv, p["gain_mix"], p["w_main"], p["w_small"], p["conv_w"], p["conv_b"],
                                             p["gate_bias"], p["gate_scale"])
        gt = jnp.swapaxes(gates, 1, 2)
        nchunk = t.shape[1] // CHUNK
        return dict(q=q, k=k, v=v, xs=xs, bc=bc, z=z, dn_rows=jnp.swapaxes(_scan_rows(gt, dn_index, nchunk), 1, 2),
                    ssd_rows=jnp.swapaxes(_scan_rows(gt, ssd_index, nchunk), 1, 2))

    pc = prepare(ctx, mod_c)
    pL = prepare(x, mod_x)
    dn0 = jnp.zeros((b, 2 * nh, DN_DK, DN_DV), F32)
    ssd0 = jnp.zeros((b, SSD_GROUPS, 2, SSD_STATE, SSD_GRP * SSD_HEAD_DIM), F32)
    ocf, ocb, dn1 = _dn_scan(pc["q"], pc["k"], pc["v"], pc["dn_rows"], dn0)
    olf, olb, _ = _dn_scan(pL["q"], pL["k"], pL["v"], pL["dn_rows"], dn1)
    ycf, ycb, ssd1 = _ssd_scan(pc["xs"], pc["bc"], pc["ssd_rows"], ssd0, p["expand"])
    ylf, ylb, _ = _ssd_scan(pL["xs"], pL["bc"], pL["ssd_rows"], ssd1, p["expand"])

    def finish(t, modv, pp, of, ob, yf, yb):
        a, bm = _even_merge(of, ob, yf, yb, pp["xs"], pp["z"], p["dn_gain"], p["ssd_d"], p["ssd_gain"])
        return _mix_ffn(t, a, bm, modv, p["gain_ffn"], p["w_out"], p["ffn_w_in"], p["ffn_w_out"])

    return finish(x, mod_x, pL, olf, olb, ylf, ylb), finish(ctx, mod_c, pc, ocf, ocb, ycf, ycb)


def _rope_tables(n_tokens):
    lane = jnp.arange(LANE)
    quarter = (lane % 64) // 16
    n_freq = DA_DIM // 4
    inv_freq = ROPE_THETA ** (-(lane % 16).astype(F32) / n_freq)
    tok = jnp.arange(n_tokens)
    pos = jnp.where(quarter[None, :] < 2, (tok // GRID_W)[:, None], (tok % GRID_W)[:, None]).astype(F32)
    ang = pos * inv_freq[None, :]
    sign = jnp.where(quarter % 2 == 0, -1.0, 1.0).astype(F32)
    return jnp.cos(ang), jnp.sin(ang) * sign[None, :]


def _odd_layer(x, ctx, mod_x, mod_c, p, lambda_init):
    lq, lc = x.shape[1], ctx.shape[1]
    cos_t, sin_t = _rope_tables(lq)
    cos_t = jnp.concatenate([cos_t, jnp.ones((lc, LANE), F32)], axis=0)
    sin_t = jnp.concatenate([sin_t, jnp.zeros((lc, LANE), F32)], axis=0)
    mod2 = jnp.stack([mod_x, mod_c], axis=1)
    daq, dak, davt, mq, mk, mvt = _odd_in(x, ctx, mod2, p["gain_mix"], p["w_in"], p["q_a_gain"], p["w_uq"], p["kv_a_gain"],
                                          p["w_ukv"], p["dq_gain"], p["dk_gain"], p["mq_gain"], p["mk_gain"], cos_t, sin_t)
    da = _diff_attention(daq, dak, davt, lq, p["da_lambda"], p["sub_gain"], lambda_init)
    ml = _mla_attention(mq, mk, mvt, lq)
    return _mix_ffn(x, da, ml, mod_x, p["gain_ffn"], p["w_out"], p["ffn_w_in"], p["ffn_w_out"])


def _even_params(i, j, norm_mix, norm_ffn, ffn_w_in, ffn_w_out, ev_w_in, ev_conv_w, ev_conv_b, dn_a_log, dn_dt_bias,
                 dn_norm, ssd_a_log, ssd_dt_bias, ssd_d, ssd_norm, ev_w_out):
    d = norm_mix.shape[1]
    w = ev_w_in[j]
    small = w[:, EV_CONV + EV_GATE:]
    nh, ns = DN_HEADS, SSD_HEADS
    w_small = jnp.concatenate([small, small[:, 4 * nh:], jnp.zeros((d, LANE - 4 * nh - 4 * ns), F32)], axis=1)
    zeros = lambda n: jnp.zeros((n,), F32)
    gate_bias = jnp.concatenate([zeros(2 * nh), dn_dt_bias[j].reshape(-1), ssd_dt_bias[j].reshape(-1),
                                 ssd_dt_bias[j].reshape(-1), zeros(LANE - 4 * nh - 4 * ns)])
    gate_scale = jnp.concatenate([jnp.ones((2 * nh,), F32), -jnp.exp(dn_a_log[j].reshape(-1)), jnp.ones((2 * ns,), F32),
                                  -jnp.exp(ssd_a_log[j].reshape(-1)), zeros(LANE - 4 * nh - 4 * ns)])
    expand = jnp.repeat(jnp.eye(SSD_GRP, dtype=F32), SSD_HEAD_DIM, axis=1)
    return dict(
        gain_mix=norm_mix[i].reshape(1, d), gain_ffn=norm_ffn[i].reshape(1, d),
        w_main=w[:, :EV_CONV + EV_GATE].astype(BF16), w_small=w_small,
        conv_w=ev_conv_w[j], conv_b=ev_conv_b[j].reshape(1, -1),
        gate_bias=gate_bias.reshape(1, LANE), gate_scale=gate_scale.reshape(1, LANE), expand=expand,
        dn_gain=dn_norm[j].reshape(1, DN_DV), ssd_d=jnp.repeat(ssd_d[j], SSD_HEAD_DIM).reshape(1, SSD_INNER),
        ssd_gain=ssd_norm[j].reshape(1, SSD_INNER), w_out=ev_w_out[j].astype(BF16),
        ffn_w_in=ffn_w_in[i].astype(BF16), ffn_w_out=ffn_w_out[i].astype(BF16))


def _odd_params(i, j, norm_mix, norm_ffn, ffn_w_in, ffn_w_out, od_w_in, da_q_norm, da_k_norm, da_lambda, da_sub_norm,
                mla_q_a_norm, mla_w_uq, mla_kv_a_norm, mla_w_ukv, mla_q_norm, mla_k_norm, od_w_out):
    d = norm_mix.shape[1]
    w = od_w_in[j]
    w_in = jnp.concatenate([w, jnp.zeros((d, LANE - MLA_ROPE), F32)], axis=1).astype(BF16)
    hq = mla_w_uq[j].reshape(MLA_Q_RANK, MLA_HEADS, MLA_NOPE + MLA_ROPE)
    hq = jnp.pad(hq, ((0, 0), (0, 0), (0, MLA_QK_PAD - MLA_NOPE - MLA_ROPE)))
    hkv = mla_w_ukv[j].reshape(MLA_KV_RANK, MLA_HEADS, MLA_NOPE + MLA_V)
    w_ukv = jnp.concatenate([hkv[:, :, :MLA_NOPE].reshape(MLA_KV_RANK, -1), hkv[:, :, MLA_NOPE:].reshape(MLA_KV_RANK, -1)], axis=1)
    pad_gain = lambda g: jnp.pad(g, (0, MLA_QK_PAD - MLA_NOPE - MLA_ROPE)).reshape(1, MLA_QK_PAD)
    return dict(
        gain_mix=norm_mix[i].reshape(1, d), gain_ffn=norm_ffn[i].reshape(1, d), w_in=w_in,
        q_a_gain=mla_q_a_norm[j].reshape(1, -1), w_uq=hq.reshape(MLA_Q_RANK, -1).astype(BF16),
        kv_a_gain=mla_kv_a_norm[j].reshape(1, -1), w_ukv=w_ukv.astype(BF16),
        dq_gain=jnp.tile(da_q_norm[j], 2).reshape(1, LANE), dk_gain=jnp.tile(da_k_norm[j], 2).reshape(1, LANE),
        mq_gain=pad_gain(mla_q_norm[j]), mk_gain=pad_gain(mla_k_norm[j]),
        da_lambda=jnp.pad(da_lambda[j], ((0, 4), (0, LANE - DA_DIM))), sub_gain=da_sub_norm[j].reshape(2 * DA_DIM, 1),
        w_out=od_w_out[j].astype(BF16), ffn_w_in=ffn_w_in[i].astype(BF16), ffn_w_out=ffn_w_out[i].astype(BF16))


def kernel(x, c, ctx, c_ctx, ada_w, ada_b, norm_mix, norm_ffn, ffn_w_in, ffn_w_out, ev_w_in, ev_conv_w, ev_conv_b, dn_a_log, dn_dt_bias, dn_norm, ssd_a_log, ssd_dt_bias, ssd_d, ssd_norm, ev_w_out, od_w_in, da_q_norm, da_k_norm, da_lambda, da_sub_norm, mla_q_a_norm, mla_w_uq, mla_kv_a_norm, mla_w_ukv, mla_q_norm, mla_k_norm, od_w_out):
    b, _, d = x.shape
    depth = ada_w.shape[0]
    assert b < 8
    cvec = jnp.concatenate([c, c_ctx[None, :], jnp.zeros((8 - b - 1, d), F32)], axis=0)
    mod = _modulation(cvec, ada_w, ada_b)
    for i in range(depth):
        last = i == depth - 1
        j = i // 2
        mod_x = mod[i, :b].reshape(b, 1, 6 * d)
        mod_c = jnp.broadcast_to(mod[i, b].reshape(1, 1, 6 * d), (b, 1, 6 * d))
        if i % 2 == 0:
            p = _even_params(i, j, norm_mix, norm_ffn, ffn_w_in, ffn_w_out, ev_w_in, ev_conv_w, ev_conv_b, dn_a_log,
                             dn_dt_bias, dn_norm, ssd_a_log, ssd_dt_bias, ssd_d, ssd_norm, ev_w_out)
            x, ctx_new = _even_layer(x, ctx, mod_x, mod_c, p)
        else:
            p = _odd_params(i, j, norm_mix, norm_ffn, ffn_w_in, ffn_w_out, od_w_in, da_q_norm, da_k_norm, da_lambda,
                            da_sub_norm, mla_q_a_norm, mla_w_uq, mla_kv_a_norm, mla_w_ukv, mla_q_norm, mla_k_norm, od_w_out)
            lambda_init = 0.8 - 0.6 * math.exp(-0.3 * i)
            if last:
                x = _odd_layer(x, ctx, mod_x, mod_c, p, lambda_init)
                ctx_new = ctx
            else:
                raise NotImplementedError("context update after an attention layer is not needed for depth 2")
        ctx = ctx_new
    return x
```

```python
import functools
import math

import jax
import jax.numpy as jnp
from jax import lax
from jax.experimental import pallas as pl
from jax.experimental.pallas import tpu as pltpu

F32 = jnp.float32
BF16 = jnp.bfloat16

GRID_W = 64
EPS = 1e-6
ROPE_THETA = 10000.0
CONV_K = 5
DN_HEADS = 8
DN_DK = 128
DN_DV = 128
SSD_HEADS = 16
SSD_HEAD_DIM = 64
SSD_GROUPS = 2
SSD_STATE = 128
SSD_GRP = SSD_HEADS // SSD_GROUPS
SSD_INNER = SSD_HEADS * SSD_HEAD_DIM
DA_HEADS = 8
DA_DIM = 64
MLA_HEADS = 8
MLA_Q_RANK = 512
MLA_KV_RANK = 256
MLA_NOPE = 128
MLA_ROPE = 64
MLA_V = 128
MLA_QK_PAD = 256

DN_QK = DN_HEADS * DN_DK
DN_VW = DN_HEADS * DN_DV
SSD_BC = SSD_GROUPS * SSD_STATE
EV_CONV = 2 * DN_QK + DN_VW + SSD_INNER + 2 * SSD_BC
EV_GATE = DN_VW + SSD_INNER
DA_QK = DA_HEADS * 2 * DA_DIM
DA_VW = DA_HEADS * 2 * DA_DIM

LANE = 128
BF16_ROWS = 16
CHUNK = 128
VMEM_LIMIT = 56 * 1024 * 1024

ROW_TILE = 512
ODD_TILE = 256
SCAN_TILE = 512
DN_TILE = 256
DA_TQ = 1024
MLA_TQ = 1024
ATT_TK = 256
ATT_STRIP = 256
ACC_LAG = 3
V_ROWS = MLA_V + BF16_ROWS
LOG2E = 1.4426950408889634


def _dot(a, b):
    return jnp.dot(a, b, preferred_element_type=F32)


def _dot_nt(a, b):
    return lax.dot_general(a, b, (((1,), (1,)), ((), ())), preferred_element_type=F32)


def _dot_tn(a, b):
    return lax.dot_general(a, b, (((0,), (0,)), ((), ())), preferred_element_type=F32)


def _sigmoid(x):
    return 1.0 / (1.0 + jnp.exp(-x))


def _silu(x):
    return x * _sigmoid(x)


def _softplus(x):
    return jnp.maximum(x, 0.0) + jnp.log(1.0 + jnp.exp(-jnp.abs(x)))


def _params(*sem):
    return pltpu.CompilerParams(dimension_semantics=sem, vmem_limit_bytes=VMEM_LIMIT)


def _resident(shape):
    nd = len(shape)
    return pl.BlockSpec(shape, lambda *_: (0,) * nd, pipeline_mode=pl.Buffered(1))


def _norm_mod(x, gain, shift, scale):
    y = x * lax.rsqrt(jnp.mean(x * x, axis=-1, keepdims=True) + EPS)
    return (y * gain) * (1.0 + scale) + shift


def _mod_kernel(c_ref, w_ref, b_ref, o_ref):
    o_ref[0] = _dot(_silu(c_ref[...]), w_ref[0]) + b_ref[0]


def _modulation(cvec, ada_w, ada_b):
    depth, d, n = ada_w.shape
    tn = n // 4
    return pl.pallas_call(
        _mod_kernel,
        grid=(depth, n // tn),
        in_specs=[
            pl.BlockSpec((8, d), lambda l, j: (0, 0)),
            pl.BlockSpec((1, d, tn), lambda l, j: (l, 0, j)),
            pl.BlockSpec((1, 1, tn), lambda l, j: (l, 0, j)),
        ],
        out_specs=pl.BlockSpec((1, 8, tn), lambda l, j: (l, 0, j)),
        out_shape=jax.ShapeDtypeStruct((depth, 8, n), F32),
        compiler_params=_params("parallel", "parallel"),
        name="modulation",
    )(cvec, ada_w, ada_b.reshape(depth, 1, n))


def _evin_kernel(xp_ref, xc_ref, xn_ref, mod_ref, gain_ref, w_ref, wsm_ref, cw_ref, cb_ref, gb_ref, gs_ref,
                 q_ref, k_ref, v_ref, xs_ref, bc_ref, z_ref, g_ref, hext_ref, pext_ref):
    i = pl.program_id(1)
    nb = pl.num_programs(1)
    tm = xc_ref.shape[1]
    d = xc_ref.shape[2]
    halo = BF16_ROWS
    gain = gain_ref[...]
    shift = mod_ref[0, :, 0:d]
    scale = mod_ref[0, :, d:2 * d]

    hc = _norm_mod(xc_ref[0], gain, shift, scale)
    hp = _norm_mod(xp_ref[0], gain, shift, scale)
    hn = _norm_mod(xn_ref[0], gain, shift, scale)
    hext_ref[0:halo, :] = jnp.where(i > 0, hp, 0.0).astype(BF16)
    hext_ref[halo:halo + tm, :] = hc.astype(BF16)
    hext_ref[halo + tm:, :] = jnp.where(i < nb - 1, hn, 0.0).astype(BF16)

    gw = pext_ref.shape[1]
    pad = CONV_K // 2
    plan = ((q_ref, 0, DN_QK, DN_DK ** -0.5), (k_ref, DN_QK, DN_QK, 1.0), (v_ref, 2 * DN_QK, DN_VW, None),
            (xs_ref, 2 * DN_QK + DN_VW, SSD_INNER, None), (bc_ref, 2 * DN_QK + DN_VW + SSD_INNER, 2 * SSD_BC, None))
    for out_ref, base, width, l2_scale in plan:
        for off in range(0, width, gw):
            c0 = base + off
            pext_ref[...] = _dot(hext_ref[...], w_ref[:, c0:c0 + gw])
            acc = cb_ref[:, c0:c0 + gw] + cw_ref[0:1, c0:c0 + gw] * pext_ref[halo - pad:halo - pad + tm, :]
            for t in range(1, CONV_K):
                acc = acc + cw_ref[t:t + 1, c0:c0 + gw] * pext_ref[halo - pad + t:halo - pad + t + tm, :]
            u = _silu(acc)
            if l2_scale is None:
                out_ref[0, :, off:off + gw] = u
            else:
                for s in range(0, gw, LANE):
                    uh = u[:, s:s + LANE]
                    inv = lax.rsqrt(jnp.sum(uh * uh, axis=-1, keepdims=True) + EPS)
                    out_ref[0, :, off + s:off + s + LANE] = uh * (inv * l2_scale)

    hcb = hext_ref[halo:halo + tm, :]
    for off in range(0, EV_GATE, gw):
        z_ref[0, :, off:off + gw] = _dot(hcb, w_ref[:, EV_CONV + off:EV_CONV + off + gw]).astype(z_ref.dtype)

    p = _dot(hc, wsm_ref[...]) + gb_ref[...]
    lane = lax.broadcasted_iota(jnp.int32, p.shape, 1)
    g_ref[0] = jnp.where(lane < 2 * DN_HEADS, _sigmoid(p), _softplus(p)) * gs_ref[...]


def _even_in(x, modv, gain, w_main, w_small, conv_w, conv_b, gate_bias, gate_scale):
    b, l, d = x.shape
    tm = min(ROW_TILE, l)
    nb = l // tm
    hb = tm // BF16_ROWS
    gw = 512
    row = lambda c: pl.BlockSpec((1, tm, c), lambda bi, i: (bi, i, 0))
    outs = (DN_QK, DN_QK, DN_VW, SSD_INNER, 2 * SSD_BC, EV_GATE, LANE)
    return pl.pallas_call(
        _evin_kernel,
        grid=(b, nb),
        in_specs=[
            pl.BlockSpec((1, BF16_ROWS, d), lambda bi, i: (bi, jnp.maximum(i * hb - 1, 0), 0)),
            row(d),
            pl.BlockSpec((1, BF16_ROWS, d), lambda bi, i: (bi, jnp.minimum((i + 1) * hb, l // BF16_ROWS - 1), 0)),
            pl.BlockSpec((1, 1, modv.shape[2]), lambda bi, i: (bi, 0, 0)),
            _resident(gain.shape), _resident(w_main.shape), _resident(w_small.shape), _resident(conv_w.shape),
            _resident(conv_b.shape), _resident(gate_bias.shape), _resident(gate_scale.shape),
        ],
        out_specs=[row(c) for c in outs],
        out_shape=[jax.ShapeDtypeStruct((b, l, c), BF16 if c == EV_GATE else F32) for c in outs],
        scratch_shapes=[pltpu.VMEM((tm + 2 * BF16_ROWS, d), BF16), pltpu.VMEM((tm + 2 * BF16_ROWS, gw), F32)],
        compiler_params=_params("parallel", "parallel"),
        name="even_in",
    )(x, x, x, modv, gain, w_main, w_small, conv_w, conv_b, gate_bias, gate_scale)


def _tri_masks(c):
    r = lax.broadcasted_iota(jnp.int32, (c, c), 0)
    s = lax.broadcasted_iota(jnp.int32, (c, c), 1)
    return r, s


TRI_BASE = 16


def _block_masks(r, s):
    same = lambda n: (r // n) == (s // n)
    masks = [same(TRI_BASE).astype(F32)]
    n = TRI_BASE
    while n < CHUNK:
        n *= 2
        masks.append(jnp.where(same(n), 1.0, 0.0) - jnp.where(same(n // 2), 1.0, 0.0))
    return masks


def _unit_tri_inverses(a, eye, blocks):
    p = [-(a_n * blocks[0]) for a_n in a]
    x = [eye + p_n for p_n in p]
    for _ in range(int(math.log2(TRI_BASE)) - 1):
        p = [_dot(p_n, p_n) for p_n in p]
        x = [x_n + _dot(x_n, p_n) for x_n, p_n in zip(x, p)]
    for m in blocks[1:]:
        e = [_dot(a_n * m, x_n) for a_n, x_n in zip(a, x)]
        x = [x_n - _dot(x_n, e_n) for x_n, e_n in zip(x, e)]
    return x


def _dn_chunks(q, k, v, gc, beta, crow, total, state, incl, strict, eye, blocks):
    n = range(len(q))
    c = q[0].shape[0]
    decay = [jnp.where(incl[i], jnp.exp(gc[i] - crow[i]), 0.0) for i in n]
    kb = [k[i] * beta[i] for i in n]
    a = [jnp.where(strict[i], _dot_nt(kb[i], k[i]) * decay[i], 0.0) for i in n]
    t = _unit_tri_inverses(a, eye, blocks)
    egc = [jnp.exp(gc[i]) for i in n]
    uw = [_dot(t[i], jnp.concatenate([v[i] * beta[i], kb[i] * egc[i]], axis=1)) for i in n]
    ws_qs = [_dot(jnp.concatenate([uw[i][:, DN_DV:], q[i] * egc[i]], axis=0), state[i]) for i in n]
    v_new = [uw[i][:, 0:DN_DV] - ws_qs[i][0:c] for i in n]
    attn = [_dot_nt(q[i], k[i]) * decay[i] for i in n]
    o = [ws_qs[i][c:2 * c] + _dot(attn[i], v_new[i]) for i in n]
    state = [state[i] * jnp.exp(total[i]) + _dot_tn(k[i] * jnp.exp(total[i] - gc[i]), v_new[i]) for i in n]
    return o, state


def _dn_kernel(qf_ref, kf_ref, vf_ref, gf_ref, qb_ref, kb_ref, vb_ref, gb_ref, s0_ref,
               of_ref, ob_ref, sfin_ref, s_ref):
    i = pl.program_id(1)
    nck = gf_ref.shape[1]
    c = CHUNK
    nh = DN_HEADS

    @pl.when(i == 0)
    def _():
        s_ref[...] = s0_ref[0]

    r, s = _tri_masks(c)
    eye = (r == s).astype(F32)
    blocks = _block_masks(r, s)
    dirs = ((qf_ref, kf_ref, vf_ref, gf_ref, of_ref, r >= s, r > s, c - 1),
            (qb_ref, kb_ref, vb_ref, gb_ref, ob_ref, r <= s, r < s, 0))

    def body(ci, carry):
        q, k, v, gc, beta, crow, total, state, incl, strict, dest = ([] for _ in range(11))
        for d, (q_ref, k_ref, v_ref, g_ref, o_ref, inc, stri, tot_row) in enumerate(dirs):
            cc = ci if d == 0 else nck - 1 - ci
            rows = pl.ds(pl.multiple_of(cc * c, c), c)
            inclf = inc.astype(F32)
            grow = g_ref[0, cc, d]
            cum_rows = _dot_nt(grow, inclf)
            cols = _dot_nt(jnp.concatenate([inclf, eye], axis=0), grow)
            for h in range(nh):
                lanes = slice(h * LANE, (h + 1) * LANE)
                q.append(q_ref[0, rows, lanes])
                k.append(k_ref[0, rows, lanes])
                v.append(v_ref[0, rows, lanes])
                gc.append(cols[0:c, nh + h:nh + h + 1])
                beta.append(cols[c:2 * c, h:h + 1])
                crow.append(cum_rows[nh + h:nh + h + 1, :])
                total.append(cols[tot_row:tot_row + 1, nh + h:nh + h + 1])
                state.append(s_ref[d * nh + h])
                incl.append(inc)
                strict.append(stri)
                dest.append((o_ref, rows, lanes))
        o, state = _dn_chunks(q, k, v, gc, beta, crow, total, state, incl, strict, eye, blocks)
        for n, (o_ref, rows, lanes) in enumerate(dest):
            o_ref[0, rows, lanes] = o[n].astype(o_ref.dtype)
            s_ref[n] = state[n]
        return carry

    lax.fori_loop(0, nck, body, 0)

    @pl.when(i == pl.num_programs(1) - 1)
    def _():
        sfin_ref[0] = s_ref[...]


def _dn_scan(q, k, v, grows, state0):
    b, l, width = q.shape
    tl = min(DN_TILE, l)
    nb = l // tl
    nck = tl // CHUNK
    fwd = pl.BlockSpec((1, tl, width), lambda bi, i: (bi, i, 0))
    bwd = pl.BlockSpec((1, tl, width), lambda bi, i: (bi, nb - 1 - i, 0))
    gfwd = pl.BlockSpec((1, nck) + grows.shape[2:], lambda bi, i: (bi, i, 0, 0, 0))
    gbwd = pl.BlockSpec((1, nck) + grows.shape[2:], lambda bi, i: (bi, nb - 1 - i, 0, 0, 0))
    st = pl.BlockSpec((1,) + state0.shape[1:], lambda bi, i: (bi, 0, 0, 0))
    return pl.pallas_call(
        _dn_kernel,
        grid=(b, nb),
        in_specs=[fwd, fwd, fwd, gfwd, bwd, bwd, bwd, gbwd, st],
        out_specs=[fwd, bwd, st],
        out_shape=[jax.ShapeDtypeStruct(q.shape, BF16), jax.ShapeDtypeStruct(q.shape, BF16),
                   jax.ShapeDtypeStruct(state0.shape, F32)],
        scratch_shapes=[pltpu.VMEM(state0.shape[1:], F32)],
        compiler_params=_params("parallel", "arbitrary"),
        name="deltanet_scan",
    )(q, k, v, grows, q, k, v, grows, state0)


def _ssd_chunks(x, bm, cm, rows, state, expand, incl, eye, first_row):
    n = range(len(x))
    c = x[0].shape[0]
    nh = SSD_GRP
    p = SSD_HEAD_DIM
    inclf = [incl[i].astype(F32) for i in n]
    cum_row = [_dot_nt(rows[i][nh:2 * nh], inclf[i]) for i in n]
    cols = [_dot_nt(jnp.concatenate([inclf[i], eye], axis=0), rows[i]) for i in n]
    cum = [cols[i][0:c, nh:2 * nh] for i in n]
    total = [cum[i][first_row[i]:first_row[i] + 1, :] for i in n]
    xdt = [x[i] * _dot(cols[i][c:2 * c, 0:nh], expand) for i in n]
    cb = [_dot_nt(cm[i], bm[i]) for i in n]
    from_state = [_dot(cm[i], state[i]) * _dot(jnp.exp(cum[i]), expand) for i in n]
    parts = [[] for _ in n]
    for h in range(nh):
        seg = [jnp.where(incl[i], jnp.exp(cum[i][:, h:h + 1] - cum_row[i][h:h + 1, :]), 0.0) for i in n]
        for i in n:
            parts[i].append(_dot(cb[i] * seg[i], xdt[i][:, h * p:(h + 1) * p]))
    y = [jnp.concatenate(parts[i], axis=1) + from_state[i] for i in n]
    carry_in = [xdt[i] * _dot(jnp.exp(total[i] - cum[i]), expand) for i in n]
    state = [state[i] * _dot(jnp.exp(total[i]), expand) + _dot_tn(bm[i], carry_in[i]) for i in n]
    return y, state


def _ssd_kernel(xf_ref, bcf_ref, gf_ref, xb_ref, bcb_ref, gb_ref, s0_ref, e_ref, yf_ref, yb_ref, sfin_ref, s_ref):
    i = pl.program_id(1)
    nck = gf_ref.shape[1]
    c = CHUNK
    nh = SSD_GRP
    ng = SSD_GROUPS
    gwid = nh * SSD_HEAD_DIM

    @pl.when(i == 0)
    def _():
        s_ref[...] = s0_ref[0]

    r, s = _tri_masks(c)
    eye = (r == s).astype(F32)
    expand = e_ref[...]
    dirs = ((xf_ref, bcf_ref, gf_ref, yf_ref, r >= s, c - 1), (xb_ref, bcb_ref, gb_ref, yb_ref, r <= s, 0))

    def body(ci, carry):
        x, bm, cm, rows, state, incl, first, dest = ([] for _ in range(8))
        for d, (x_ref, bc_ref, g_ref, y_ref, inc, tot_row) in enumerate(dirs):
            cc = ci if d == 0 else nck - 1 - ci
            tok = pl.ds(pl.multiple_of(cc * c, c), c)
            for g in range(ng):
                lanes = slice(g * gwid, (g + 1) * gwid)
                x.append(x_ref[0, tok, lanes])
                bm.append(bc_ref[0, tok, g * SSD_STATE:(g + 1) * SSD_STATE])
                cm.append(bc_ref[0, tok, (ng + g) * SSD_STATE:(ng + g + 1) * SSD_STATE])
                rows.append(g_ref[0, cc, g][2 * nh * d:2 * nh * (d + 1)])
                state.append(s_ref[g, d])
                incl.append(inc)
                first.append(tot_row)
                dest.append((y_ref, tok, lanes, g, d))
        y, state = _ssd_chunks(x, bm, cm, rows, state, expand, incl, eye, first)
        for n, (y_ref, tok, lanes, g, d) in enumerate(dest):
            y_ref[0, tok, lanes] = y[n].astype(y_ref.dtype)
            s_ref[g, d] = state[n]
        return carry

    lax.fori_loop(0, nck, body, 0)

    @pl.when(i == pl.num_programs(1) - 1)
    def _():
        sfin_ref[0] = s_ref[...]


def _ssd_scan(xs, bc, grows, state0, expand):
    b, l, width = xs.shape
    tl = min(SCAN_TILE, l)
    nb = l // tl
    nck = tl // CHUNK
    fwd = lambda w: pl.BlockSpec((1, tl, w), lambda bi, i: (bi, i, 0))
    bwd = lambda w: pl.BlockSpec((1, tl, w), lambda bi, i: (bi, nb - 1 - i, 0))
    gf = pl.BlockSpec((1, nck) + grows.shape[2:], lambda bi, i: (bi, i, 0, 0, 0))
    gb = pl.BlockSpec((1, nck) + grows.shape[2:], lambda bi, i: (bi, nb - 1 - i, 0, 0, 0))
    st = pl.BlockSpec((1,) + state0.shape[1:], lambda bi, i: (bi, 0, 0, 0, 0))
    return pl.pallas_call(
        _ssd_kernel,
        grid=(b, nb),
        in_specs=[fwd(width), fwd(bc.shape[2]), gf, bwd(width), bwd(bc.shape[2]), gb, st, _resident(expand.shape)],
        out_specs=[fwd(width), bwd(width), st],
        out_shape=[jax.ShapeDtypeStruct(xs.shape, BF16), jax.ShapeDtypeStruct(xs.shape, BF16),
                   jax.ShapeDtypeStruct(state0.shape, F32)],
        scratch_shapes=[pltpu.VMEM(state0.shape[1:], F32)],
        compiler_params=_params("parallel", "arbitrary"),
        name="ssd_scan",
    )(xs, bc, grows, xs, bc, grows, state0, expand)


def _evmerge_kernel(of_ref, ob_ref, yf_ref, yb_ref, xs_ref, z_ref, dng_ref, sd_ref, sg_ref, a_ref, b_ref):
    f32 = lambda ref, lo, width: ref[0, :, lo:lo + width].astype(F32)
    for s in range(0, DN_VW, LANE):
        o = f32(of_ref, s, LANE) + f32(ob_ref, s, LANE)
        o = o * lax.rsqrt(jnp.mean(o * o, axis=-1, keepdims=True) + EPS) * dng_ref[...]
        a_ref[0, :, s:s + LANE] = (o * _silu(f32(z_ref, s, LANE))).astype(BF16)
    gwid = SSD_INNER // SSD_GROUPS
    for s in range(0, SSD_INNER, gwid):
        y = f32(yf_ref, s, gwid) + f32(yb_ref, s, gwid) + sd_ref[:, s:s + gwid] * xs_ref[0, :, s:s + gwid]
        y = y * _silu(f32(z_ref, DN_VW + s, gwid))
        y = y * lax.rsqrt(jnp.mean(y * y, axis=-1, keepdims=True) + EPS) * sg_ref[:, s:s + gwid]
        b_ref[0, :, s:s + gwid] = y.astype(BF16)


def _even_merge(o_f, o_b, y_f, y_b, xs, z, dn_gain, ssd_d, ssd_gain):
    b, l, _ = o_f.shape
    tm = min(ROW_TILE, l)
    row = lambda c: pl.BlockSpec((1, tm, c), lambda bi, i: (bi, i, 0))
    return pl.pallas_call(
        _evmerge_kernel,
        grid=(b, l // tm),
        in_specs=[row(DN_VW), row(DN_VW), row(SSD_INNER), row(SSD_INNER), row(SSD_INNER), row(EV_GATE),
                  _resident(dn_gain.shape), _resident(ssd_d.shape), _resident(ssd_gain.shape)],
        out_specs=[row(DN_VW), row(SSD_INNER)],
        out_shape=[jax.ShapeDtypeStruct((b, l, DN_VW), BF16), jax.ShapeDtypeStruct((b, l, SSD_INNER), BF16)],
        compiler_params=_params("parallel", "parallel"),
        name="even_merge",
    )(o_f, o_b, y_f, y_b, xs, z, dn_gain, ssd_d, ssd_gain)


def _mixffn_kernel(x_ref, a_ref, b_ref, mod_ref, gain_ref, wo_ref, wi_ref, wf_ref, o_ref, act_ref):
    d = x_ref.shape[2]
    na = a_ref.shape[2]
    hid = wf_ref.shape[0]
    g1 = mod_ref[0, :, 2 * d:3 * d]
    shift = mod_ref[0, :, 3 * d:4 * d]
    scale = mod_ref[0, :, 4 * d:5 * d]
    g2 = mod_ref[0, :, 5 * d:6 * d]
    mixed = _dot(a_ref[0], wo_ref[0:na, :]) + _dot(b_ref[0], wo_ref[na:, :])
    x1 = x_ref[0] + g1 * mixed
    h = _norm_mod(x1, gain_ref[...], shift, scale).astype(BF16)
    half = hid // 2
    for s in range(0, hid, half):
        gate = _dot(h, wi_ref[:, s:s + half])
        up = _dot(h, wi_ref[:, hid + s:hid + s + half])
        act_ref[:, s:s + half] = (_silu(gate) * up).astype(BF16)
    o_ref[0] = x1 + g2 * _dot(act_ref[...], wf_ref[...])


def _mix_ffn(x, a, bmix, modv, gain, w_out, w_in, w_ffn_out):
    b, l, d = x.shape
    tm = min(ROW_TILE, l)
    hid = w_ffn_out.shape[0]
    row = lambda c: pl.BlockSpec((1, tm, c), lambda bi, i: (bi, i, 0))
    return pl.pallas_call(
        _mixffn_kernel,
        grid=(b, l // tm),
        in_specs=[row(d), row(a.shape[2]), row(bmix.shape[2]),
                  pl.BlockSpec((1, 1, modv.shape[2]), lambda bi, i: (bi, 0, 0)),
                  _resident(gain.shape), _resident(w_out.shape), _resident(w_in.shape), _resident(w_ffn_out.shape)],
        out_specs=row(d),
        out_shape=jax.ShapeDtypeStruct(x.shape, F32),
        scratch_shapes=[pltpu.VMEM((tm, hid), BF16)],
        compiler_params=_params("parallel", "parallel"),
        name="mix_ffn",
    )(x, a, bmix, modv, gain, w_out, w_in, w_ffn_out)


def _rope(x, cos, sin):
    lane = lax.broadcasted_iota(jnp.int32, x.shape, 1)
    partner = jnp.where(lane % 32 < 16, pltpu.roll(x, LANE - 16, 1), pltpu.roll(x, 16, 1))
    return x * cos + partner * sin


def _odin_kernel(x_ref, c_ref, mod_ref, gain_ref, w_ref, qan_ref, wuq_ref, kvan_ref, wukv_ref, dqn_ref, dkn_ref,
                 mqn_ref, mkn_ref, nm_ref, cos_ref, sin_ref, daq_ref, dak_ref, davt_ref, mq_ref, mk_ref, mvt_ref,
                 *, n_lat):
    d = x_ref.shape[2]
    xin = jnp.where(pl.program_id(1) < n_lat, x_ref[0], c_ref[0])
    h = _norm_mod(xin, gain_ref[...], mod_ref[0, 0, :, 0:d], mod_ref[0, 0, :, d:2 * d]).astype(BF16)
    cos = cos_ref[...]
    sin = sin_ref[...]
    ones = jnp.ones((V_ROWS - MLA_V, ATT_TK), BF16)

    def store_values_t(out_ref, vals):
        for hd in range(vals.shape[1] // LANE):
            for t in range(vals.shape[0] // ATT_TK):
                tile = vals[t * ATT_TK:(t + 1) * ATT_TK, hd * LANE:(hd + 1) * LANE]
                out_ref[0, hd, t, 0:MLA_V, :] = tile.T.astype(BF16)
                out_ref[0, hd, t, MLA_V:, :] = ones

    def slab_norm(u, g, kind):
        return u * lax.rsqrt(_dot(u * u, nm_ref[kind]) + EPS) * g

    pair_norm = functools.partial(slab_norm, kind=0)
    full_norm = functools.partial(slab_norm, kind=1)
    low_norm = functools.partial(slab_norm, kind=2)

    def da_part(out_ref, base, g, out_scale):
        p = _dot(h, w_ref[:, base:base + DA_QK])
        for s in range(0, DA_QK, LANE):
            u = _rope(pair_norm(p[:, s:s + LANE], g), cos, sin)
            out_ref[0, :, s:s + LANE] = (u * out_scale).astype(BF16)

    da_part(daq_ref, 0, dqn_ref[...], DA_DIM ** -0.5 * LOG2E)
    da_part(dak_ref, DA_QK, dkn_ref[...], 1.0)
    store_values_t(davt_ref, _dot(h, w_ref[:, 2 * DA_QK:2 * DA_QK + DA_VW]))

    def wide_norm(u, g):
        return u * lax.rsqrt(jnp.mean(u * u, axis=-1, keepdims=True) + EPS) * g

    c0 = 2 * DA_QK + DA_VW
    cq = wide_norm(_dot(h, w_ref[:, c0:c0 + MLA_Q_RANK]), qan_ref[...]).astype(BF16)
    qf = _dot(cq, wuq_ref[...])
    scale = (MLA_NOPE + MLA_ROPE) ** -0.5 * LOG2E
    for hd in range(MLA_HEADS):
        s = hd * MLA_QK_PAD
        mq_ref[0, :, s:s + LANE] = (full_norm(qf[:, s:s + LANE], mqn_ref[:, 0:LANE]) * scale).astype(BF16)
        u = _rope(low_norm(qf[:, s + LANE:s + 2 * LANE], mqn_ref[:, LANE:2 * LANE]), cos, sin)
        mq_ref[0, :, s + LANE:s + 2 * LANE] = (u * scale).astype(BF16)
    c0 += MLA_Q_RANK
    ckv = wide_norm(_dot(h, w_ref[:, c0:c0 + MLA_KV_RANK]), kvan_ref[...]).astype(BF16)
    kvu = _dot(ckv, wukv_ref[...])
    c0 += MLA_KV_RANK
    kr = _rope(low_norm(_dot(h, w_ref[:, c0:c0 + LANE]), mkn_ref[:, LANE:2 * LANE]), cos, sin).astype(BF16)
    for hd in range(MLA_HEADS):
        s = hd * MLA_QK_PAD
        mk_ref[0, :, s:s + LANE] = full_norm(kvu[:, hd * LANE:(hd + 1) * LANE], mkn_ref[:, 0:LANE]).astype(BF16)
        mk_ref[0, :, s + LANE:s + 2 * LANE] = kr
    store_values_t(mvt_ref, kvu[:, MLA_HEADS * MLA_NOPE:])


def _odd_in(x, ctx, mod2, gain, w_in, q_a_gain, w_uq, kv_a_gain, w_ukv, dq_gain, dk_gain, mq_gain, mk_gain,
            cos_t, sin_t):
    b, l, d = x.shape
    lc = ctx.shape[1]
    tm = min(ODD_TILE, l, lc)
    n_lat, n_ctx = l // tm, lc // tm
    lk = l + lc
    tiles = tm // ATT_TK
    row = lambda c: pl.BlockSpec((1, tm, c), lambda bi, i: (bi, i, 0))
    vt = pl.BlockSpec((1, MLA_HEADS, tiles, V_ROWS, ATT_TK), lambda bi, i: (bi, 0, i, 0, 0))
    widths = (DA_QK, DA_QK, None, MLA_HEADS * MLA_QK_PAD, MLA_HEADS * MLA_QK_PAD, None)
    table = pl.BlockSpec((tm, LANE), lambda bi, i: (i, 0))
    li = jnp.arange(LANE)
    half = (li[:, None] // DA_DIM == li[None, :] // DA_DIM).astype(F32) / DA_DIM
    low = jnp.broadcast_to((li[:, None] < MLA_ROPE).astype(F32) / MLA_ROPE, (LANE, LANE))
    norm_mats = jnp.stack([half, jnp.full((LANE, LANE), 1.0 / LANE, F32), low])
    res = [gain, w_in, q_a_gain, w_uq, kv_a_gain, w_ukv, dq_gain, dk_gain, mq_gain, mk_gain, norm_mats]
    vt_shape = jax.ShapeDtypeStruct((b, MLA_HEADS, lk // ATT_TK, V_ROWS, ATT_TK), BF16)
    return pl.pallas_call(
        functools.partial(_odin_kernel, n_lat=n_lat),
        grid=(b, n_lat + n_ctx),
        in_specs=[pl.BlockSpec((1, tm, d), lambda bi, i: (bi, jnp.minimum(i, n_lat - 1), 0)),
                  pl.BlockSpec((1, tm, d), lambda bi, i: (bi, jnp.maximum(i - n_lat, 0), 0)),
                  pl.BlockSpec((1, 1, 1, mod2.shape[3]), lambda bi, i: (bi, i // n_lat, 0, 0))]
        + [_resident(a.shape) for a in res] + [table, table],
        out_specs=[vt if c is None else row(c) for c in widths],
        out_shape=[vt_shape if c is None else jax.ShapeDtypeStruct((b, lk, c), BF16) for c in widths],
        compiler_params=_params("parallel", "parallel"),
        name="odd_in",
    )(x, ctx, mod2, *res, cos_t, sin_t)


def _flash_scratch(nchain, strip, tk, qk_width):
    half = [pltpu.VMEM((nchain, tk, strip), F32), pltpu.VMEM((nchain, 1, strip), F32)]
    return [pltpu.VMEM((nchain, 1, strip), F32), pltpu.VMEM((nchain, V_ROWS, strip), F32),
            pltpu.VMEM((nchain, qk_width, strip), BF16)] + half + half


def _flash_loop(q_ref, k_ref, vt_ref, scratch, strip, split_q):
    m_ref, acc_ref, qz_ref, s0_ref, mx0_ref, s1_ref, mx1_ref = scratch
    s_ref, mx_ref = (s0_ref, s1_ref), (mx0_ref, mx1_ref)
    tq = q_ref.shape[1]
    tk = vt_ref.shape[4]
    nk = vt_ref.shape[2]
    nchain = m_ref.shape[0]
    m_ref[...] = jnp.full(m_ref.shape, -jnp.inf, F32)
    acc_ref[...] = jnp.zeros(acc_ref.shape, F32)
    row = lax.broadcasted_iota(jnp.int32, (q_ref.shape[2], strip), 0)
    for st in range(tq // strip):
        qt = q_ref[0, pl.ds(st * strip, strip), :].astype(F32).T
        if split_q:
            qz_ref[2 * st] = jnp.where(row < DA_DIM, qt, 0.0).astype(BF16)
            qz_ref[2 * st + 1] = jnp.where(row < DA_DIM, 0.0, qt).astype(BF16)
        else:
            qz_ref[st] = qt.astype(BF16)

    def scores(j, c):
        st = _dot(k_ref[0, j * tk:(j + 1) * tk, :], qz_ref[c])
        s_ref[j % 2][c] = st
        mx_ref[j % 2][c] = jnp.max(st, axis=0, keepdims=True)

    def weigh(j, c):
        m_old = m_ref[c]
        m_new = jnp.maximum(m_old, mx_ref[j % 2][c])
        p = jnp.exp2((s_ref[j % 2][c] - m_new).astype(BF16))
        m_ref[c] = m_new
        return c, jnp.exp2(m_old - m_new), _dot(vt_ref[0, 0, j], p)

    def accumulate(pending):
        c, alpha, pv = pending
        acc_ref[c] = alpha * acc_ref[c] + pv

    for c in range(nchain):
        scores(0, c)
    pending = []
    for j in range(nk):
        for c in range(nchain):
            if j + 1 < nk:
                scores(j + 1, c)
            pending.append(weigh(j, c))
            if len(pending) > ACC_LAG:
                accumulate(pending.pop(0))
    for item in pending:
        accumulate(item)


def _softmax_out(acc):
    return acc[0:MLA_V] / acc[MLA_V:MLA_V + 1]


def _da_kernel(q_ref, k_ref, vt_ref, lam_ref, gain_ref, o_ref, *scratch, strip, lambda_init):
    _flash_loop(q_ref, k_ref, vt_ref, scratch, strip, True)
    acc_ref = scratch[1]
    lp = lam_ref[...]
    lam = (jnp.exp(jnp.sum(lp[0:1] * lp[1:2], axis=-1, keepdims=True))
           - jnp.exp(jnp.sum(lp[2:3] * lp[3:4], axis=-1, keepdims=True)) + lambda_init)
    for st in range(q_ref.shape[1] // strip):
        o = _softmax_out(acc_ref[2 * st]) - lam * _softmax_out(acc_ref[2 * st + 1])
        o = o * lax.rsqrt(jnp.mean(o * o, axis=0, keepdims=True) + EPS) * gain_ref[...]
        o_ref[0, pl.ds(st * strip, strip), :] = (o * (1.0 - lambda_init)).T.astype(BF16)


def _diff_attention(q, k, vt, lq, lam_p, gain_col, lambda_init):
    b, lk, _ = k.shape
    tq = min(DA_TQ, lq)
    strip = min(ATT_STRIP, tq)
    qs = pl.BlockSpec((1, tq, LANE), lambda bi, h, i: (bi, i, h))
    ks = pl.BlockSpec((1, lk, LANE), lambda bi, h, i: (bi, 0, h))
    vs = pl.BlockSpec((1, 1) + vt.shape[2:], lambda bi, h, i: (bi, h, 0, 0, 0))
    nchain = 2 * (tq // strip)
    return pl.pallas_call(
        functools.partial(_da_kernel, strip=strip, lambda_init=lambda_init),
        grid=(b, DA_HEADS, lq // tq),
        in_specs=[qs, ks, vs, _resident(lam_p.shape), _resident(gain_col.shape)],
        out_specs=qs,
        out_shape=jax.ShapeDtypeStruct((b, lq, q.shape[2]), BF16),
        scratch_shapes=_flash_scratch(nchain, strip, ATT_TK, LANE),
        compiler_params=_params("parallel", "parallel", "parallel"),
        name="diff_attention",
    )(q, k, vt, lam_p, gain_col)


def _mla_kernel(q_ref, k_ref, vt_ref, o_ref, *scratch, strip):
    _flash_loop(q_ref, k_ref, vt_ref, scratch, strip, False)
    acc_ref = scratch[1]
    for st in range(q_ref.shape[1] // strip):
        o_ref[0, pl.ds(st * strip, strip), :] = _softmax_out(acc_ref[st]).T.astype(BF16)


def _mla_attention(q, k, vt, lq):
    b, lk, _ = k.shape
    tq = min(MLA_TQ, lq)
    strip = min(ATT_STRIP, tq)
    qs = pl.BlockSpec((1, tq, MLA_QK_PAD), lambda bi, h, i: (bi, i, h))
    ks = pl.BlockSpec((1, lk, MLA_QK_PAD), lambda bi, h, i: (bi, 0, h))
    vs = pl.BlockSpec((1, 1) + vt.shape[2:], lambda bi, h, i: (bi, h, 0, 0, 0))
    nchain = tq // strip
    return pl.pallas_call(
        functools.partial(_mla_kernel, strip=strip),
        grid=(b, MLA_HEADS, lq // tq),
        in_specs=[qs, ks, vs],
        out_specs=pl.BlockSpec((1, tq, MLA_V), lambda bi, h, i: (bi, i, h)),
        out_shape=jax.ShapeDtypeStruct((b, lq, MLA_HEADS * MLA_V), BF16),
        scratch_shapes=_flash_scratch(nchain, strip, ATT_TK, MLA_QK_PAD),
        compiler_params=_params("parallel", "parallel", "parallel"),
        name="mla_attention",
    )(q, k, vt)


def _scan_rows(gates_t, index, chunks):
    b, _, l = gates_t.shape
    idx = jnp.asarray(index, jnp.int32)
    rows = gates_t[:, idx, :]
    rows = rows.reshape(b, idx.shape[0], idx.shape[1], chunks, l // chunks)
    return jnp.swapaxes(rows, 2, 3)


def _even_layer(x, ctx, mod_x, mod_c, p):
    b = x.shape[0]
    nh, ns = DN_HEADS, SSD_HEADS
    dn_index = [[d * nh + h for h in range(nh)] + [2 * nh + d * nh + h for h in range(nh)] for d in range(2)]
    dt0, a0 = 4 * nh, 4 * nh + 2 * ns
    ssd_index = [[dt0 + g * SSD_GRP + r for r in range(SSD_GRP)] + [a0 + g * SSD_GRP + r for r in range(SSD_GRP)]
                 + [dt0 + ns + g * SSD_GRP + r for r in range(SSD_GRP)] + [a0 + ns + g * SSD_GRP + r for r in range(SSD_GRP)]
                 for g in range(SSD_GROUPS)]

    def prepare(t, modv):
        q, k, v, xs, bc, z, gates = _even_in(t, modv, p["gain_mix"], p["w_main"], p["w_small"], p["conv_w"], p["conv_b"],
                                             p["gate_bias"], p["gate_scale"])
        gt = jnp.swapaxes(gates, 1, 2)
        nchunk = t.shape[1] // CHUNK
        return dict(q=q, k=k, v=v, xs=xs, bc=bc, z=z, dn_rows=jnp.swapaxes(_scan_rows(gt, dn_index, nchunk), 1, 2),
                    ssd_rows=jnp.swapaxes(_scan_rows(gt, ssd_index, nchunk), 1, 2))

    pc = prepare(ctx, mod_c)
    pL = prepare(x, mod_x)
    dn0 = jnp.zeros((b, 2 * nh, DN_DK, DN_DV), F32)
    ssd0 = jnp.zeros((b, SSD_GROUPS, 2, SSD_STATE, SSD_GRP * SSD_HEAD_DIM), F32)
    ocf, ocb, dn1 = _dn_scan(pc["q"], pc["k"], pc["v"], pc["dn_rows"], dn0)
    olf, olb, _ = _dn_scan(pL["q"], pL["k"], pL["v"], pL["dn_rows"], dn1)
    ycf, ycb, ssd1 = _ssd_scan(pc["xs"], pc["bc"], pc["ssd_rows"], ssd0, p["expand"])
    ylf, ylb, _ = _ssd_scan(pL["xs"], pL["bc"], pL["ssd_rows"], ssd1, p["expand"])

    def finish(t, modv, pp, of, ob, yf, yb):
        a, bm = _even_merge(of, ob, yf, yb, pp["xs"], pp["z"], p["dn_gain"], p["ssd_d"], p["ssd_gain"])
        return _mix_ffn(t, a, bm, modv, p["gain_ffn"], p["w_out"], p["ffn_w_in"], p["ffn_w_out"])

    return finish(x, mod_x, pL, olf, olb, ylf, ylb), finish(ctx, mod_c, pc, ocf, ocb, ycf, ycb)


def _rope_tables(n_tokens):
    lane = jnp.arange(LANE)
    quarter = (lane % 64) // 16
    n_freq = DA_DIM // 4
    inv_freq = ROPE_THETA ** (-(lane % 16).astype(F32) / n_freq)
    tok = jnp.arange(n_tokens)
    pos = jnp.where(quarter[None, :] < 2, (tok // GRID_W)[:, None], (tok % GRID_W)[:, None]).astype(F32)
    ang = pos * inv_freq[None, :]
    sign = jnp.where(quarter % 2 == 0, -1.0, 1.0).astype(F32)
    return jnp.cos(ang), jnp.sin(ang) * sign[None, :]


def _odd_layer(x, ctx, mod_x, mod_c, p, lambda_init):
    lq, lc = x.shape[1], ctx.shape[1]
    cos_t, sin_t = _rope_tables(lq)
    cos_t = jnp.concatenate([cos_t, jnp.ones((lc, LANE), F32)], axis=0)
    sin_t = jnp.concatenate([sin_t, jnp.zeros((lc, LANE), F32)], axis=0)
    mod2 = jnp.stack([mod_x, mod_c], axis=1)
    daq, dak, davt, mq, mk, mvt = _odd_in(x, ctx, mod2, p["gain_mix"], p["w_in"], p["q_a_gain"], p["w_uq"], p["kv_a_gain"],
                                          p["w_ukv"], p["dq_gain"], p["dk_gain"], p["mq_gain"], p["mk_gain"], cos_t, sin_t)
    da = _diff_attention(daq, dak, davt, lq, p["da_lambda"], p["sub_gain"], lambda_init)
    ml = _mla_attention(mq, mk, mvt, lq)
    return _mix_ffn(x, da, ml, mod_x, p["gain_ffn"], p["w_out"], p["ffn_w_in"], p["ffn_w_out"])


def _even_params(i, j, norm_mix, norm_ffn, ffn_w_in, ffn_w_out, ev_w_in, ev_conv_w, ev_conv_b, dn_a_log, dn_dt_bias,
                 dn_norm, ssd_a_log, ssd_dt_bias, ssd_d, ssd_norm, ev_w_out):
    d = norm_mix.shape[1]
    w = ev_w_in[j]
    small = w[:, EV_CONV + EV_GATE:]
    nh, ns = DN_HEADS, SSD_HEADS
    w_small = jnp.concatenate([small, small[:, 4 * nh:], jnp.zeros((d, LANE - 4 * nh - 4 * ns), F32)], axis=1)
    zeros = lambda n: jnp.zeros((n,), F32)
    gate_bias = jnp.concatenate([zeros(2 * nh), dn_dt_bias[j].reshape(-1), ssd_dt_bias[j].reshape(-1),
                                 ssd_dt_bias[j].reshape(-1), zeros(LANE - 4 * nh - 4 * ns)])
    gate_scale = jnp.concatenate([jnp.ones((2 * nh,), F32), -jnp.exp(dn_a_log[j].reshape(-1)), jnp.ones((2 * ns,), F32),
                                  -jnp.exp(ssd_a_log[j].reshape(-1)), zeros(LANE - 4 * nh - 4 * ns)])
    expand = jnp.repeat(jnp.eye(SSD_GRP, dtype=F32), SSD_HEAD_DIM, axis=1)
    return dict(
        gain_mix=norm_mix[i].reshape(1, d), gain_ffn=norm_ffn[i].reshape(1, d),
        w_main=w[:, :EV_CONV + EV_GATE].astype(BF16), w_small=w_small,
        conv_w=ev_conv_w[j], conv_b=ev_conv_b[j].reshape(1, -1),
        gate_bias=gate_bias.reshape(1, LANE), gate_scale=gate_scale.reshape(1, LANE), expand=expand,
        dn_gain=dn_norm[j].reshape(1, DN_DV), ssd_d=jnp.repeat(ssd_d[j], SSD_HEAD_DIM).reshape(1, SSD_INNER),
        ssd_gain=ssd_norm[j].reshape(1, SSD_INNER), w_out=ev_w_out[j].astype(BF16),
        ffn_w_in=ffn_w_in[i].astype(BF16), ffn_w_out=ffn_w_out[i].astype(BF16))


def _odd_params(i, j, norm_mix, norm_ffn, ffn_w_in, ffn_w_out, od_w_in, da_q_norm, da_k_norm, da_lambda, da_sub_norm,
                mla_q_a_norm, mla_w_uq, mla_kv_a_norm, mla_w_ukv, mla_q_norm, mla_k_norm, od_w_out):
    d = norm_mix.shape[1]
    w = od_w_in[j]
    w_in = jnp.concatenate([w, jnp.zeros((d, LANE - MLA_ROPE), F32)], axis=1).astype(BF16)
    hq = mla_w_uq[j].reshape(MLA_Q_RANK, MLA_HEADS, MLA_NOPE + MLA_ROPE)
    hq = jnp.pad(hq, ((0, 0), (0, 0), (0, MLA_QK_PAD - MLA_NOPE - MLA_ROPE)))
    hkv = mla_w_ukv[j].reshape(MLA_KV_RANK, MLA_HEADS, MLA_NOPE + MLA_V)
    w_ukv = jnp.concatenate([hkv[:, :, :MLA_NOPE].reshape(MLA_KV_RANK, -1), hkv[:, :, MLA_NOPE:].reshape(MLA_KV_RANK, -1)], axis=1)
    pad_gain = lambda g: jnp.pad(g, (0, MLA_QK_PAD - MLA_NOPE - MLA_ROPE)).reshape(1, MLA_QK_PAD)
    return dict(
        gain_mix=norm_mix[i].reshape(1, d), gain_ffn=norm_ffn[i].reshape(1, d), w_in=w_in,
        q_a_gain=mla_q_a_norm[j].reshape(1, -1), w_uq=hq.reshape(MLA_Q_RANK, -1).astype(BF16),
        kv_a_gain=mla_kv_a_norm[j].reshape(1, -1), w_ukv=w_ukv.astype(BF16),
        dq_gain=jnp.tile(da_q_norm[j], 2).reshape(1, LANE), dk_gain=jnp.tile(da_k_norm[j], 2).reshape(1, LANE),
        mq_gain=pad_gain(mla_q_norm[j]), mk_gain=pad_gain(mla_k_norm[j]),
        da_lambda=jnp.pad(da_lambda[j], ((0, 4), (0, LANE - DA_DIM))), sub_gain=da_sub_norm[j].reshape(2 * DA_DIM, 1),
        w_out=od_w_out[j].astype(BF16), ffn_w_in=ffn_w_in[i].astype(BF16), ffn_w_out=ffn_w_out[i].astype(BF16))


def kernel(x, c, ctx, c_ctx, ada_w, ada_b, norm_mix, norm_ffn, ffn_w_in, ffn_w_out, ev_w_in, ev_conv_w, ev_conv_b, dn_a_log, dn_dt_bias, dn_norm, ssd_a_log, ssd_dt_bias, ssd_d, ssd_norm, ev_w_out, od_w_in, da_q_norm, da_k_norm, da_lambda, da_sub_norm, mla_q_a_norm, mla_w_uq, mla_kv_a_norm, mla_w_ukv, mla_q_norm, mla_k_norm, od_w_out):
    b, _, d = x.shape
    depth = ada_w.shape[0]
    assert b < 8
    cvec = jnp.concatenate([c, c_ctx[None, :], jnp.zeros((8 - b - 1, d), F32)], axis=0)
    mod = _modulation(cvec, ada_w, ada_b)
    for i in range(depth):
        last = i == depth - 1
        j = i // 2
        mod_x = mod[i, :b].reshape(b, 1, 6 * d)
        mod_c = jnp.broadcast_to(mod[i, b].reshape(1, 1, 6 * d), (b, 1, 6 * d))
        if i % 2 == 0:
            p = _even_params(i, j, norm_mix, norm_ffn, ffn_w_in, ffn_w_out, ev_w_in, ev_conv_w, ev_conv_b, dn_a_log,
                             dn_dt_bias, dn_norm, ssd_a_log, ssd_dt_bias, ssd_d, ssd_norm, ev_w_out)
            x, ctx_new = _even_layer(x, ctx, mod_x, mod_c, p)
        else:
            p = _odd_params(i, j, norm_mix, norm_ffn, ffn_w_in, ffn_w_out, od_w_in, da_q_norm, da_k_norm, da_lambda,
                            da_sub_norm, mla_q_a_norm, mla_w_uq, mla_kv_a_norm, mla_w_ukv, mla_q_norm, mla_k_norm, od_w_out)
            lambda_init = 0.8 - 0.6 * math.exp(-0.3 * i)
            if last:
                x = _odd_layer(x, ctx, mod_x, mod_c, p, lambda_init)
                ctx_new = ctx
            else:
                raise NotImplementedError("context update after an attention layer is not needed for depth 2")
        ctx = ctx_new
    return x
```
